```python
import jax
import jax.numpy as jnp
from jax import lax
import numpy as np

D_MODEL = 1024
BATCH = 8
SEQ = 2048
DEPTH = 2

CHUNK = 64
Q_BLOCK = 128
ROPE_BASE = 10000.0
LN_EPS = 1e-5
NEG = -1e30

RET_HEADS = 4
RET_DIM = 64
RET_W = RET_HEADS * RET_DIM
MLA_HEADS = 8
MLA_NOPE = 64
MLA_ROPE = 32
MLA_V = 64
MLA_Q_RANK = 256
MLA_KV_RANK = 128
MLA_W = MLA_HEADS * MLA_V
MLSTM_HEADS = 4
MLSTM_DIM = 64
MLSTM_W = MLSTM_HEADS * MLSTM_DIM
MLSTM_CONV = 4
MIX_W = RET_W + MLA_W + MLSTM_W

IN_SIZES = (RET_W, RET_W, RET_W, RET_W, MLA_Q_RANK, MLA_KV_RANK, MLA_ROPE, MLSTM_W, MLSTM_W, MLSTM_W, MLSTM_W, MLSTM_HEADS, MLSTM_HEADS)
IN_IS_VALUE = (False, False, True, False, False, False, False, False, False, True, False, False, False)
D_IN = sum(IN_SIZES)

D_FF = 2816
N_EXPERTS = 8
TOP_K = 2
D_FF_EXPERT = 3584
N_DENSE = (DEPTH + 1) // 2
N_MOE = DEPTH // 2

DN_ALPHA = (2 * DEPTH) ** 0.25
DN_BETA = (8 * DEPTH) ** -0.25

kernel_name = 'hybrid_retention_mla_mlstm_moe_block'


def layer_norm(x, w, b):
    x32 = x.astype(jnp.float32)
    mu = x32.mean(-1, keepdims=True)
    var = jnp.square(x32 - mu).mean(-1, keepdims=True)
    return ((x32 - mu) * lax.rsqrt(var + LN_EPS) * w + b).astype(x.dtype)


def rms_norm(x, w):
    x32 = x.astype(jnp.float32)
    y = x32 * lax.rsqrt(jnp.square(x32).mean(-1, keepdims=True) + LN_EPS) * w
    return y.astype(x.dtype)


def head_norm(x, w, n_heads):
    b, s, wd = x.shape
    xh = x.astype(jnp.float32).reshape(b, s, n_heads, wd // n_heads)
    mu = xh.mean(-1, keepdims=True)
    var = jnp.square(xh - mu).mean(-1, keepdims=True)
    return ((xh - mu) * lax.rsqrt(var + LN_EPS)).reshape(b, s, wd) * w


def to_heads(x, n):
    b, s, w = x.shape
    return x.reshape(b, s, n, w // n).transpose(0, 2, 1, 3)


def from_heads(x):
    b, h, s, d = x.shape
    return x.transpose(0, 2, 1, 3).reshape(b, s, h * d)


def rope(x, pos):
    half = x.shape[-1] // 2
    inv = ROPE_BASE ** (-jnp.arange(half, dtype=jnp.float32) / half)
    ang = pos[:, None] * inv[None, :]
    cos, sin = jnp.cos(ang), jnp.sin(ang)
    x32 = x.astype(jnp.float32)
    x1, x2 = x32[..., :half], x32[..., half:]
    return jnp.concatenate([x1 * cos - x2 * sin, x1 * sin + x2 * cos], -1).astype(x.dtype)


def retention(q, k, v):
    b, h, s, d = q.shape
    L = CHUNK
    nc = s // L
    log_gamma = jnp.log(1.0 - 2.0 ** (-5.0 - jnp.arange(h, dtype=jnp.float32)))
    idx = jnp.arange(L, dtype=jnp.float32)
    diff = idx[:, None] - idx[None, :]
    inner_decay = jnp.where(diff >= 0, jnp.exp(diff[None] * log_gamma[:, None, None]), 0.0)
    qc = q.astype(jnp.float32).reshape(b, h, nc, L, d)
    kc = (k.astype(jnp.float32) * d ** -0.5).reshape(b, h, nc, L, d)
    vc = v.astype(jnp.float32).reshape(b, h, nc, L, d)
    scores = jnp.einsum('bhcld,bhcmd->bhclm', qc, kc) * inner_decay[None, :, None]
    inner = jnp.einsum('bhclm,bhcme->bhcle', scores, vc)
    kv_w = jnp.exp((L - 1 - idx)[None, :] * log_gamma[:, None])
    local = jnp.einsum('bhcmd,bhcme->bhcde', kc * kv_w[None, :, None, :, None], vc)
    chunk_decay = jnp.exp(L * log_gamma)[None, :, None, None]

    def step(state, loc):
        return chunk_decay * state + loc, state

    _, r_prev = lax.scan(step, jnp.zeros((b, h, d, d), jnp.float32), jnp.moveaxis(local, 2, 0))
    r_prev = jnp.moveaxis(r_prev, 0, 2)
    q_w = jnp.exp((idx + 1.0)[None, :] * log_gamma[:, None])
    cross = jnp.einsum('bhcld,bhcde->bhcle', qc, r_prev) * q_w[None, :, None, :, None]
    return (inner + cross).reshape(b, h, s, d)


def mlstm(q, k, v, ig, fg):
    b, h, s, d = q.shape
    L = CHUNK
    nc = s // L
    qc = q.astype(jnp.float32).reshape(b, h, nc, L, d)
    kc = (k.astype(jnp.float32) * d ** -0.5).reshape(b, h, nc, L, d)
    vc = v.astype(jnp.float32).reshape(b, h, nc, L, d)
    ic = ig.astype(jnp.float32).reshape(b, h, nc, L)
    lf = jax.nn.log_sigmoid(fg.astype(jnp.float32)).reshape(b, h, nc, L)
    bcum = jnp.cumsum(lf, axis=-1)
    g = bcum[..., -1]
    a = g[..., None] - bcum + ic
    amax = a.max(-1)
    wa = jnp.exp(a - amax[..., None])
    loc_c = jnp.einsum('bhcl,bhcld,bhcle->bhcde', wa, kc, vc)
    loc_n = jnp.einsum('bhcl,bhcld->bhcd', wa, kc)

    def step(carry, xs):
        c_s, n_s, m_s = carry
        g_c, amax_c, lc, ln = xs
        m_new = jnp.maximum(g_c + m_s, amax_c)
        sp = jnp.exp(g_c + m_s - m_new)
        sl = jnp.exp(amax_c - m_new)
        c_new = sp[..., None, None] * c_s + sl[..., None, None] * lc
        n_new = sp[..., None] * n_s + sl[..., None] * ln
        return (c_new, n_new, m_new), (c_s, n_s, m_s)

    init = (jnp.zeros((b, h, d, d), jnp.float32), jnp.zeros((b, h, d), jnp.float32), jnp.full((b, h), NEG, jnp.float32))
    xs = (jnp.moveaxis(g, 2, 0), jnp.moveaxis(amax, 2, 0), jnp.moveaxis(loc_c, 2, 0), jnp.moveaxis(loc_n, 2, 0))
    _, (c_prev, n_prev, m_prev) = lax.scan(step, init, xs)
    c_prev = jnp.moveaxis(c_prev, 0, 2)
    n_prev = jnp.moveaxis(n_prev, 0, 2)
    m_prev = jnp.moveaxis(m_prev, 0, 2)
    idx = jnp.arange(L)
    causal = idx[:, None] >= idx[None, :]
    log_d = jnp.where(causal, bcum[..., :, None] - bcum[..., None, :] + ic[..., None, :], NEG)
    inter = bcum + m_prev[..., None]
    m_t = jnp.maximum(inter, log_d.max(-1))
    dw = jnp.exp(log_d - m_t[..., None])
    cmat = jnp.einsum('bhcld,bhcmd->bhclm', qc, kc) * dw
    si = jnp.exp(inter - m_t)
    num = jnp.einsum('bhclm,bhcme->bhcle', cmat, vc) + si[..., None] * jnp.einsum('bhcld,bhcde->bhcle', qc, c_prev)
    den = cmat.sum(-1) + si * jnp.einsum('bhcld,bhcd->bhcl', qc, n_prev)
    hcell = num / jnp.maximum(jnp.abs(den), jnp.exp(-m_t))[..., None]
    return hcell.reshape(b, h, s, d)


def mla_attention(q_nope, q_rope, k_nope, k_rope, v):
    b, h, s, dn = q_nope.shape
    dr = q_rope.shape[-1]
    nq = s // Q_BLOCK
    scale = (dn + dr) ** -0.5
    kpos = jnp.arange(s)
    qn_b = q_nope.reshape(b, h, nq, Q_BLOCK, dn).transpose(2, 0, 1, 3, 4)
    qr_b = q_rope.reshape(b, h, nq, Q_BLOCK, dr).transpose(2, 0, 1, 3, 4)

    def block(args):
        qn, qr, i = args
        sc = jnp.einsum('bhqd,bhkd->bhqk', qn, k_nope) + jnp.einsum('bhqr,bkr->bhqk', qr, k_rope)
        sc = sc.astype(jnp.float32) * scale
        qpos = i * Q_BLOCK + jnp.arange(Q_BLOCK)
        limit = (qpos // CHUNK + 1) * CHUNK
        sc = jnp.where(kpos[None, :] < limit[:, None], sc, NEG)
        p = jax.nn.softmax(sc, axis=-1).astype(v.dtype)
        return jnp.einsum('bhqk,bhkd->bhqd', p, v)

    out = lax.map(block, (qn_b, qr_b, jnp.arange(nq)))
    return out.transpose(1, 2, 0, 3, 4).reshape(b, h, s, v.shape[-1])


def causal_dwconv(x, w, bias):
    kw, c = w.shape
    y = lax.conv_general_dilated(x, w.reshape(kw, 1, c).astype(x.dtype), window_strides=(1,), padding=[(kw - 1, 0)], dimension_numbers=('NWC', 'WIO', 'NWC'), feature_group_count=c)
    return y + bias


def hybrid_mixer(x, pos, w_in, ret_gn_w, mla_q_norm_w, mla_w_uq, mla_kv_norm_w, mla_w_ukv, conv_w, conv_b, b_i, b_f, mlstm_gn_w, w_out):
    bsz, s, _ = x.shape
    hcat = x @ w_in
    offsets = np.cumsum(IN_SIZES)[:-1].tolist()
    (r_q, r_k, r_v, r_g, c_q, c_kv, k_r, m_q, m_k, m_v, m_o, m_i, m_f) = jnp.split(hcat, offsets, axis=-1)
    rq = rope(to_heads(r_q, RET_HEADS), pos)
    rk = rope(to_heads(r_k, RET_HEADS), pos)
    ret = retention(rq, rk, to_heads(r_v, RET_HEADS))
    ret_out = jax.nn.silu(r_g.astype(jnp.float32)) * head_norm(from_heads(ret), ret_gn_w, RET_HEADS)
    qf = (rms_norm(c_q, mla_q_norm_w) @ mla_w_uq).reshape(bsz, s, MLA_HEADS, MLA_NOPE + MLA_ROPE).transpose(0, 2, 1, 3)
    q_nope, q_rope = qf[..., :MLA_NOPE], rope(qf[..., MLA_NOPE:], pos)
    kvf = (rms_norm(c_kv, mla_kv_norm_w) @ mla_w_ukv).reshape(bsz, s, MLA_HEADS, MLA_NOPE + MLA_V).transpose(0, 2, 1, 3)
    k_nope, v_mla = kvf[..., :MLA_NOPE], kvf[..., MLA_NOPE:]
    k_rope = rope(k_r, pos)
    att_out = from_heads(mla_attention(q_nope, q_rope, k_nope, k_rope, v_mla))
    qk = jax.nn.silu(causal_dwconv(jnp.concatenate([m_q, m_k], -1), conv_w, conv_b))
    mq, mk = qk[..., :MLSTM_W], qk[..., MLSTM_W:]
    ig = (m_i + b_i).transpose(0, 2, 1)
    fg = (m_f + b_f).transpose(0, 2, 1)
    hcell = mlstm(to_heads(mq, MLSTM_HEADS), to_heads(mk, MLSTM_HEADS), to_heads(m_v, MLSTM_HEADS), ig, fg)
    h_gated = jax.nn.sigmoid(m_o.astype(jnp.float32)) * from_heads(hcell)
    mlstm_out = head_norm(h_gated, mlstm_gn_w, MLSTM_HEADS)
    merged = jnp.concatenate([ret_out.astype(x.dtype), att_out.astype(x.dtype), mlstm_out.astype(x.dtype)], -1)
    return merged @ w_out


def swiglu(x, wg, wu, wd):
    return (jax.nn.silu(x @ wg) * (x @ wu)) @ wd


def moe_swiglu(x, router, wg, wu, wd):
    bsz, s, d = x.shape
    t = x.reshape(bsz * s, d)
    logits = (t @ router).astype(jnp.float32)
    top_v, top_i = lax.top_k(logits, TOP_K)
    top_w = jax.nn.softmax(top_v, axis=-1)
    gate = jnp.sum(jax.nn.one_hot(top_i, N_EXPERTS, dtype=jnp.float32) * top_w[..., None], axis=1)
    y = jnp.zeros((bsz * s, d), jnp.float32)
    for e in range(N_EXPERTS):
        y = y + gate[:, e:e + 1] * swiglu(t, wg[e], wu[e], wd[e]).astype(jnp.float32)
    return y.astype(x.dtype).reshape(bsz, s, d)


def setup_inputs(seed: int = 0) -> dict:
    key = jax.random.key(seed)
    ks = jax.random.split(key, 24)
    L = DEPTH

    def nrm(k, shape, scale):
        return jax.random.normal(k, shape, jnp.float32) * scale

    col_scale = np.concatenate([np.full(n, DN_BETA if isv else 1.0, np.float32) for n, isv in zip(IN_SIZES, IN_IS_VALUE)])
    ukv_scale = np.tile(np.concatenate([np.ones(MLA_NOPE, np.float32), np.full(MLA_V, DN_BETA, np.float32)]), MLA_HEADS)
    return {
        'x': nrm(ks[0], (BATCH, SEQ, D_MODEL), 1.0),
        'w_in': nrm(ks[1], (L, D_MODEL, D_IN), D_MODEL ** -0.5) * jnp.asarray(col_scale),
        'ret_gn_w': 1.0 + nrm(ks[2], (L, RET_W), 0.01),
        'mla_q_norm_w': 1.0 + nrm(ks[3], (L, MLA_Q_RANK), 0.01),
        'mla_w_uq': nrm(ks[4], (L, MLA_Q_RANK, MLA_HEADS * (MLA_NOPE + MLA_ROPE)), MLA_Q_RANK ** -0.5),
        'mla_kv_norm_w': 1.0 + nrm(ks[5], (L, MLA_KV_RANK), 0.01),
        'mla_w_ukv': nrm(ks[6], (L, MLA_KV_RANK, MLA_HEADS * (MLA_NOPE + MLA_V)), MLA_KV_RANK ** -0.5) * jnp.asarray(ukv_scale),
        'mlstm_conv_w': nrm(ks[7], (L, MLSTM_CONV, 2 * MLSTM_W), MLSTM_CONV ** -0.5),
        'mlstm_conv_b': nrm(ks[8], (L, 2 * MLSTM_W), 0.01),
        'mlstm_b_i': nrm(ks[9], (L, MLSTM_HEADS), 0.1),
        'mlstm_b_f': jnp.linspace(3.0, 6.0, MLSTM_HEADS, dtype=jnp.float32)[None, :] + nrm(ks[10], (L, MLSTM_HEADS), 0.01),
        'mlstm_gn_w': 1.0 + nrm(ks[11], (L, MLSTM_W), 0.01),
        'w_out': nrm(ks[12], (L, MIX_W, D_MODEL), MIX_W ** -0.5 * DN_BETA),
        'ln1_w': 1.0 + nrm(ks[13], (L, D_MODEL), 0.01),
        'ln1_b': nrm(ks[14], (L, D_MODEL), 0.01),
        'ffn_w_gate': nrm(ks[15], (N_DENSE, D_MODEL, D_FF), D_MODEL ** -0.5),
        'ffn_w_up': nrm(ks[16], (N_DENSE, D_MODEL, D_FF), D_MODEL ** -0.5 * DN_BETA),
        'ffn_w_down': nrm(ks[17], (N_DENSE, D_FF, D_MODEL), D_FF ** -0.5 * DN_BETA),
        'moe_router': nrm(ks[18], (N_MOE, D_MODEL, N_EXPERTS), D_MODEL ** -0.5),
        'moe_w_gate': nrm(ks[19], (N_MOE, N_EXPERTS, D_MODEL, D_FF_EXPERT), D_MODEL ** -0.5),
        'moe_w_up': nrm(ks[20], (N_MOE, N_EXPERTS, D_MODEL, D_FF_EXPERT), D_MODEL ** -0.5 * DN_BETA),
        'moe_w_down': nrm(ks[21], (N_MOE, N_EXPERTS, D_FF_EXPERT, D_MODEL), D_FF_EXPERT ** -0.5 * DN_BETA),
        'ln2_w': 1.0 + nrm(ks[22], (L, D_MODEL), 0.01),
        'ln2_b': nrm(ks[23], (L, D_MODEL), 0.01),
    }


def reference(x, w_in, ret_gn_w, mla_q_norm_w, mla_w_uq, mla_kv_norm_w, mla_w_ukv, mlstm_conv_w, mlstm_conv_b, mlstm_b_i, mlstm_b_f, mlstm_gn_w, w_out, ln1_w, ln1_b, ffn_w_gate, ffn_w_up, ffn_w_down, moe_router, moe_w_gate, moe_w_up, moe_w_down, ln2_w, ln2_b):
    pos = jnp.arange(x.shape[1], dtype=jnp.float32)
    for l in range(DEPTH):
        mix = hybrid_mixer(x, pos, w_in[l], ret_gn_w[l], mla_q_norm_w[l], mla_w_uq[l], mla_kv_norm_w[l], mla_w_ukv[l], mlstm_conv_w[l], mlstm_conv_b[l], mlstm_b_i[l], mlstm_b_f[l], mlstm_gn_w[l], w_out[l])
        x = layer_norm(DN_ALPHA * x + mix, ln1_w[l], ln1_b[l])
        if l % 2 == 0:
            f = swiglu(x, ffn_w_gate[l // 2], ffn_w_up[l // 2], ffn_w_down[l // 2])
        else:
            f = moe_swiglu(x, moe_router[l // 2], moe_w_gate[l // 2], moe_w_up[l // 2], moe_w_down[l // 2])
        x = layer_norm(DN_ALPHA * x + f, ln2_w[l], ln2_b[l])
    return x
```

```python
import functools

import numpy as np
import jax
import jax.numpy as jnp
from jax import lax
from jax.experimental import pallas as pl
from jax.experimental.pallas import tpu as pltpu

F32 = jnp.float32
BF16 = jnp.bfloat16

D_MODEL = 1024
DEPTH = 2
ROPE_BASE = 10000.0
LN_EPS = 1e-5
NEG = -1e30

HEAD_DIM = 64
N_RHEADS = 4
REC_W = N_RHEADS * HEAD_DIM
MLA_HEADS = 8
MLA_NOPE = 64
MLA_ROPE = 32
MLA_V = 64
MLA_Q_RANK = 256
MLA_KV_RANK = 128
MLA_W = MLA_HEADS * MLA_V
MLSTM_CONV = 4
MASK_CHUNK = 64
REC_CHUNK = 256
N_EXPERTS = 8
LANES = 128

DN_ALPHA = (2 * DEPTH) ** 0.25

COL_RET = 0
COL_MLSTM = 1024
COL_CQ = 2048
COL_CKV = 2304
COL_MISC = 2432
D_IN_PAD = 2560
MISC_KR = 64

VMEM_LIMIT = 56 * 1024 * 1024


def _cparams(*sem):
    return pltpu.CompilerParams(dimension_semantics=sem, vmem_limit_bytes=VMEM_LIMIT)


def _layer_norm(y, w, b):
    mu = jnp.mean(y, axis=-1, keepdims=True)
    d = y - mu
    var = jnp.mean(d * d, axis=-1, keepdims=True)
    return d * lax.rsqrt(var + LN_EPS) * w + b


def _silu(x):
    return x * (1.0 / (1.0 + jnp.exp(-x)))


def _sigmoid(x):
    return 1.0 / (1.0 + jnp.exp(-x))


def _split3(x):
    hi = x.astype(BF16)
    r1 = x - hi.astype(F32)
    mid = r1.astype(BF16)
    lo = (r1 - mid.astype(F32)).astype(BF16)
    return hi, mid, lo


def _dot(a, b):
    return jnp.dot(a, b, preferred_element_type=F32)


def _dot_nt(a, b):
    return lax.dot_general(a, b, (((1,), (1,)), ((), ())), preferred_element_type=F32)


def _dot_exact_rhs(x, m_bf16):
    hi, mid, lo = _split3(x)
    return _dot(hi, m_bf16) + _dot(mid, m_bf16) + _dot(lo, m_bf16)


def _swap_halves(x, half):
    n = x.shape[-1]
    lane = lax.broadcasted_iota(jnp.int32, x.shape, x.ndim - 1)
    first = (lane % (2 * half)) < half
    return jnp.where(first, pltpu.roll(x, n - half, x.ndim - 1), pltpu.roll(x, half, x.ndim - 1))


def _head_norm(x, avg_bf16, w):
    mu = _dot_exact_rhs(x, avg_bf16)
    d = x - mu
    var = _dot_exact_rhs(d * d, avg_bf16)
    return d * lax.rsqrt(var + LN_EPS) * w


def _expand_heads(v, lane_head):
    out = jnp.zeros((v.shape[0], REC_W), F32)
    for h in range(N_RHEADS):
        out = jnp.where(lane_head == h, v[:, h:h + 1], out)
    return out


def _rope_tables(seq):
    pos = np.arange(seq, dtype=np.float64)[:, None]
    half = HEAD_DIM // 2
    inv = ROPE_BASE ** (-np.arange(half, dtype=np.float64) / half)
    ang = pos * inv[None, :]
    cos_r = np.tile(np.concatenate([np.cos(ang), np.cos(ang)], -1), (1, N_RHEADS))
    sin_r = np.tile(np.concatenate([-np.sin(ang), np.sin(ang)], -1), (1, N_RHEADS))
    half = MLA_ROPE // 2
    inv = ROPE_BASE ** (-np.arange(half, dtype=np.float64) / half)
    ang = pos * inv[None, :]
    cos_m = np.ones((seq, LANES))
    sin_m = np.zeros((seq, LANES))
    cos_m[:, MISC_KR:MISC_KR + MLA_ROPE] = np.concatenate([np.cos(ang), np.cos(ang)], -1)
    sin_m[:, MISC_KR:MISC_KR + MLA_ROPE] = np.concatenate([-np.sin(ang), np.sin(ang)], -1)
    f = lambda a: jnp.asarray(a.astype(np.float32))
    return f(cos_r), f(sin_r), f(cos_m), f(sin_m)


def _retention_tables():
    L = REC_CHUNK
    log_gamma = np.log(1.0 - 2.0 ** (-5.0 - np.arange(N_RHEADS, dtype=np.float64)))
    idx = np.arange(L, dtype=np.float64)
    diff = idx[:, None] - idx[None, :]
    dmask = np.where(diff >= 0, np.exp(diff[None] * log_gamma[:, None, None]), 0.0)
    lane_lg = np.repeat(log_gamma, HEAD_DIM)[None, :]
    qw = np.exp((idx + 1.0)[:, None] * lane_lg)
    kw = np.exp((L - 1 - idx)[:, None] * lane_lg)
    cd = np.exp(L * lane_lg)
    f = lambda a: jnp.asarray(a.astype(np.float32))
    return f(dmask), f(qw), f(kw), f(cd)


def _head_avg_matrix():
    h = np.arange(REC_W) // HEAD_DIM
    return jnp.asarray((h[:, None] == h[None, :]).astype(np.float32) / HEAD_DIM, dtype=BF16)


def _inproj_kernel(x_ref, w_ref, o_ref):
    o_ref[...] = _dot(x_ref[...].astype(BF16), w_ref[...])


def _inproj(x2, w_bf16, tm=512):
    t, k = x2.shape
    n = w_bf16.shape[1]
    return pl.pallas_call(
        _inproj_kernel,
        grid=(t // tm,),
        in_specs=[pl.BlockSpec((tm, k), lambda i: (i, 0)),
                  pl.BlockSpec((k, n), lambda i: (0, 0))],
        out_specs=pl.BlockSpec((tm, n), lambda i: (i, 0)),
        out_shape=jax.ShapeDtypeStruct((t, n), F32),
        compiler_params=_cparams("parallel"),
        name="inproj",
    )(x2, w_bf16)


def _retention_kernel(h_ref, cos_ref, sin_ref, dmask_ref, qw_ref, kw_ref, cd_ref, avg_ref, gnw_ref,
                      o_ref, state_ref):
    L = REC_CHUNK
    nc = h_ref.shape[0] // L
    lane_head = lax.broadcasted_iota(jnp.int32, (1, REC_W), 1) // HEAD_DIM
    row_head = lax.broadcasted_iota(jnp.int32, (REC_W, REC_W), 0) // HEAD_DIM
    col_head = lax.broadcasted_iota(jnp.int32, (REC_W, REC_W), 1) // HEAD_DIM
    block_diag = row_head == col_head
    state_ref[...] = jnp.zeros_like(state_ref)

    def chunk(c, carry):
        r0 = pl.multiple_of(c * L, L)
        rows = pl.ds(r0, L)
        cos = cos_ref[rows, :]
        sin = sin_ref[rows, :]
        q = h_ref[rows, 0:REC_W]
        k = h_ref[rows, REC_W:2 * REC_W]
        v = h_ref[rows, 2 * REC_W:3 * REC_W]
        g = h_ref[rows, 3 * REC_W:4 * REC_W]
        q = q * cos + _swap_halves(q, HEAD_DIM // 2) * sin
        k = (k * cos + _swap_halves(k, HEAD_DIM // 2) * sin) * (HEAD_DIM ** -0.5)
        kb = k.astype(BF16)
        vb = v.astype(BF16)
        inner = jnp.zeros((L, REC_W), F32)
        for h in range(N_RHEADS):
            mh = lane_head == h
            qh = jnp.where(mh, q, 0.0).astype(BF16)
            s = _dot_nt(qh, kb) * dmask_ref[h]
            o = _dot(s.astype(BF16), vb)
            inner = jnp.where(mh, o, inner)
        state = state_ref[...]
        cross = _dot(q.astype(BF16), state.astype(BF16)) * qw_ref[...]
        ret = inner + cross
        kwt = (k * kw_ref[...]).T.astype(BF16)
        loc = _dot(kwt, vb)
        state_ref[...] = cd_ref[...] * state + jnp.where(block_diag, loc, 0.0)
        o_ref[rows, :] = _silu(g) * _head_norm(ret, avg_ref[...], gnw_ref[...])
        return carry

    lax.fori_loop(0, nc, chunk, 0)


def _retention(h3, cos_r, sin_r, gn_w):
    b, s, _ = h3.shape
    dmask, qw, kw, cd = _retention_tables()
    avg = _head_avg_matrix()
    const = lambda shape: pl.BlockSpec(shape, lambda i: (0,) * len(shape))
    return pl.pallas_call(
        _retention_kernel,
        grid=(b,),
        in_specs=[pl.BlockSpec((None, s, 4 * REC_W), lambda i: (i, 0, COL_RET // (4 * REC_W))),
                  const((s, REC_W)), const((s, REC_W)),
                  const((N_RHEADS, REC_CHUNK, REC_CHUNK)), const((REC_CHUNK, REC_W)),
                  const((REC_CHUNK, REC_W)), const((1, REC_W)), const((REC_W, REC_W)), const((1, REC_W))],
        out_specs=pl.BlockSpec((None, s, REC_W), lambda i: (i, 0, 0)),
        out_shape=jax.ShapeDtypeStruct((b, s, REC_W), F32),
        scratch_shapes=[pltpu.VMEM((REC_W, REC_W), F32)],
        compiler_params=_cparams("parallel"),
        name="retention",
    )(h3, cos_r, sin_r, dmask, qw, kw, cd, avg, gn_w.reshape(1, REC_W))


def _mlstm_kernel(h_ref, misc_ref, convw_ref, convb_ref, bif_ref, avg_ref, gnw_ref,
                  o_ref, qk_ref, lf_ref, ic_ref, c_ref, n_ref, m_ref):
    L = REC_CHUNK
    s_len = h_ref.shape[0]
    nc = s_len // L
    lane_head = lax.broadcasted_iota(jnp.int32, (1, REC_W), 1) // HEAD_DIM
    row_head = lax.broadcasted_iota(jnp.int32, (REC_W, REC_W), 0) // HEAD_DIM
    col_head = lax.broadcasted_iota(jnp.int32, (REC_W, REC_W), 1) // HEAD_DIM
    block_diag = row_head == col_head
    n_mask = (lax.broadcasted_iota(jnp.int32, (REC_W, LANES), 0) // HEAD_DIM
              == lax.broadcasted_iota(jnp.int32, (REC_W, LANES), 1))
    lane128 = lax.broadcasted_iota(jnp.int32, (1, LANES), 1)
    gate_lanes = lane128 < N_RHEADS
    ri = lax.broadcasted_iota(jnp.int32, (L, L), 0)
    ci = lax.broadcasted_iota(jnp.int32, (L, L), 1)
    causal = ri >= ci
    tri = causal.astype(BF16)

    x = h_ref[:, 0:2 * REC_W]
    row = lax.broadcasted_iota(jnp.int32, x.shape, 0)
    acc = x * convw_ref[MLSTM_CONV - 1:MLSTM_CONV, :] + convb_ref[...]
    for sft in range(1, MLSTM_CONV):
        xs = jnp.where(row >= sft, pltpu.roll(x, sft, 0), 0.0)
        acc = acc + xs * convw_ref[MLSTM_CONV - 1 - sft:MLSTM_CONV - sft, :]
    qk_ref[...] = _silu(acc)

    gates = misc_ref[...] + bif_ref[...]
    ic_ref[...] = jnp.where(gate_lanes, gates, 0.0)
    fg = pltpu.roll(gates, LANES - N_RHEADS, 1)
    lf = jnp.minimum(fg, 0.0) - jnp.log1p(jnp.exp(-jnp.abs(fg)))
    lf_ref[...] = jnp.where(gate_lanes, lf, 0.0)

    c_ref[...] = jnp.zeros_like(c_ref)
    n_ref[...] = jnp.zeros_like(n_ref)
    m_ref[...] = jnp.full(m_ref.shape, NEG, F32)

    def chunk(c, carry):
        r0 = pl.multiple_of(c * L, L)
        rows = pl.ds(r0, L)
        q = qk_ref[rows, 0:REC_W]
        k = qk_ref[rows, REC_W:2 * REC_W] * (HEAD_DIM ** -0.5)
        v = h_ref[rows, 2 * REC_W:3 * REC_W]
        og = h_ref[rows, 3 * REC_W:4 * REC_W]
        lf_c = lf_ref[rows, :]
        ic_c = ic_ref[rows, :]
        qb = q.astype(BF16)
        kb = k.astype(BF16)
        vb = v.astype(BF16)

        bcum = _dot_exact_rhs_left(tri, lf_c)
        gsum = bcum[L - 1:L, :]
        m_s = m_ref[...]
        a = gsum - bcum + ic_c
        amax = jnp.max(a, axis=0, keepdims=True)
        rvec = (ic_c - bcum).T
        inter = bcum + m_s

        c_state = c_ref[...]
        n_state = n_ref[...]
        q_c = _dot(qb, c_state.astype(BF16))
        q_n = _dot(qb, n_state.astype(BF16))

        hcell = jnp.zeros((L, REC_W), F32)
        for h in range(N_RHEADS):
            mh = lane_head == h
            log_d = jnp.where(causal, bcum[:, h:h + 1] + rvec[h:h + 1, :], NEG)
            inter_h = inter[:, h:h + 1]
            m_t = jnp.maximum(inter_h, jnp.max(log_d, axis=-1, keepdims=True))
            dw = jnp.exp(log_d - m_t)
            qh = jnp.where(mh, q, 0.0).astype(BF16)
            cmat = _dot_nt(qh, kb) * dw
            si = jnp.exp(inter_h - m_t)
            num = _dot(cmat.astype(BF16), vb) + si * q_c
            den = jnp.sum(cmat, axis=-1, keepdims=True) + si * q_n[:, h:h + 1]
            denom = jnp.maximum(jnp.abs(den), jnp.exp(-m_t))
            hcell = jnp.where(mh, num / denom, hcell)

        wa = jnp.exp(a - amax)
        m_new = jnp.maximum(gsum + m_s, amax)
        sp = jnp.exp(gsum + m_s - m_new)
        sl = jnp.exp(amax - m_new)
        kwt = (k * _expand_heads(wa, lane_head)).T.astype(BF16)
        loc_c = _dot(kwt, vb)
        loc_n = _dot(kwt, jnp.ones((L, LANES), BF16))
        c_ref[...] = (c_state * _expand_heads(sp, lane_head)
                      + jnp.where(block_diag, loc_c, 0.0) * _expand_heads(sl, lane_head))
        n_ref[...] = n_state * sp + jnp.where(n_mask, loc_n, 0.0) * sl
        m_ref[...] = m_new

        o_ref[rows, :] = _head_norm(_sigmoid(og) * hcell, avg_ref[...], gnw_ref[...])
        return carry

    lax.fori_loop(0, nc, chunk, 0)


def _dot_exact_rhs_left(m_bf16, x):
    hi, mid, lo = _split3(x)
    return _dot(m_bf16, hi) + _dot(m_bf16, mid) + _dot(m_bf16, lo)


def _mlstm(h3, conv_w, conv_b, b_i, b_f, gn_w):
    b, s, _ = h3.shape
    avg = _head_avg_matrix()
    bif = jnp.zeros((1, LANES), F32).at[0, 0:N_RHEADS].set(b_i).at[0, N_RHEADS:2 * N_RHEADS].set(b_f)
    const = lambda shape: pl.BlockSpec(shape, lambda i: (0,) * len(shape))
    return pl.pallas_call(
        _mlstm_kernel,
        grid=(b,),
        in_specs=[pl.BlockSpec((None, s, 4 * REC_W), lambda i: (i, 0, COL_MLSTM // (4 * REC_W))),
                  pl.BlockSpec((None, s, LANES), lambda i: (i, 0, COL_MISC // LANES)),
                  const((MLSTM_CONV, 2 * REC_W)), const((1, 2 * REC_W)), const((1, LANES)),
                  const((REC_W, REC_W)), const((1, REC_W))],
        out_specs=pl.BlockSpec((None, s, REC_W), lambda i: (i, 0, 0)),
        out_shape=jax.ShapeDtypeStruct((b, s, REC_W), F32),
        scratch_shapes=[pltpu.VMEM((s, 2 * REC_W), F32), pltpu.VMEM((s, LANES), F32),
                        pltpu.VMEM((s, LANES), F32), pltpu.VMEM((REC_W, REC_W), F32),
                        pltpu.VMEM((REC_W, LANES), F32), pltpu.VMEM((1, LANES), F32)],
        compiler_params=_cparams("parallel"),
        name="mlstm",
    )(h3, h3, conv_w, conv_b.reshape(1, -1), bif, avg, gn_w.reshape(1, REC_W))


def _mla_prep_kernel(cq_ref, ckv_ref, cos_ref, sin_ref, qnw_ref, kvnw_ref, wuq_ref, wuk_ref, wuv_ref,
                     q_ref, k_ref, v_ref):
    scale = (MLA_NOPE + MLA_ROPE) ** -0.5
    cos = cos_ref[...]
    sin = sin_ref[...]
    cos_all = jnp.concatenate([cos] * MLA_HEADS, axis=1)
    sin_all = jnp.concatenate([sin] * MLA_HEADS, axis=1)

    cq = cq_ref[...]
    cqn = cq * lax.rsqrt(jnp.mean(cq * cq, axis=-1, keepdims=True) + LN_EPS) * qnw_ref[...]
    q = _dot(cqn.astype(BF16), wuq_ref[...])
    q = q * cos_all + _swap_halves(q, MLA_ROPE // 2) * sin_all
    q_ref[...] = (q * scale).astype(BF16)

    ckv = ckv_ref[:, 0:MLA_KV_RANK]
    kvn = ckv * lax.rsqrt(jnp.mean(ckv * ckv, axis=-1, keepdims=True) + LN_EPS) * kvnw_ref[...]
    kvb = kvn.astype(BF16)
    misc = ckv_ref[:, MLA_KV_RANK:MLA_KV_RANK + LANES]
    lane = lax.broadcasted_iota(jnp.int32, (1, LANES), 1)
    kr = misc * cos + _swap_halves(misc, MLA_ROPE // 2) * sin
    kr = jnp.where((lane >= MISC_KR) & (lane < MISC_KR + MLA_ROPE), kr, 0.0)
    k = _dot(kvb, wuk_ref[...]) + jnp.concatenate([kr] * MLA_HEADS, axis=1)
    k_ref[...] = k.astype(BF16)
    v_ref[...] = _dot(kvb, wuv_ref[...]).astype(BF16)


def _mla_prep(h2, seq, cos_m, sin_m, qn_w, kvn_w, wuq, wuk, wuv, tm=512):
    t = h2.shape[0]
    nblk = seq // tm
    hw = MLA_HEADS * LANES
    const = lambda shape: pl.BlockSpec(shape, lambda i: (0,) * len(shape))
    return pl.pallas_call(
        _mla_prep_kernel,
        grid=(t // tm,),
        in_specs=[pl.BlockSpec((tm, MLA_Q_RANK), lambda i: (i, COL_CQ // MLA_Q_RANK)),
                  pl.BlockSpec((tm, 2 * LANES), lambda i: (i, COL_CKV // (2 * LANES))),
                  pl.BlockSpec((tm, LANES), lambda i: (i % nblk, 0)),
                  pl.BlockSpec((tm, LANES), lambda i: (i % nblk, 0)),
                  const((1, MLA_Q_RANK)), const((1, MLA_KV_RANK)),
                  const((MLA_Q_RANK, hw)), const((MLA_KV_RANK, hw)), const((MLA_KV_RANK, hw))],
        out_specs=[pl.BlockSpec((tm, hw), lambda i: (i, 0))] * 3,
        out_shape=[jax.ShapeDtypeStruct((t, hw), BF16)] * 3,
        compiler_params=_cparams("parallel"),
        name="mla_prep",
    )(h2, h2, cos_m, sin_m, qn_w.reshape(1, -1), kvn_w.reshape(1, -1), wuq, wuk, wuv)


def _mla_attn_kernel(q_ref, k_ref, v_ref, o_ref, *, tq):
    qi = pl.program_id(2)
    ri = lax.broadcasted_iota(jnp.int32, (tq, tq), 0)
    ci = lax.broadcasted_iota(jnp.int32, (tq, tq), 1)
    diag_ok = ci < (ri // MASK_CHUNK + 1) * MASK_CHUNK
    qs = [q_ref[:, j * LANES:(j + 1) * LANES] for j in range(2)]

    def step(kt, carry, masked):
        rows = pl.ds(pl.multiple_of(kt * tq, tq), tq)
        new = []
        for j in range(2):
            m, l, acc = carry[j]
            kj = k_ref[rows, j * LANES:(j + 1) * LANES]
            vj = v_ref[rows, j * LANES:(j + 1) * LANES]
            s = _dot_nt(qs[j], kj)
            if masked:
                s = jnp.where(diag_ok, s, NEG)
            m_new = jnp.maximum(m, jnp.max(s, axis=-1, keepdims=True))
            alpha = jnp.exp(m - m_new)
            p = jnp.exp(s - m_new)
            l = alpha * l + jnp.sum(p, axis=-1, keepdims=True)
            acc = alpha * acc + _dot(p.astype(BF16), vj)
            new.append((m_new, l, acc))
        return tuple(new)

    init = tuple((jnp.full((tq, 1), NEG, F32), jnp.zeros((tq, 1), F32), jnp.zeros((tq, LANES), F32))
                 for _ in range(2))
    carry = lax.fori_loop(0, qi, lambda kt, c: step(kt, c, False), init)
    carry = step(qi, carry, True)
    o_ref[...] = carry[0][2] / carry[0][1] + carry[1][2] / carry[1][1]


def _mla_attn(q, k, v, batch, seq, tq=256):
    t = q.shape[0]
    nq = seq // tq
    pair = 2 * LANES
    return pl.pallas_call(
        functools.partial(_mla_attn_kernel, tq=tq),
        grid=(batch, MLA_HEADS // 2, nq),
        in_specs=[pl.BlockSpec((tq, pair), lambda b, p, i: (b * nq + i, p)),
                  pl.BlockSpec((seq, pair), lambda b, p, i: (b, p)),
                  pl.BlockSpec((seq, pair), lambda b, p, i: (b, p))],
        out_specs=pl.BlockSpec((tq, LANES), lambda b, p, i: (b * nq + i, p)),
        out_shape=jax.ShapeDtypeStruct((t, MLA_W), F32),
        compiler_params=_cparams("parallel", "parallel", "arbitrary"),
        name="mla_attn",
    )(q, k, v)


def _outproj_kernel(ret_ref, att_ref, ml_ref, x_ref, w_ref, lnw_ref, lnb_ref, o_ref):
    mix = (_dot(ret_ref[...].astype(BF16), w_ref[0:REC_W, :])
           + _dot(att_ref[...].astype(BF16), w_ref[REC_W:REC_W + MLA_W, :])
           + _dot(ml_ref[...].astype(BF16), w_ref[REC_W + MLA_W:, :]))
    o_ref[...] = _layer_norm(DN_ALPHA * x_ref[...] + mix, lnw_ref[...], lnb_ref[...])


def _outproj_ln(ret, att, ml, x2, w_bf16, ln_w, ln_b, tm=512):
    t = x2.shape[0]
    const = lambda shape: pl.BlockSpec(shape, lambda i: (0,) * len(shape))
    row = lambda w: pl.BlockSpec((tm, w), lambda i: (i, 0))
    return pl.pallas_call(
        _outproj_kernel,
        grid=(t // tm,),
        in_specs=[row(REC_W), row(MLA_W), row(REC_W), row(D_MODEL),
                  const((D_MODEL, D_MODEL)), const((1, D_MODEL)), const((1, D_MODEL))],
        out_specs=row(D_MODEL),
        out_shape=jax.ShapeDtypeStruct((t, D_MODEL), F32),
        compiler_params=_cparams("parallel"),
        name="outproj_ln",
    )(ret, att, ml, x2, w_bf16, ln_w.reshape(1, -1), ln_b.reshape(1, -1))


def _ffn_kernel(x_ref, wg_ref, wu_ref, wd_ref, lnw_ref, lnb_ref, o_ref, xb_ref, acc_ref):
    f = pl.program_id(1)

    @pl.when(f == 0)
    def _():
        xb_ref[...] = x_ref[...].astype(BF16)
        acc_ref[...] = jnp.zeros_like(acc_ref)

    xb = xb_ref[...]
    g = _dot(xb, wg_ref[...])
    u = _dot(xb, wu_ref[...])
    acc_ref[...] += _dot((_silu(g) * u).astype(BF16), wd_ref[...])

    @pl.when(f == pl.num_programs(1) - 1)
    def _():
        o_ref[...] = _layer_norm(DN_ALPHA * x_ref[...] + acc_ref[...], lnw_ref[...], lnb_ref[...])


def _ffn_ln(x2, wg, wu, wd, ln_w, ln_b, tm=1024, tf=256):
    t = x2.shape[0]
    dff = wg.shape[1]
    const = lambda shape: pl.BlockSpec(shape, lambda i, f: (0,) * len(shape))
    return pl.pallas_call(
        _ffn_kernel,
        grid=(t // tm, dff // tf),
        in_specs=[pl.BlockSpec((tm, D_MODEL), lambda i, f: (i, 0)),
                  pl.BlockSpec((D_MODEL, tf), lambda i, f: (0, f)),
                  pl.BlockSpec((D_MODEL, tf), lambda i, f: (0, f)),
                  pl.BlockSpec((tf, D_MODEL), lambda i, f: (f, 0)),
                  const((1, D_MODEL)), const((1, D_MODEL))],
        out_specs=pl.BlockSpec((tm, D_MODEL), lambda i, f: (i, 0)),
        out_shape=jax.ShapeDtypeStruct((t, D_MODEL), F32),
        scratch_shapes=[pltpu.VMEM((tm, D_MODEL), BF16), pltpu.VMEM((tm, D_MODEL), F32)],
        compiler_params=_cparams("parallel", "arbitrary"),
        name="ffn_ln",
    )(x2, wg, wu, wd, ln_w.reshape(1, -1), ln_b.reshape(1, -1))


def _router_kernel(x_ref, r_ref, g_ref):
    xh, xm, xl = _split3(x_ref[...])
    rh, rm, rl = r_ref[0], r_ref[1], r_ref[2]
    logits = (_dot(xh, rh) + _dot(xh, rm) + _dot(xm, rh)
              + _dot(xh, rl) + _dot(xm, rm) + _dot(xl, rh))
    lane = lax.broadcasted_iota(jnp.int32, logits.shape, 1)
    lg = jnp.where(lane < N_EXPERTS, logits, -jnp.inf)
    m1 = jnp.max(lg, axis=-1, keepdims=True)
    i1 = jnp.min(jnp.where(lg == m1, lane, LANES), axis=-1, keepdims=True)
    lg2 = jnp.where(lane == i1, -jnp.inf, lg)
    m2 = jnp.max(lg2, axis=-1, keepdims=True)
    i2 = jnp.min(jnp.where(lg2 == m2, lane, LANES), axis=-1, keepdims=True)
    e2 = jnp.exp(m2 - m1)
    w1 = 1.0 / (1.0 + e2)
    w2 = e2 / (1.0 + e2)
    g_ref[...] = jnp.where(lane == i1, w1, 0.0) + jnp.where(lane == i2, w2, 0.0)


def _router(x2, router, tm=1024):
    t = x2.shape[0]
    rp = jnp.zeros((D_MODEL, LANES), F32).at[:, 0:N_EXPERTS].set(router)
    rh = rp.astype(BF16)
    r1 = rp - rh.astype(F32)
    rm = r1.astype(BF16)
    rl = (r1 - rm.astype(F32)).astype(BF16)
    r3 = jnp.stack([rh, rm, rl])
    return pl.pallas_call(
        _router_kernel,
        grid=(t // tm,),
        in_specs=[pl.BlockSpec((tm, D_MODEL), lambda i: (i, 0)),
                  pl.BlockSpec((3, D_MODEL, LANES), lambda i: (0, 0, 0))],
        out_specs=pl.BlockSpec((tm, LANES), lambda i: (i, 0)),
        out_shape=jax.ShapeDtypeStruct((t, LANES), F32),
        compiler_params=_cparams("parallel"),
        name="router",
    )(x2, r3)


def _moe_kernel(x_ref, gate_ref, wg_ref, wu_ref, wd_ref, lnw_ref, lnb_ref, o_ref, xb_ref, acc_ref):
    e = pl.program_id(1)
    f = pl.program_id(2)

    @pl.when((e == 0) & (f == 0))
    def _():
        xb_ref[...] = x_ref[...].astype(BF16)
        acc_ref[...] = jnp.zeros_like(acc_ref)

    xb = xb_ref[...]
    g = _dot(xb, wg_ref[...].astype(BF16))
    u = _dot(xb, wu_ref[...].astype(BF16))
    y = _dot((_silu(g) * u).astype(BF16), wd_ref[...].astype(BF16))
    lane = lax.broadcasted_iota(jnp.int32, gate_ref.shape, 1)
    ge = jnp.sum(jnp.where(lane == e, gate_ref[...], 0.0), axis=-1, keepdims=True)
    acc_ref[...] += ge * y

    @pl.when((e == pl.num_programs(1) - 1) & (f == pl.num_programs(2) - 1))
    def _():
        o_ref[...] = _layer_norm(DN_ALPHA * x_ref[...] + acc_ref[...], lnw_ref[...], lnb_ref[...])


def _moe_ln(x2, gates, wg, wu, wd, ln_w, ln_b, tm=1024, tf=512):
    t = x2.shape[0]
    ne, _, dff = wg.shape
    const = lambda shape: pl.BlockSpec(shape, lambda i, e, f: (0,) * len(shape))
    return pl.pallas_call(
        _moe_kernel,
        grid=(t // tm, ne, dff // tf),
        in_specs=[pl.BlockSpec((tm, D_MODEL), lambda i, e, f: (i, 0)),
                  pl.BlockSpec((tm, LANES), lambda i, e, f: (i, 0)),
                  pl.BlockSpec((None, D_MODEL, tf), lambda i, e, f: (e, 0, f)),
                  pl.BlockSpec((None, D_MODEL, tf), lambda i, e, f: (e, 0, f)),
                  pl.BlockSpec((None, tf, D_MODEL), lambda i, e, f: (e, f, 0)),
                  const((1, D_MODEL)), const((1, D_MODEL))],
        out_specs=pl.BlockSpec((tm, D_MODEL), lambda i, e, f: (i, 0)),
        out_shape=jax.ShapeDtypeStruct((t, D_MODEL), F32),
        scratch_shapes=[pltpu.VMEM((tm, D_MODEL), BF16), pltpu.VMEM((tm, D_MODEL), F32)],
        compiler_params=_cparams("parallel", "arbitrary", "arbitrary"),
        name="moe_ln",
    )(x2, gates, wg, wu, wd, ln_w.reshape(1, -1), ln_b.reshape(1, -1))


def _layout_w_in(w):
    sizes = (REC_W, REC_W, REC_W, REC_W, MLA_Q_RANK, MLA_KV_RANK, MLA_ROPE,
             REC_W, REC_W, REC_W, REC_W, N_RHEADS, N_RHEADS)
    offs = np.concatenate([[0], np.cumsum(sizes)])
    part = lambda j: w[:, offs[j]:offs[j + 1]]
    z = lambda n: jnp.zeros((w.shape[0], n), w.dtype)
    misc = jnp.concatenate([part(11), part(12), z(MISC_KR - 2 * N_RHEADS), part(6),
                            z(LANES - MISC_KR - MLA_ROPE)], axis=1)
    cols = [part(0), part(1), part(2), part(3), part(7), part(8), part(9), part(10), part(4), part(5), misc]
    return jnp.concatenate(cols, axis=1).astype(BF16)


def _layout_mla(w_uq, w_ukv):
    r = w_uq.shape[0]
    uq = w_uq.reshape(r, MLA_HEADS, MLA_NOPE + MLA_ROPE)
    uq = jnp.pad(uq, ((0, 0), (0, 0), (0, LANES - MLA_NOPE - MLA_ROPE))).reshape(r, MLA_HEADS * LANES)
    r = w_ukv.shape[0]
    ukv = w_ukv.reshape(r, MLA_HEADS, MLA_NOPE + MLA_V)
    uk = jnp.pad(ukv[:, :, :MLA_NOPE], ((0, 0), (0, 0), (0, LANES - MLA_NOPE))).reshape(r, MLA_HEADS * LANES)
    uv = ukv[:, :, MLA_NOPE:]
    zero = jnp.zeros_like(uv)
    odd = (jnp.arange(MLA_HEADS) % 2 == 1)[None, :, None]
    uv = jnp.concatenate([jnp.where(odd, zero, uv), jnp.where(odd, uv, zero)], axis=-1)
    return uq.astype(BF16), uk.astype(BF16), uv.reshape(r, MLA_HEADS * LANES).astype(BF16)


def kernel(x, w_in, ret_gn_w, mla_q_norm_w, mla_w_uq, mla_kv_norm_w, mla_w_ukv, mlstm_conv_w, mlstm_conv_b,
           mlstm_b_i, mlstm_b_f, mlstm_gn_w, w_out, ln1_w, ln1_b, ffn_w_gate, ffn_w_up, ffn_w_down,
           moe_router, moe_w_gate, moe_w_up, moe_w_down, ln2_w, ln2_b):
    bsz, seq, d = x.shape
    t = bsz * seq
    cos_r, sin_r, cos_m, sin_m = _rope_tables(seq)
    x2 = x.reshape(t, d)
    for l in range(DEPTH):
        h2 = _inproj(x2, _layout_w_in(w_in[l]))
        h3 = h2.reshape(bsz, seq, D_IN_PAD)
        ret = _retention(h3, cos_r, sin_r, ret_gn_w[l]).reshape(t, REC_W)
        ml = _mlstm(h3, mlstm_conv_w[l], mlstm_conv_b[l], mlstm_b_i[l], mlstm_b_f[l],
                    mlstm_gn_w[l]).reshape(t, REC_W)
        wuq, wuk, wuv = _layout_mla(mla_w_uq[l], mla_w_ukv[l])
        q, k, v = _mla_prep(h2, seq, cos_m, sin_m, mla_q_norm_w[l], mla_kv_norm_w[l], wuq, wuk, wuv)
        att = _mla_attn(q, k, v, bsz, seq)
        x2 = _outproj_ln(ret, att, ml, x2, w_out[l].astype(BF16), ln1_w[l], ln1_b[l])
        if l % 2 == 0:
            j = l // 2
            x2 = _ffn_ln(x2, ffn_w_gate[j].astype(BF16), ffn_w_up[j].astype(BF16),
                         ffn_w_down[j].astype(BF16), ln2_w[l], ln2_b[l])
        else:
            j = l // 2
            gates = _router(x2, moe_router[j])
            x2 = _moe_ln(x2, gates, moe_w_gate[j], moe_w_up[j], moe_w_down[j], ln2_w[l], ln2_b[l])
    return x2.reshape(bsz, seq, d)
```

```python
import functools

import numpy as np
import jax
import jax.numpy as jnp
from jax import lax
from jax.experimental import pallas as pl
from jax.experimental.pallas import tpu as pltpu

F32 = jnp.float32
BF16 = jnp.bfloat16

D_MODEL = 1024
DEPTH = 2
ROPE_BASE = 10000.0
LN_EPS = 1e-5
NEG = -1e30

HEAD_DIM = 64
N_RHEADS = 4
REC_W = N_RHEADS * HEAD_DIM
MLA_HEADS = 8
MLA_NOPE = 64
MLA_ROPE = 32
MLA_V = 64
MLA_Q_RANK = 256
MLA_KV_RANK = 128
MLA_W = MLA_HEADS * MLA_V
MLSTM_CONV = 4
MASK_CHUNK = 64
REC_CHUNK = 256
ATT_TILE = 256
N_EXPERTS = 8
LANES = 128

DN_ALPHA = (2 * DEPTH) ** 0.25
LOG2_E = 1.4426950408889634

COL_RET = 0
COL_MLSTM = 1024
COL_CQ = 2048
COL_CKV = 2304
COL_MISC = 2432
D_IN_PAD = 2560
MISC_KR = 64

VMEM_LIMIT = 56 * 1024 * 1024


def _cparams(*sem):
    return pltpu.CompilerParams(dimension_semantics=sem, vmem_limit_bytes=VMEM_LIMIT)


def _layer_norm(y, w, b):
    mu = jnp.mean(y, axis=-1, keepdims=True)
    d = y - mu
    var = jnp.mean(d * d, axis=-1, keepdims=True)
    return d * lax.rsqrt(var + LN_EPS) * w + b


def _silu(x):
    return x * (1.0 / (1.0 + jnp.exp(-x)))


def _sigmoid(x):
    return 1.0 / (1.0 + jnp.exp(-x))


def _split3(x):
    hi = x.astype(BF16)
    r1 = x - hi.astype(F32)
    mid = r1.astype(BF16)
    lo = (r1 - mid.astype(F32)).astype(BF16)
    return hi, mid, lo


def _dot(a, b):
    return jnp.dot(a, b, preferred_element_type=F32)


def _dot_nt(a, b):
    return lax.dot_general(a, b, (((1,), (1,)), ((), ())), preferred_element_type=F32)


def _dot_exact_rhs(x, m_bf16):
    hi, mid, lo = _split3(x)
    return _dot(hi, m_bf16) + _dot(mid, m_bf16) + _dot(lo, m_bf16)


def _swap_halves(x, half):
    n = x.shape[-1]
    lane = lax.broadcasted_iota(jnp.int32, x.shape, x.ndim - 1)
    first = (lane % (2 * half)) < half
    return jnp.where(first, pltpu.roll(x, n - half, x.ndim - 1), pltpu.roll(x, half, x.ndim - 1))


def _head_norm(x, avg_bf16, w):
    mu = _dot_exact_rhs(x, avg_bf16)
    d = x - mu
    var = _dot_exact_rhs(d * d, avg_bf16)
    return d * lax.rsqrt(var + LN_EPS) * w


def _expand_heads(v, lane_head):
    out = jnp.zeros((v.shape[0], REC_W), F32)
    for h in range(N_RHEADS):
        out = jnp.where(lane_head == h, v[:, h:h + 1], out)
    return out


def _rope_tables(seq):
    pos = np.arange(seq, dtype=np.float64)[:, None]
    half = HEAD_DIM // 2
    inv = ROPE_BASE ** (-np.arange(half, dtype=np.float64) / half)
    ang = pos * inv[None, :]
    cos_r = np.tile(np.concatenate([np.cos(ang), np.cos(ang)], -1), (1, N_RHEADS))
    sin_r = np.tile(np.concatenate([-np.sin(ang), np.sin(ang)], -1), (1, N_RHEADS))
    half = MLA_ROPE // 2
    inv = ROPE_BASE ** (-np.arange(half, dtype=np.float64) / half)
    ang = pos * inv[None, :]
    cos_m = np.ones((seq, LANES))
    sin_m = np.zeros((seq, LANES))
    cos_m[:, MISC_KR:MISC_KR + MLA_ROPE] = np.concatenate([np.cos(ang), np.cos(ang)], -1)
    sin_m[:, MISC_KR:MISC_KR + MLA_ROPE] = np.concatenate([-np.sin(ang), np.sin(ang)], -1)
    f = lambda a: jnp.asarray(a.astype(np.float32))
    return f(cos_r), f(sin_r), f(cos_m), f(sin_m)


def _retention_tables():
    L = REC_CHUNK
    log_gamma = np.log(1.0 - 2.0 ** (-5.0 - np.arange(N_RHEADS, dtype=np.float64)))
    idx = np.arange(L, dtype=np.float64)
    diff = idx[:, None] - idx[None, :]
    dmask = np.where(diff >= 0, np.exp(diff[None] * log_gamma[:, None, None]), 0.0)
    lane_lg = np.repeat(log_gamma, HEAD_DIM)[None, :]
    qw = np.exp((idx + 1.0)[:, None] * lane_lg)
    kw = np.exp((L - 1 - idx)[:, None] * lane_lg)
    cd = np.exp(L * lane_lg)
    f = lambda a: jnp.asarray(a.astype(np.float32))
    return f(dmask), f(qw), f(kw), f(cd)


def _head_avg_matrix():
    h = np.arange(REC_W) // HEAD_DIM
    return jnp.asarray((h[:, None] == h[None, :]).astype(np.float32) / HEAD_DIM, dtype=BF16)


def _inproj_kernel(x_ref, w_ref, o_ref):
    o_ref[...] = _dot(x_ref[...].astype(BF16), w_ref[...])


def _inproj(x2, w_bf16, tm=512):
    t, k = x2.shape
    n = w_bf16.shape[1]
    return pl.pallas_call(
        _inproj_kernel,
        grid=(t // tm,),
        in_specs=[pl.BlockSpec((tm, k), lambda i: (i, 0)),
                  pl.BlockSpec((k, n), lambda i: (0, 0))],
        out_specs=pl.BlockSpec((tm, n), lambda i: (i, 0)),
        out_shape=jax.ShapeDtypeStruct((t, n), F32),
        compiler_params=_cparams("parallel"),
        name="inproj",
    )(x2, w_bf16)


def _retention_kernel(h_ref, cos_ref, sin_ref, dmask_ref, qw_ref, kw_ref, cd_ref, avg_ref, gnw_ref,
                      o_ref, state_ref):
    L = REC_CHUNK
    nc = h_ref.shape[0] // L
    lane_head = lax.broadcasted_iota(jnp.int32, (1, REC_W), 1) // HEAD_DIM
    row_head = lax.broadcasted_iota(jnp.int32, (REC_W, REC_W), 0) // HEAD_DIM
    col_head = lax.broadcasted_iota(jnp.int32, (REC_W, REC_W), 1) // HEAD_DIM
    block_diag = row_head == col_head
    state_ref[...] = jnp.zeros_like(state_ref)

    def chunk(c, carry):
        r0 = pl.multiple_of(c * L, L)
        rows = pl.ds(r0, L)
        cos = cos_ref[rows, :]
        sin = sin_ref[rows, :]
        q = h_ref[rows, 0:REC_W]
        k = h_ref[rows, REC_W:2 * REC_W]
        v = h_ref[rows, 2 * REC_W:3 * REC_W]
        g = h_ref[rows, 3 * REC_W:4 * REC_W]
        q = q * cos + _swap_halves(q, HEAD_DIM // 2) * sin
        k = (k * cos + _swap_halves(k, HEAD_DIM // 2) * sin) * (HEAD_DIM ** -0.5)
        kb = k.astype(BF16)
        vb = v.astype(BF16)
        inner = jnp.zeros((L, REC_W), F32)
        for h in range(N_RHEADS):
            mh = lane_head == h
            qh = jnp.where(mh, q, 0.0).astype(BF16)
            s = _dot_nt(qh, kb) * dmask_ref[h]
            o = _dot(s.astype(BF16), vb)
            inner = jnp.where(mh, o, inner)
        state = state_ref[...]
        cross = _dot(q.astype(BF16), state.astype(BF16)) * qw_ref[...]
        ret = inner + cross
        kwt = (k * kw_ref[...]).T.astype(BF16)
        loc = _dot(kwt, vb)
        state_ref[...] = cd_ref[...] * state + jnp.where(block_diag, loc, 0.0)
        o_ref[rows, :] = _silu(g) * _head_norm(ret, avg_ref[...], gnw_ref[...])
        return carry

    lax.fori_loop(0, nc, chunk, 0)


def _retention(h3, cos_r, sin_r, gn_w):
    b, s, _ = h3.shape
    dmask, qw, kw, cd = _retention_tables()
    avg = _head_avg_matrix()
    const = lambda shape: pl.BlockSpec(shape, lambda i: (0,) * len(shape))
    return pl.pallas_call(
        _retention_kernel,
        grid=(b,),
        in_specs=[pl.BlockSpec((None, s, 4 * REC_W), lambda i: (i, 0, COL_RET // (4 * REC_W))),
                  const((s, REC_W)), const((s, REC_W)),
                  const((N_RHEADS, REC_CHUNK, REC_CHUNK)), const((REC_CHUNK, REC_W)),
                  const((REC_CHUNK, REC_W)), const((1, REC_W)), const((REC_W, REC_W)), const((1, REC_W))],
        out_specs=pl.BlockSpec((None, s, REC_W), lambda i: (i, 0, 0)),
        out_shape=jax.ShapeDtypeStruct((b, s, REC_W), F32),
        scratch_shapes=[pltpu.VMEM((REC_W, REC_W), F32)],
        compiler_params=_cparams("parallel"),
        name="retention",
    )(h3, cos_r, sin_r, dmask, qw, kw, cd, avg, gn_w.reshape(1, REC_W))


def _mlstm_kernel(h_ref, misc_ref, convw_ref, convb_ref, bif_ref, avg_ref, gnw_ref,
                  o_ref, c_ref, n_ref, m_ref):
    L = REC_CHUNK
    s_len = h_ref.shape[0]
    nc = s_len // L
    lane_head = lax.broadcasted_iota(jnp.int32, (1, REC_W), 1) // HEAD_DIM
    row_head = lax.broadcasted_iota(jnp.int32, (REC_W, REC_W), 0) // HEAD_DIM
    col_head = lax.broadcasted_iota(jnp.int32, (REC_W, REC_W), 1) // HEAD_DIM
    block_diag = row_head == col_head
    n_mask = (lax.broadcasted_iota(jnp.int32, (REC_W, LANES), 0) // HEAD_DIM
              == lax.broadcasted_iota(jnp.int32, (REC_W, LANES), 1))
    lane128 = lax.broadcasted_iota(jnp.int32, (1, LANES), 1)
    gate_lanes = lane128 < N_RHEADS
    ri = lax.broadcasted_iota(jnp.int32, (L, L), 0)
    ci = lax.broadcasted_iota(jnp.int32, (L, L), 1)
    causal = ri >= ci
    tri = causal.astype(BF16)

    row8 = lax.broadcasted_iota(jnp.int32, (8, 2 * REC_W), 0)

    c_ref[...] = jnp.zeros_like(c_ref)
    n_ref[...] = jnp.zeros_like(n_ref)
    m_ref[...] = jnp.full(m_ref.shape, NEG, F32)

    def chunk(c, carry):
        r0 = pl.multiple_of(c * L, L)
        rows = pl.ds(r0, L)

        x = h_ref[rows, 0:2 * REC_W]
        prev = h_ref[pl.ds(pl.multiple_of(jnp.maximum(r0 - 8, 0), 8), 8), 0:2 * REC_W]
        prev = jnp.where(c > 0, prev, 0.0)
        acc = x * convw_ref[MLSTM_CONV - 1:MLSTM_CONV, :] + convb_ref[...]
        for sft in range(1, MLSTM_CONV):
            xs = pltpu.roll(x, sft, 0)
            head = jnp.where(row8 < sft, pltpu.roll(prev, sft, 0), xs[0:8, :])
            xs = jnp.concatenate([head, xs[8:, :]], axis=0)
            acc = acc + xs * convw_ref[MLSTM_CONV - 1 - sft:MLSTM_CONV - sft, :]
        qk = _silu(acc)
        q = qk[:, 0:REC_W]
        k = qk[:, REC_W:2 * REC_W] * (HEAD_DIM ** -0.5)
        v = h_ref[rows, 2 * REC_W:3 * REC_W]
        og = h_ref[rows, 3 * REC_W:4 * REC_W]

        gates = misc_ref[rows, :] + bif_ref[...]
        ic_c = jnp.where(gate_lanes, gates, 0.0)
        fg = pltpu.roll(gates, LANES - N_RHEADS, 1)
        lf_c = jnp.where(gate_lanes, jnp.minimum(fg, 0.0) - jnp.log1p(jnp.exp(-jnp.abs(fg))), 0.0)
        qb = q.astype(BF16)
        kb = k.astype(BF16)
        vb = v.astype(BF16)

        bcum = _dot_exact_rhs_left(tri, lf_c)
        gsum = bcum[L - 1:L, :]
        m_s = m_ref[...]
        a = gsum - bcum + ic_c
        amax = jnp.max(a, axis=0, keepdims=True)
        rvec = (ic_c - bcum).T
        inter = bcum + m_s

        c_state = c_ref[...]
        n_state = n_ref[...]
        q_c = _dot(qb, c_state.astype(BF16))
        q_n = _dot(qb, n_state.astype(BF16))

        hcell = jnp.zeros((L, REC_W), F32)
        for h in range(N_RHEADS):
            mh = lane_head == h
            log_d = jnp.where(causal, bcum[:, h:h + 1] + rvec[h:h + 1, :], NEG)
            inter_h = inter[:, h:h + 1]
            m_t = jnp.maximum(inter_h, jnp.max(log_d, axis=-1, keepdims=True))
            dw = jnp.exp(log_d - m_t)
            qh = jnp.where(mh, q, 0.0).astype(BF16)
            cmat = _dot_nt(qh, kb) * dw
            si = jnp.exp(inter_h - m_t)
            num = _dot(cmat.astype(BF16), vb) + si * q_c
            den = jnp.sum(cmat, axis=-1, keepdims=True) + si * q_n[:, h:h + 1]
            denom = jnp.maximum(jnp.abs(den), jnp.exp(-m_t))
            hcell = jnp.where(mh, num / denom, hcell)

        wa = jnp.exp(a - amax)
        m_new = jnp.maximum(gsum + m_s, amax)
        sp = jnp.exp(gsum + m_s - m_new)
        sl = jnp.exp(amax - m_new)
        kwt = (k * _expand_heads(wa, lane_head)).T.astype(BF16)
        loc_c = _dot(kwt, vb)
        loc_n = _dot(kwt, jnp.ones((L, LANES), BF16))
        c_ref[...] = (c_state * _expand_heads(sp, lane_head)
                      + jnp.where(block_diag, loc_c, 0.0) * _expand_heads(sl, lane_head))
        n_ref[...] = n_state * sp + jnp.where(n_mask, loc_n, 0.0) * sl
        m_ref[...] = m_new

        o_ref[rows, :] = _head_norm(_sigmoid(og) * hcell, avg_ref[...], gnw_ref[...])
        return carry

    lax.fori_loop(0, nc, chunk, 0)


def _dot_exact_rhs_left(m_bf16, x):
    hi, mid, lo = _split3(x)
    return _dot(m_bf16, hi) + _dot(m_bf16, mid) + _dot(m_bf16, lo)


def _mlstm(h3, conv_w, conv_b, b_i, b_f, gn_w):
    b, s, _ = h3.shape
    avg = _head_avg_matrix()
    bif = jnp.zeros((1, LANES), F32).at[0, 0:N_RHEADS].set(b_i).at[0, N_RHEADS:2 * N_RHEADS].set(b_f)
    const = lambda shape: pl.BlockSpec(shape, lambda i: (0,) * len(shape))
    return pl.pallas_call(
        _mlstm_kernel,
        grid=(b,),
        in_specs=[pl.BlockSpec((None, s, 4 * REC_W), lambda i: (i, 0, COL_MLSTM // (4 * REC_W))),
                  pl.BlockSpec((None, s, LANES), lambda i: (i, 0, COL_MISC // LANES)),
                  const((MLSTM_CONV, 2 * REC_W)), const((1, 2 * REC_W)), const((1, LANES)),
                  const((REC_W, REC_W)), const((1, REC_W))],
        out_specs=pl.BlockSpec((None, s, REC_W), lambda i: (i, 0, 0)),
        out_shape=jax.ShapeDtypeStruct((b, s, REC_W), F32),
        scratch_shapes=[pltpu.VMEM((REC_W, REC_W), F32),
                        pltpu.VMEM((REC_W, LANES), F32), pltpu.VMEM((1, LANES), F32)],
        compiler_params=_cparams("parallel"),
        name="mlstm",
    )(h3, h3, conv_w, conv_b.reshape(1, -1), bif, avg, gn_w.reshape(1, REC_W))


def _mla_prep_kernel(cq_ref, ckv_ref, cos_ref, sin_ref, qnw_ref, kvnw_ref, wuq_ref, wuk_ref, wuvt_ref,
                     q_ref, k_ref, vt_ref):
    scale = (MLA_NOPE + MLA_ROPE) ** -0.5 * LOG2_E
    cos = cos_ref[...]
    sin = sin_ref[...]
    cos_all = jnp.concatenate([cos] * MLA_HEADS, axis=1)
    sin_all = jnp.concatenate([sin] * MLA_HEADS, axis=1)

    cq = cq_ref[...]
    cqn = cq * lax.rsqrt(jnp.mean(cq * cq, axis=-1, keepdims=True) + LN_EPS) * qnw_ref[...]
    q = _dot(cqn.astype(BF16), wuq_ref[...])
    q = q * cos_all + _swap_halves(q, MLA_ROPE // 2) * sin_all
    q_ref[...] = (q * scale).astype(BF16)

    ckv = ckv_ref[:, 0:MLA_KV_RANK]
    kvn = ckv * lax.rsqrt(jnp.mean(ckv * ckv, axis=-1, keepdims=True) + LN_EPS) * kvnw_ref[...]
    kvb = kvn.astype(BF16)
    misc = ckv_ref[:, MLA_KV_RANK:MLA_KV_RANK + LANES]
    lane = lax.broadcasted_iota(jnp.int32, (1, LANES), 1)
    kr = misc * cos + _swap_halves(misc, MLA_ROPE // 2) * sin
    kr = jnp.where((lane >= MISC_KR) & (lane < MISC_KR + MLA_ROPE), kr, 0.0)
    k = _dot(kvb, wuk_ref[...]) + jnp.concatenate([kr] * MLA_HEADS, axis=1)
    k_ref[...] = k.astype(BF16)
    vt = _dot_nt(wuvt_ref[...], kvb).astype(BF16)
    for j in range(vt_ref.shape[0]):
        vt_ref[j] = vt[:, j * ATT_TILE:(j + 1) * ATT_TILE]


def _mla_prep(h2, seq, cos_m, sin_m, qn_w, kvn_w, wuq, wuk, wuvt, tm=512):
    t = h2.shape[0]
    nblk = seq // tm
    hw = MLA_HEADS * LANES
    const = lambda shape: pl.BlockSpec(shape, lambda i: (0,) * len(shape))
    return pl.pallas_call(
        _mla_prep_kernel,
        grid=(t // tm,),
        in_specs=[pl.BlockSpec((tm, MLA_Q_RANK), lambda i: (i, COL_CQ // MLA_Q_RANK)),
                  pl.BlockSpec((tm, 2 * LANES), lambda i: (i, COL_CKV // (2 * LANES))),
                  pl.BlockSpec((tm, LANES), lambda i: (i % nblk, 0)),
                  pl.BlockSpec((tm, LANES), lambda i: (i % nblk, 0)),
                  const((1, MLA_Q_RANK)), const((1, MLA_KV_RANK)),
                  const((MLA_Q_RANK, hw)), const((MLA_KV_RANK, hw)), const((MLA_W, MLA_KV_RANK))],
        out_specs=[pl.BlockSpec((tm, hw), lambda i: (i, 0)), pl.BlockSpec((tm, hw), lambda i: (i, 0)),
                   pl.BlockSpec((tm // ATT_TILE, MLA_W, ATT_TILE), lambda i: (i, 0, 0))],
        out_shape=[jax.ShapeDtypeStruct((t, hw), BF16), jax.ShapeDtypeStruct((t, hw), BF16),
                   jax.ShapeDtypeStruct((t // ATT_TILE, MLA_W, ATT_TILE), BF16)],
        compiler_params=_cparams("parallel"),
        name="mla_prep",
    )(h2, h2, cos_m, sin_m, qn_w.reshape(1, -1), kvn_w.reshape(1, -1), wuq, wuk, wuvt)


def _mla_attn_kernel(q_ref, k_ref, vt_ref, o_ref):
    tq = ATT_TILE
    qi = pl.program_id(1)
    key = lax.broadcasted_iota(jnp.int32, (tq, tq), 0)
    qry = lax.broadcasted_iota(jnp.int32, (tq, tq), 1)
    diag_ok = key < (qry // MASK_CHUNK + 1) * MASK_CHUNK

    def step(kt, carry, masked):
        rows = pl.ds(pl.multiple_of(kt * tq, tq), tq)
        sts = []
        for j in range(MLA_HEADS):
            cols = slice(j * LANES, (j + 1) * LANES)
            sts.append(_dot_nt(k_ref[rows, cols], q_ref[:, cols]))
        pts = []
        for j in range(MLA_HEADS):
            m, l, _ = carry[j]
            st = sts[j]
            if masked:
                st = jnp.where(diag_ok, st, NEG)
            m_new = jnp.maximum(m, jnp.max(st, axis=0, keepdims=True))
            alpha = jnp.exp2(m - m_new)
            pt = jnp.exp2(st - m_new)
            l = alpha * l + jnp.sum(pt, axis=0, keepdims=True)
            pts.append((m_new, l, alpha, pt.astype(BF16)))
        new = []
        for j in range(MLA_HEADS):
            m_new, l, alpha, pt = pts[j]
            acc = alpha * carry[j][2] + _dot(vt_ref[kt, j * MLA_V:(j + 1) * MLA_V, :], pt)
            new.append((m_new, l, acc))
        return tuple(new)

    init = tuple((jnp.full((1, tq), NEG, F32), jnp.zeros((1, tq), F32), jnp.zeros((MLA_V, tq), F32))
                 for _ in range(MLA_HEADS))
    carry = lax.fori_loop(0, qi, lambda kt, c: step(kt, c, False), init)
    carry = step(qi, carry, True)
    for p in range(MLA_HEADS // 2):
        (_, l0, a0), (_, l1, a1) = carry[2 * p], carry[2 * p + 1]
        o_ref[:, p * LANES:(p + 1) * LANES] = jnp.concatenate([a0 / l0, a1 / l1], axis=0).T


def _mla_attn(q, k, vt, batch, seq):
    t = q.shape[0]
    nq = seq // ATT_TILE
    hw = MLA_HEADS * LANES
    return pl.pallas_call(
        _mla_attn_kernel,
        grid=(batch, nq),
        in_specs=[pl.BlockSpec((ATT_TILE, hw), lambda b, i: (b * nq + i, 0)),
                  pl.BlockSpec((seq, hw), lambda b, i: (b, 0)),
                  pl.BlockSpec((nq, MLA_W, ATT_TILE), lambda b, i: (b, 0, 0))],
        out_specs=pl.BlockSpec((ATT_TILE, MLA_W), lambda b, i: (b * nq + i, 0)),
        out_shape=jax.ShapeDtypeStruct((t, MLA_W), F32),
        compiler_params=_cparams("parallel", "arbitrary"),
        name="mla_attn",
    )(q, k, vt)


def _outproj_kernel(ret_ref, att_ref, ml_ref, x_ref, w_ref, lnw_ref, lnb_ref, o_ref):
    mix = (_dot(ret_ref[...].astype(BF16), w_ref[0:REC_W, :])
           + _dot(att_ref[...].astype(BF16), w_ref[REC_W:REC_W + MLA_W, :])
           + _dot(ml_ref[...].astype(BF16), w_ref[REC_W + MLA_W:, :]))
    o_ref[...] = _layer_norm(DN_ALPHA * x_ref[...] + mix, lnw_ref[...], lnb_ref[...])


def _outproj_ln(ret, att, ml, x2, w_bf16, ln_w, ln_b, tm=512):
    t = x2.shape[0]
    const = lambda shape: pl.BlockSpec(shape, lambda i: (0,) * len(shape))
    row = lambda w: pl.BlockSpec((tm, w), lambda i: (i, 0))
    return pl.pallas_call(
        _outproj_kernel,
        grid=(t // tm,),
        in_specs=[row(REC_W), row(MLA_W), row(REC_W), row(D_MODEL),
                  const((D_MODEL, D_MODEL)), const((1, D_MODEL)), const((1, D_MODEL))],
        out_specs=row(D_MODEL),
        out_shape=jax.ShapeDtypeStruct((t, D_MODEL), F32),
        compiler_params=_cparams("parallel"),
        name="outproj_ln",
    )(ret, att, ml, x2, w_bf16, ln_w.reshape(1, -1), ln_b.reshape(1, -1))


def _ffn_kernel(x_ref, wg_ref, wu_ref, wd_ref, lnw_ref, lnb_ref, o_ref, xb_ref, acc_ref):
    f = pl.program_id(1)

    @pl.when(f == 0)
    def _():
        xb_ref[...] = x_ref[...].astype(BF16)
        acc_ref[...] = jnp.zeros_like(acc_ref)

    xb = xb_ref[...]
    g = _dot(xb, wg_ref[...])
    u = _dot(xb, wu_ref[...])
    acc_ref[...] += _dot((_silu(g) * u).astype(BF16), wd_ref[...])

    @pl.when(f == pl.num_programs(1) - 1)
    def _():
        o_ref[...] = _layer_norm(DN_ALPHA * x_ref[...] + acc_ref[...], lnw_ref[...], lnb_ref[...])


def _ffn_ln(x2, wg, wu, wd, ln_w, ln_b, tm=1024, tf=256):
    t = x2.shape[0]
    dff = wg.shape[1]
    const = lambda shape: pl.BlockSpec(shape, lambda i, f: (0,) * len(shape))
    return pl.pallas_call(
        _ffn_kernel,
        grid=(t // tm, dff // tf),
        in_specs=[pl.BlockSpec((tm, D_MODEL), lambda i, f: (i, 0)),
                  pl.BlockSpec((D_MODEL, tf), lambda i, f: (0, f)),
                  pl.BlockSpec((D_MODEL, tf), lambda i, f: (0, f)),
                  pl.BlockSpec((tf, D_MODEL), lambda i, f: (f, 0)),
                  const((1, D_MODEL)), const((1, D_MODEL))],
        out_specs=pl.BlockSpec((tm, D_MODEL), lambda i, f: (i, 0)),
        out_shape=jax.ShapeDtypeStruct((t, D_MODEL), F32),
        scratch_shapes=[pltpu.VMEM((tm, D_MODEL), BF16), pltpu.VMEM((tm, D_MODEL), F32)],
        compiler_params=_cparams("parallel", "arbitrary"),
        name="ffn_ln",
    )(x2, wg, wu, wd, ln_w.reshape(1, -1), ln_b.reshape(1, -1))


def _router_kernel(x_ref, r_ref, g_ref):
    xh, xm, xl = _split3(x_ref[...])
    rh, rm, rl = r_ref[0], r_ref[1], r_ref[2]
    logits = (_dot(xh, rh) + _dot(xh, rm) + _dot(xm, rh)
              + _dot(xh, rl) + _dot(xm, rm) + _dot(xl, rh))
    lane = lax.broadcasted_iota(jnp.int32, logits.shape, 1)
    lg = jnp.where(lane < N_EXPERTS, logits, -jnp.inf)
    m1 = jnp.max(lg, axis=-1, keepdims=True)
    i1 = jnp.min(jnp.where(lg == m1, lane, LANES), axis=-1, keepdims=True)
    lg2 = jnp.where(lane == i1, -jnp.inf, lg)
    m2 = jnp.max(lg2, axis=-1, keepdims=True)
    i2 = jnp.min(jnp.where(lg2 == m2, lane, LANES), axis=-1, keepdims=True)
    e2 = jnp.exp(m2 - m1)
    w1 = 1.0 / (1.0 + e2)
    w2 = e2 / (1.0 + e2)
    g_ref[...] = jnp.where(lane == i1, w1, 0.0) + jnp.where(lane == i2, w2, 0.0)


def _router(x2, router, tm=1024):
    t = x2.shape[0]
    rp = jnp.zeros((D_MODEL, LANES), F32).at[:, 0:N_EXPERTS].set(router)
    rh = rp.astype(BF16)
    r1 = rp - rh.astype(F32)
    rm = r1.astype(BF16)
    rl = (r1 - rm.astype(F32)).astype(BF16)
    r3 = jnp.stack([rh, rm, rl])
    return pl.pallas_call(
        _router_kernel,
        grid=(t // tm,),
        in_specs=[pl.BlockSpec((tm, D_MODEL), lambda i: (i, 0)),
                  pl.BlockSpec((3, D_MODEL, LANES), lambda i: (0, 0, 0))],
        out_specs=pl.BlockSpec((tm, LANES), lambda i: (i, 0)),
        out_shape=jax.ShapeDtypeStruct((t, LANES), F32),
        compiler_params=_cparams("parallel"),
        name="router",
    )(x2, r3)


def _moe_kernel(x_ref, gate_ref, wg_ref, wu_ref, wd_ref, lnw_ref, lnb_ref, o_ref, xb_ref, acc_ref):
    e = pl.program_id(1)
    f = pl.program_id(2)

    @pl.when((e == 0) & (f == 0))
    def _():
        xb_ref[...] = x_ref[...].astype(BF16)
        acc_ref[...] = jnp.zeros_like(acc_ref)

    xb = xb_ref[...]
    g = _dot(xb, wg_ref[...].astype(BF16))
    u = _dot(xb, wu_ref[...].astype(BF16))
    y = _dot((_silu(g) * u).astype(BF16), wd_ref[...].astype(BF16))
    lane = lax.broadcasted_iota(jnp.int32, gate_ref.shape, 1)
    ge = jnp.sum(jnp.where(lane == e, gate_ref[...], 0.0), axis=-1, keepdims=True)
    acc_ref[...] += ge * y

    @pl.when((e == pl.num_programs(1) - 1) & (f == pl.num_programs(2) - 1))
    def _():
        o_ref[...] = _layer_norm(DN_ALPHA * x_ref[...] + acc_ref[...], lnw_ref[...], lnb_ref[...])


def _moe_ln(x2, gates, wg, wu, wd, ln_w, ln_b, tm=1024, tf=512):
    t = x2.shape[0]
    ne, _, dff = wg.shape
    const = lambda shape: pl.BlockSpec(shape, lambda i, e, f: (0,) * len(shape))
    return pl.pallas_call(
        _moe_kernel,
        grid=(t // tm, ne, dff // tf),
        in_specs=[pl.BlockSpec((tm, D_MODEL), lambda i, e, f: (i, 0)),
                  pl.BlockSpec((tm, LANES), lambda i, e, f: (i, 0)),
                  pl.BlockSpec((None, D_MODEL, tf), lambda i, e, f: (e, 0, f)),
                  pl.BlockSpec((None, D_MODEL, tf), lambda i, e, f: (e, 0, f)),
                  pl.BlockSpec((None, tf, D_MODEL), lambda i, e, f: (e, f, 0)),
                  const((1, D_MODEL)), const((1, D_MODEL))],
        out_specs=pl.BlockSpec((tm, D_MODEL), lambda i, e, f: (i, 0)),
        out_shape=jax.ShapeDtypeStruct((t, D_MODEL), F32),
        scratch_shapes=[pltpu.VMEM((tm, D_MODEL), BF16), pltpu.VMEM((tm, D_MODEL), F32)],
        compiler_params=_cparams("parallel", "arbitrary", "arbitrary"),
        name="moe_ln",
    )(x2, gates, wg, wu, wd, ln_w.reshape(1, -1), ln_b.reshape(1, -1))


def _layout_w_in(w):
    sizes = (REC_W, REC_W, REC_W, REC_W, MLA_Q_RANK, MLA_KV_RANK, MLA_ROPE,
             REC_W, REC_W, REC_W, REC_W, N_RHEADS, N_RHEADS)
    offs = np.concatenate([[0], np.cumsum(sizes)])
    part = lambda j: w[:, offs[j]:offs[j + 1]]
    z = lambda n: jnp.zeros((w.shape[0], n), w.dtype)
    misc = jnp.concatenate([part(11), part(12), z(MISC_KR - 2 * N_RHEADS), part(6),
                            z(LANES - MISC_KR - MLA_ROPE)], axis=1)
    cols = [part(0), part(1), part(2), part(3), part(7), part(8), part(9), part(10), part(4), part(5), misc]
    return jnp.concatenate(cols, axis=1).astype(BF16)


def _layout_mla(w_uq, w_ukv):
    r = w_uq.shape[0]
    uq = w_uq.reshape(r, MLA_HEADS, MLA_NOPE + MLA_ROPE)
    uq = jnp.pad(uq, ((0, 0), (0, 0), (0, LANES - MLA_NOPE - MLA_ROPE))).reshape(r, MLA_HEADS * LANES)
    r = w_ukv.shape[0]
    ukv = w_ukv.reshape(r, MLA_HEADS, MLA_NOPE + MLA_V)
    uk = jnp.pad(ukv[:, :, :MLA_NOPE], ((0, 0), (0, 0), (0, LANES - MLA_NOPE))).reshape(r, MLA_HEADS * LANES)
    uvt = ukv[:, :, MLA_NOPE:].reshape(r, MLA_W).T
    return uq.astype(BF16), uk.astype(BF16), uvt.astype(BF16)


def kernel(x, w_in, ret_gn_w, mla_q_norm_w, mla_w_uq, mla_kv_norm_w, mla_w_ukv, mlstm_conv_w, mlstm_conv_b,
           mlstm_b_i, mlstm_b_f, mlstm_gn_w, w_out, ln1_w, ln1_b, ffn_w_gate, ffn_w_up, ffn_w_down,
           moe_router, moe_w_gate, moe_w_up, moe_w_down, ln2_w, ln2_b):
    bsz, seq, d = x.shape
    t = bsz * seq
    cos_r, sin_r, cos_m, sin_m = _rope_tables(seq)
    x2 = x.reshape(t, d)
    for l in range(DEPTH):
        h2 = _inproj(x2, _layout_w_in(w_in[l]))
        h3 = h2.reshape(bsz, seq, D_IN_PAD)
        ret = _retention(h3, cos_r, sin_r, ret_gn_w[l]).reshape(t, REC_W)
        ml = _mlstm(h3, mlstm_conv_w[l], mlstm_conv_b[l], mlstm_b_i[l], mlstm_b_f[l],
                    mlstm_gn_w[l]).reshape(t, REC_W)
        wuq, wuk, wuvt = _layout_mla(mla_w_uq[l], mla_w_ukv[l])
        q, k, vt = _mla_prep(h2, seq, cos_m, sin_m, mla_q_norm_w[l], mla_kv_norm_w[l], wuq, wuk, wuvt)
        att = _mla_attn(q, k, vt, bsz, seq)
        x2 = _outproj_ln(ret, att, ml, x2, w_out[l].astype(BF16), ln1_w[l], ln1_b[l])
        if l % 2 == 0:
            j = l // 2
            x2 = _ffn_ln(x2, ffn_w_gate[j].astype(BF16), ffn_w_up[j].astype(BF16),
                         ffn_w_down[j].astype(BF16), ln2_w[l], ln2_b[l])
        else:
            j = l // 2
            gates = _router(x2, moe_router[j])
            x2 = _moe_ln(x2, gates, moe_w_gate[j], moe_w_up[j], moe_w_down[j], ln2_w[l], ln2_b[l])
    return x2.reshape(bsz, seq, d)
```

```python
import functools

import numpy as np
import jax
import jax.numpy as jnp
from jax import lax
from jax.experimental import pallas as pl
from jax.experimental.pallas import tpu as pltpu

F32 = jnp.float32
BF16 = jnp.bfloat16

D_MODEL = 1024
DEPTH = 2
ROPE_BASE = 10000.0
LN_EPS = 1e-5
NEG = -1e30

HEAD_DIM = 64
N_RHEADS = 4
REC_W = N_RHEADS * HEAD_DIM
MLA_HEADS = 8
MLA_NOPE = 64
MLA_ROPE = 32
MLA_V = 64
MLA_Q_RANK = 256
MLA_KV_RANK = 128
MLA_W = MLA_HEADS * MLA_V
MLSTM_CONV = 4
MASK_CHUNK = 64
REC_CHUNK = 256
ATT_TILE = 256
N_EXPERTS = 8
LANES = 128

DN_ALPHA = (2 * DEPTH) ** 0.25
LOG2_E = 1.4426950408889634

COL_RET = 0
COL_MLSTM = 1024
COL_CQ = 2048
COL_CKV = 2304
COL_MISC = 2432
D_IN_PAD = 2560
MISC_KR = 64

VMEM_LIMIT = 56 * 1024 * 1024


def _cparams(*sem):
    return pltpu.CompilerParams(dimension_semantics=sem, vmem_limit_bytes=VMEM_LIMIT)


def _layer_norm(y, w, b):
    mu = jnp.mean(y, axis=-1, keepdims=True)
    d = y - mu
    var = jnp.mean(d * d, axis=-1, keepdims=True)
    return d * lax.rsqrt(var + LN_EPS) * w + b


def _silu(x):
    return x * (1.0 / (1.0 + jnp.exp(-x)))


def _sigmoid(x):
    return 1.0 / (1.0 + jnp.exp(-x))


def _split3(x):
    hi = x.astype(BF16)
    r1 = x - hi.astype(F32)
    mid = r1.astype(BF16)
    lo = (r1 - mid.astype(F32)).astype(BF16)
    return hi, mid, lo


def _dot(a, b):
    return jnp.dot(a, b, preferred_element_type=F32)


def _dot_nt(a, b):
    return lax.dot_general(a, b, (((1,), (1,)), ((), ())), preferred_element_type=F32)


def _dot_exact_rhs(x, m_bf16):
    hi, mid, lo = _split3(x)
    return _dot(hi, m_bf16) + _dot(mid, m_bf16) + _dot(lo, m_bf16)


def _swap_halves(x, half):
    n = x.shape[-1]
    lane = lax.broadcasted_iota(jnp.int32, x.shape, x.ndim - 1)
    first = (lane % (2 * half)) < half
    return jnp.where(first, pltpu.roll(x, n - half, x.ndim - 1), pltpu.roll(x, half, x.ndim - 1))


def _head_norm(x, avg_bf16, w):
    mu = _dot_exact_rhs(x, avg_bf16)
    d = x - mu
    var = _dot_exact_rhs(d * d, avg_bf16)
    return d * lax.rsqrt(var + LN_EPS) * w


def _expand_heads(v, lane_head):
    out = jnp.zeros((v.shape[0], REC_W), F32)
    for h in range(N_RHEADS):
        out = jnp.where(lane_head == h, v[:, h:h + 1], out)
    return out


def _rope_tables(seq):
    pos = np.arange(seq, dtype=np.float64)[:, None]
    half = HEAD_DIM // 2
    inv = ROPE_BASE ** (-np.arange(half, dtype=np.float64) / half)
    ang = pos * inv[None, :]
    cos_r = np.tile(np.concatenate([np.cos(ang), np.cos(ang)], -1), (1, N_RHEADS))
    sin_r = np.tile(np.concatenate([-np.sin(ang), np.sin(ang)], -1), (1, N_RHEADS))
    half = MLA_ROPE // 2
    inv = ROPE_BASE ** (-np.arange(half, dtype=np.float64) / half)
    ang = pos * inv[None, :]
    cos_m = np.ones((seq, LANES))
    sin_m = np.zeros((seq, LANES))
    cos_m[:, MISC_KR:MISC_KR + MLA_ROPE] = np.concatenate([np.cos(ang), np.cos(ang)], -1)
    sin_m[:, MISC_KR:MISC_KR + MLA_ROPE] = np.concatenate([-np.sin(ang), np.sin(ang)], -1)
    f = lambda a: jnp.asarray(a.astype(np.float32))
    return f(cos_r), f(sin_r), f(cos_m), f(sin_m)


def _retention_tables():
    L = REC_CHUNK
    log_gamma = np.log(1.0 - 2.0 ** (-5.0 - np.arange(N_RHEADS, dtype=np.float64)))
    idx = np.arange(L, dtype=np.float64)
    diff = idx[:, None] - idx[None, :]
    dmask = np.where(diff >= 0, np.exp(diff[None] * log_gamma[:, None, None]), 0.0)
    lane_lg = np.repeat(log_gamma, HEAD_DIM)[None, :]
    qw = np.exp((idx + 1.0)[:, None] * lane_lg)
    kw = np.exp((L - 1 - idx)[:, None] * lane_lg)
    cd = np.exp(L * lane_lg)
    f = lambda a: jnp.asarray(a.astype(np.float32))
    return f(dmask), f(qw), f(kw), f(cd)


def _head_avg_matrix():
    h = np.arange(REC_W) // HEAD_DIM
    return jnp.asarray((h[:, None] == h[None, :]).astype(np.float32) / HEAD_DIM, dtype=BF16)


def _inproj_kernel(x_ref, w_ref, o_ref):
    o_ref[...] = _dot(x_ref[...].astype(BF16), w_ref[...])


def _inproj(x2, w_bf16, tm=512):
    t, k = x2.shape
    n = w_bf16.shape[1]
    return pl.pallas_call(
        _inproj_kernel,
        grid=(t // tm,),
        in_specs=[pl.BlockSpec((tm, k), lambda i: (i, 0)),
                  pl.BlockSpec((k, n), lambda i: (0, 0))],
        out_specs=pl.BlockSpec((tm, n), lambda i: (i, 0)),
        out_shape=jax.ShapeDtypeStruct((t, n), F32),
        compiler_params=_cparams("parallel"),
        name="inproj",
    )(x2, w_bf16)


def _retention_kernel(h_ref, cos_ref, sin_ref, dmask_ref, qw_ref, kw_ref, cd_ref, avg_ref, gnw_ref,
                      o_ref, state_ref):
    L = REC_CHUNK
    nc = h_ref.shape[0] // L
    lane_head = lax.broadcasted_iota(jnp.int32, (1, REC_W), 1) // HEAD_DIM
    row_head = lax.broadcasted_iota(jnp.int32, (REC_W, REC_W), 0) // HEAD_DIM
    col_head = lax.broadcasted_iota(jnp.int32, (REC_W, REC_W), 1) // HEAD_DIM
    block_diag = row_head == col_head
    state_ref[...] = jnp.zeros_like(state_ref)

    def chunk(c, carry):
        r0 = pl.multiple_of(c * L, L)
        rows = pl.ds(r0, L)
        cos = cos_ref[rows, :]
        sin = sin_ref[rows, :]
        q = h_ref[rows, 0:REC_W]
        k = h_ref[rows, REC_W:2 * REC_W]
        v = h_ref[rows, 2 * REC_W:3 * REC_W]
        g = h_ref[rows, 3 * REC_W:4 * REC_W]
        q = q * cos + _swap_halves(q, HEAD_DIM // 2) * sin
        k = (k * cos + _swap_halves(k, HEAD_DIM // 2) * sin) * (HEAD_DIM ** -0.5)
        kb = k.astype(BF16)
        vb = v.astype(BF16)
        inner = jnp.zeros((L, REC_W), F32)
        for h in range(N_RHEADS):
            mh = lane_head == h
            qh = jnp.where(mh, q, 0.0).astype(BF16)
            s = _dot_nt(qh, kb) * dmask_ref[h]
            o = _dot(s.astype(BF16), vb)
            inner = jnp.where(mh, o, inner)
        state = state_ref[...]
        cross = _dot(q.astype(BF16), state.astype(BF16)) * qw_ref[...]
        ret = inner + cross
        kwt = (k * kw_ref[...]).T.astype(BF16)
        loc = _dot(kwt, vb)
        state_ref[...] = cd_ref[...] * state + jnp.where(block_diag, loc, 0.0)
        o_ref[rows, :] = _silu(g) * _head_norm(ret, avg_ref[...], gnw_ref[...])
        return carry

    lax.fori_loop(0, nc, chunk, 0)


def _retention(h3, cos_r, sin_r, gn_w):
    b, s, _ = h3.shape
    dmask, qw, kw, cd = _retention_tables()
    avg = _head_avg_matrix()
    const = lambda shape: pl.BlockSpec(shape, lambda i: (0,) * len(shape))
    return pl.pallas_call(
        _retention_kernel,
        grid=(b,),
        in_specs=[pl.BlockSpec((None, s, 4 * REC_W), lambda i: (i, 0, COL_RET // (4 * REC_W))),
                  const((s, REC_W)), const((s, REC_W)),
                  const((N_RHEADS, REC_CHUNK, REC_CHUNK)), const((REC_CHUNK, REC_W)),
                  const((REC_CHUNK, REC_W)), const((1, REC_W)), const((REC_W, REC_W)), const((1, REC_W))],
        out_specs=pl.BlockSpec((None, s, REC_W), lambda i: (i, 0, 0)),
        out_shape=jax.ShapeDtypeStruct((b, s, REC_W), F32),
        scratch_shapes=[pltpu.VMEM((REC_W, REC_W), F32)],
        compiler_params=_cparams("parallel"),
        name="retention",
    )(h3, cos_r, sin_r, dmask, qw, kw, cd, avg, gn_w.reshape(1, REC_W))


def _mlstm_kernel(h_ref, misc_ref, convw_ref, convb_ref, bif_ref, avg_ref, gnw_ref,
                  o_ref, c_ref, n_ref, m_ref):
    L = REC_CHUNK
    s_len = h_ref.shape[0]
    nc = s_len // L
    lane_head = lax.broadcasted_iota(jnp.int32, (1, REC_W), 1) // HEAD_DIM
    row_head = lax.broadcasted_iota(jnp.int32, (REC_W, REC_W), 0) // HEAD_DIM
    col_head = lax.broadcasted_iota(jnp.int32, (REC_W, REC_W), 1) // HEAD_DIM
    block_diag = row_head == col_head
    n_mask = (lax.broadcasted_iota(jnp.int32, (REC_W, LANES), 0) // HEAD_DIM
              == lax.broadcasted_iota(jnp.int32, (REC_W, LANES), 1))
    lane128 = lax.broadcasted_iota(jnp.int32, (1, LANES), 1)
    gate_lanes = lane128 < N_RHEADS
    ri = lax.broadcasted_iota(jnp.int32, (L, L), 0)
    ci = lax.broadcasted_iota(jnp.int32, (L, L), 1)
    causal = ri >= ci
    tri = causal.astype(BF16)

    row8 = lax.broadcasted_iota(jnp.int32, (8, 2 * REC_W), 0)

    c_ref[...] = jnp.zeros_like(c_ref)
    n_ref[...] = jnp.zeros_like(n_ref)
    m_ref[...] = jnp.full(m_ref.shape, NEG, F32)

    def chunk(c, carry):
        r0 = pl.multiple_of(c * L, L)
        rows = pl.ds(r0, L)

        x = h_ref[rows, 0:2 * REC_W]
        prev = h_ref[pl.ds(pl.multiple_of(jnp.maximum(r0 - 8, 0), 8), 8), 0:2 * REC_W]
        prev = jnp.where(c > 0, prev, 0.0)
        acc = x * convw_ref[MLSTM_CONV - 1:MLSTM_CONV, :] + convb_ref[...]
        for sft in range(1, MLSTM_CONV):
            xs = pltpu.roll(x, sft, 0)
            head = jnp.where(row8 < sft, pltpu.roll(prev, sft, 0), xs[0:8, :])
            xs = jnp.concatenate([head, xs[8:, :]], axis=0)
            acc = acc + xs * convw_ref[MLSTM_CONV - 1 - sft:MLSTM_CONV - sft, :]
        qk = _silu(acc)
        q = qk[:, 0:REC_W]
        k = qk[:, REC_W:2 * REC_W] * (HEAD_DIM ** -0.5)
        v = h_ref[rows, 2 * REC_W:3 * REC_W]
        og = h_ref[rows, 3 * REC_W:4 * REC_W]

        gates = misc_ref[rows, :] + bif_ref[...]
        ic_c = jnp.where(gate_lanes, gates, 0.0)
        fg = pltpu.roll(gates, LANES - N_RHEADS, 1)
        lf_c = jnp.where(gate_lanes, jnp.minimum(fg, 0.0) - jnp.log1p(jnp.exp(-jnp.abs(fg))), 0.0)
        qb = q.astype(BF16)
        kb = k.astype(BF16)
        vb = v.astype(BF16)

        bcum = _dot_exact_rhs_left(tri, lf_c)
        gsum = bcum[L - 1:L, :]
        m_s = m_ref[...]
        a = gsum - bcum + ic_c
        amax = jnp.max(a, axis=0, keepdims=True)
        rvec = (ic_c - bcum).T
        inter = bcum + m_s

        c_state = c_ref[...]
        n_state = n_ref[...]
        q_c = _dot(qb, c_state.astype(BF16))
        q_n = _dot(qb, n_state.astype(BF16))

        hcell = jnp.zeros((L, REC_W), F32)
        for h in range(N_RHEADS):
            mh = lane_head == h
            log_d = jnp.where(causal, bcum[:, h:h + 1] + rvec[h:h + 1, :], NEG)
            inter_h = inter[:, h:h + 1]
            m_t = jnp.maximum(inter_h, jnp.max(log_d, axis=-1, keepdims=True))
            dw = jnp.exp(log_d - m_t)
            qh = jnp.where(mh, q, 0.0).astype(BF16)
            cmat = _dot_nt(qh, kb) * dw
            si = jnp.exp(inter_h - m_t)
            num = _dot(cmat.astype(BF16), vb) + si * q_c
            den = jnp.sum(cmat, axis=-1, keepdims=True) + si * q_n[:, h:h + 1]
            denom = jnp.maximum(jnp.abs(den), jnp.exp(-m_t))
            hcell = jnp.where(mh, num / denom, hcell)

        wa = jnp.exp(a - amax)
        m_new = jnp.maximum(gsum + m_s, amax)
        sp = jnp.exp(gsum + m_s - m_new)
        sl = jnp.exp(amax - m_new)
        kwt = (k * _expand_heads(wa, lane_head)).T.astype(BF16)
        loc_c = _dot(kwt, vb)
        loc_n = _dot(kwt, jnp.ones((L, LANES), BF16))
        c_ref[...] = (c_state * _expand_heads(sp, lane_head)
                      + jnp.where(block_diag, loc_c, 0.0) * _expand_heads(sl, lane_head))
        n_ref[...] = n_state * sp + jnp.where(n_mask, loc_n, 0.0) * sl
        m_ref[...] = m_new

        o_ref[rows, :] = _head_norm(_sigmoid(og) * hcell, avg_ref[...], gnw_ref[...])
        return carry

    lax.fori_loop(0, nc, chunk, 0)


def _dot_exact_rhs_left(m_bf16, x):
    hi, mid, lo = _split3(x)
    return _dot(m_bf16, hi) + _dot(m_bf16, mid) + _dot(m_bf16, lo)


def _mlstm(h3, conv_w, conv_b, b_i, b_f, gn_w):
    b, s, _ = h3.shape
    avg = _head_avg_matrix()
    bif = jnp.zeros((1, LANES), F32).at[0, 0:N_RHEADS].set(b_i).at[0, N_RHEADS:2 * N_RHEADS].set(b_f)
    const = lambda shape: pl.BlockSpec(shape, lambda i: (0,) * len(shape))
    return pl.pallas_call(
        _mlstm_kernel,
        grid=(b,),
        in_specs=[pl.BlockSpec((None, s, 4 * REC_W), lambda i: (i, 0, COL_MLSTM // (4 * REC_W))),
                  pl.BlockSpec((None, s, LANES), lambda i: (i, 0, COL_MISC // LANES)),
                  const((MLSTM_CONV, 2 * REC_W)), const((1, 2 * REC_W)), const((1, LANES)),
                  const((REC_W, REC_W)), const((1, REC_W))],
        out_specs=pl.BlockSpec((None, s, REC_W), lambda i: (i, 0, 0)),
        out_shape=jax.ShapeDtypeStruct((b, s, REC_W), F32),
        scratch_shapes=[pltpu.VMEM((REC_W, REC_W), F32),
                        pltpu.VMEM((REC_W, LANES), F32), pltpu.VMEM((1, LANES), F32)],
        compiler_params=_cparams("parallel"),
        name="mlstm",
    )(h3, h3, conv_w, conv_b.reshape(1, -1), bif, avg, gn_w.reshape(1, REC_W))


def _mla_prep_kernel(cq_ref, ckv_ref, cos_ref, sin_ref, qnw_ref, kvnw_ref, wuq_ref, wuk_ref, wuvt_ref,
                     q_ref, k_ref, vt_ref):
    scale = (MLA_NOPE + MLA_ROPE) ** -0.5 * LOG2_E
    cos = cos_ref[...]
    sin = sin_ref[...]
    cos_all = jnp.concatenate([cos] * MLA_HEADS, axis=1)
    sin_all = jnp.concatenate([sin] * MLA_HEADS, axis=1)

    cq = cq_ref[...]
    cqn = cq * lax.rsqrt(jnp.mean(cq * cq, axis=-1, keepdims=True) + LN_EPS) * qnw_ref[...]
    q = _dot(cqn.astype(BF16), wuq_ref[...])
    q = q * cos_all + _swap_halves(q, MLA_ROPE // 2) * sin_all
    q_ref[...] = (q * scale).astype(BF16)

    ckv = ckv_ref[:, 0:MLA_KV_RANK]
    kvn = ckv * lax.rsqrt(jnp.mean(ckv * ckv, axis=-1, keepdims=True) + LN_EPS) * kvnw_ref[...]
    kvb = kvn.astype(BF16)
    misc = ckv_ref[:, MLA_KV_RANK:MLA_KV_RANK + LANES]
    lane = lax.broadcasted_iota(jnp.int32, (1, LANES), 1)
    kr = misc * cos + _swap_halves(misc, MLA_ROPE // 2) * sin
    kr = jnp.where((lane >= MISC_KR) & (lane < MISC_KR + MLA_ROPE), kr, 0.0)
    k = _dot(kvb, wuk_ref[...]) + jnp.concatenate([kr] * MLA_HEADS, axis=1)
    k_ref[...] = k.astype(BF16)
    vt = _dot_nt(wuvt_ref[...], kvb).astype(BF16)
    for j in range(vt_ref.shape[0]):
        vt_ref[j] = vt[:, j * ATT_TILE:(j + 1) * ATT_TILE]


def _mla_prep(h2, seq, cos_m, sin_m, qn_w, kvn_w, wuq, wuk, wuvt, tm=512):
    t = h2.shape[0]
    nblk = seq // tm
    hw = MLA_HEADS * LANES
    const = lambda shape: pl.BlockSpec(shape, lambda i: (0,) * len(shape))
    return pl.pallas_call(
        _mla_prep_kernel,
        grid=(t // tm,),
        in_specs=[pl.BlockSpec((tm, MLA_Q_RANK), lambda i: (i, COL_CQ // MLA_Q_RANK)),
                  pl.BlockSpec((tm, 2 * LANES), lambda i: (i, COL_CKV // (2 * LANES))),
                  pl.BlockSpec((tm, LANES), lambda i: (i % nblk, 0)),
                  pl.BlockSpec((tm, LANES), lambda i: (i % nblk, 0)),
                  const((1, MLA_Q_RANK)), const((1, MLA_KV_RANK)),
                  const((MLA_Q_RANK, hw)), const((MLA_KV_RANK, hw)), const((MLA_W, MLA_KV_RANK))],
        out_specs=[pl.BlockSpec((tm, hw), lambda i: (i, 0)), pl.BlockSpec((tm, hw), lambda i: (i, 0)),
                   pl.BlockSpec((tm // ATT_TILE, MLA_W, ATT_TILE), lambda i: (i, 0, 0))],
        out_shape=[jax.ShapeDtypeStruct((t, hw), BF16), jax.ShapeDtypeStruct((t, hw), BF16),
                   jax.ShapeDtypeStruct((t // ATT_TILE, MLA_W, ATT_TILE), BF16)],
        compiler_params=_cparams("parallel"),
        name="mla_prep",
    )(h2, h2, cos_m, sin_m, qn_w.reshape(1, -1), kvn_w.reshape(1, -1), wuq, wuk, wuvt)


def _mla_attn_kernel(q_ref, k_ref, vt_ref, o_ref):
    tq = ATT_TILE
    qi = pl.program_id(1)
    key = lax.broadcasted_iota(jnp.int32, (tq, tq), 0)
    qry = lax.broadcasted_iota(jnp.int32, (tq, tq), 1)
    diag_ok = key < (qry // MASK_CHUNK + 1) * MASK_CHUNK

    def step(kt, carry, masked):
        rows = pl.ds(pl.multiple_of(kt * tq, tq), tq)
        sts = []
        for j in range(MLA_HEADS):
            cols = slice(j * LANES, (j + 1) * LANES)
            sts.append(_dot_nt(k_ref[rows, cols], q_ref[:, cols]))
        pts = []
        for j in range(MLA_HEADS):
            m, l, _ = carry[j]
            st = sts[j]
            if masked:
                st = jnp.where(diag_ok, st, NEG)
            m_new = jnp.maximum(m, jnp.max(st, axis=0, keepdims=True))
            alpha = jnp.exp2(m - m_new)
            pt = jnp.exp2(st - m_new)
            l = alpha * l + jnp.sum(pt, axis=0, keepdims=True)
            pts.append((m_new, l, alpha, pt.astype(BF16)))
        new = []
        for j in range(MLA_HEADS):
            m_new, l, alpha, pt = pts[j]
            acc = alpha * carry[j][2] + _dot(vt_ref[kt, j * MLA_V:(j + 1) * MLA_V, :], pt)
            new.append((m_new, l, acc))
        return tuple(new)

    init = tuple((jnp.full((1, tq), NEG, F32), jnp.zeros((1, tq), F32), jnp.zeros((MLA_V, tq), F32))
                 for _ in range(MLA_HEADS))
    carry = lax.fori_loop(0, qi, lambda kt, c: step(kt, c, False), init)
    carry = step(qi, carry, True)
    for p in range(MLA_HEADS // 2):
        (_, l0, a0), (_, l1, a1) = carry[2 * p], carry[2 * p + 1]
        o_ref[:, p * LANES:(p + 1) * LANES] = jnp.concatenate([a0 / l0, a1 / l1], axis=0).T


def _mla_attn(q, k, vt, batch, seq):
    t = q.shape[0]
    nq = seq // ATT_TILE
    hw = MLA_HEADS * LANES
    return pl.pallas_call(
        _mla_attn_kernel,
        grid=(batch, nq),
        in_specs=[pl.BlockSpec((ATT_TILE, hw), lambda b, i: (b * nq + i, 0)),
                  pl.BlockSpec((seq, hw), lambda b, i: (b, 0)),
                  pl.BlockSpec((nq, MLA_W, ATT_TILE), lambda b, i: (b, 0, 0))],
        out_specs=pl.BlockSpec((ATT_TILE, MLA_W), lambda b, i: (b * nq + i, 0)),
        out_shape=jax.ShapeDtypeStruct((t, MLA_W), F32),
        compiler_params=_cparams("parallel", "arbitrary"),
        name="mla_attn",
    )(q, k, vt)


def _outproj_kernel(ret_ref, att_ref, ml_ref, x_ref, w_ref, lnw_ref, lnb_ref, o_ref):
    mix = (_dot(ret_ref[...].astype(BF16), w_ref[0:REC_W, :])
           + _dot(att_ref[...].astype(BF16), w_ref[REC_W:REC_W + MLA_W, :])
           + _dot(ml_ref[...].astype(BF16), w_ref[REC_W + MLA_W:, :]))
    o_ref[...] = _layer_norm(DN_ALPHA * x_ref[...] + mix, lnw_ref[...], lnb_ref[...])


def _outproj_ln(ret, att, ml, x2, w_bf16, ln_w, ln_b, tm=512):
    t = x2.shape[0]
    const = lambda shape: pl.BlockSpec(shape, lambda i: (0,) * len(shape))
    row = lambda w: pl.BlockSpec((tm, w), lambda i: (i, 0))
    return pl.pallas_call(
        _outproj_kernel,
        grid=(t // tm,),
        in_specs=[row(REC_W), row(MLA_W), row(REC_W), row(D_MODEL),
                  const((D_MODEL, D_MODEL)), const((1, D_MODEL)), const((1, D_MODEL))],
        out_specs=row(D_MODEL),
        out_shape=jax.ShapeDtypeStruct((t, D_MODEL), F32),
        compiler_params=_cparams("parallel"),
        name="outproj_ln",
    )(ret, att, ml, x2, w_bf16, ln_w.reshape(1, -1), ln_b.reshape(1, -1))


def _ffn_kernel(x_ref, wg_ref, wu_ref, wd_ref, lnw_ref, lnb_ref, o_ref, xb_ref, acc_ref):
    f = pl.program_id(1)

    @pl.when(f == 0)
    def _():
        xb_ref[...] = x_ref[...].astype(BF16)
        acc_ref[...] = jnp.zeros_like(acc_ref)

    xb = xb_ref[...]
    g = _dot(xb, wg_ref[...])
    u = _dot(xb, wu_ref[...])
    acc_ref[...] += _dot((_silu(g) * u).astype(BF16), wd_ref[...])

    @pl.when(f == pl.num_programs(1) - 1)
    def _():
        o_ref[...] = _layer_norm(DN_ALPHA * x_ref[...] + acc_ref[...], lnw_ref[...], lnb_ref[...])


def _ffn_ln(x2, wg, wu, wd, ln_w, ln_b, tm=1024, tf=256):
    t = x2.shape[0]
    dff = wg.shape[1]
    const = lambda shape: pl.BlockSpec(shape, lambda i, f: (0,) * len(shape))
    return pl.pallas_call(
        _ffn_kernel,
        grid=(t // tm, dff // tf),
        in_specs=[pl.BlockSpec((tm, D_MODEL), lambda i, f: (i, 0)),
                  pl.BlockSpec((D_MODEL, tf), lambda i, f: (0, f)),
                  pl.BlockSpec((D_MODEL, tf), lambda i, f: (0, f)),
                  pl.BlockSpec((tf, D_MODEL), lambda i, f: (f, 0)),
                  const((1, D_MODEL)), const((1, D_MODEL))],
        out_specs=pl.BlockSpec((tm, D_MODEL), lambda i, f: (i, 0)),
        out_shape=jax.ShapeDtypeStruct((t, D_MODEL), F32),
        scratch_shapes=[pltpu.VMEM((tm, D_MODEL), BF16), pltpu.VMEM((tm, D_MODEL), F32)],
        compiler_params=_cparams("parallel", "arbitrary"),
        name="ffn_ln",
    )(x2, wg, wu, wd, ln_w.reshape(1, -1), ln_b.reshape(1, -1))


RT_E1, RT_E2, RT_W1, RT_W2, RT_R1, RT_R2 = 0, 1, 2, 3, 4, 5


def _router_kernel(x_ref, r_ref, route_ref, count_ref, carry_ref):
    @pl.when(pl.program_id(0) == 0)
    def _():
        carry_ref[...] = jnp.zeros_like(carry_ref)

    xh, xm, xl = _split3(x_ref[...])
    rh, rm, rl = r_ref[0], r_ref[1], r_ref[2]
    logits = (_dot(xh, rh) + _dot(xh, rm) + _dot(xm, rh)
              + _dot(xh, rl) + _dot(xm, rm) + _dot(xl, rh))
    tm = logits.shape[0]
    lane = lax.broadcasted_iota(jnp.int32, logits.shape, 1)
    lg = jnp.where(lane < N_EXPERTS, logits, -jnp.inf)
    m1 = jnp.max(lg, axis=-1, keepdims=True)
    i1 = jnp.min(jnp.where(lg == m1, lane, LANES), axis=-1, keepdims=True)
    lg2 = jnp.where(lane == i1, -jnp.inf, lg)
    m2 = jnp.max(lg2, axis=-1, keepdims=True)
    i2 = jnp.min(jnp.where(lg2 == m2, lane, LANES), axis=-1, keepdims=True)
    e2 = jnp.exp(m2 - m1)
    w1 = 1.0 / (1.0 + e2)
    w2 = e2 / (1.0 + e2)

    sel = ((lane == i1) | (lane == i2)).astype(F32)
    before = (lax.broadcasted_iota(jnp.int32, (tm, tm), 0) > lax.broadcasted_iota(jnp.int32, (tm, tm), 1))
    ranks = _dot(before.astype(BF16), sel.astype(BF16)) + carry_ref[...]
    r1 = jnp.sum(jnp.where(lane == i1, ranks, 0.0), axis=-1, keepdims=True)
    r2 = jnp.sum(jnp.where(lane == i2, ranks, 0.0), axis=-1, keepdims=True)
    carry_ref[...] += jnp.sum(sel, axis=0, keepdims=True)
    count_ref[...] = carry_ref[...]

    rec = jnp.zeros(logits.shape, F32)
    for pos, val in ((RT_E1, i1.astype(F32)), (RT_E2, i2.astype(F32)), (RT_W1, w1), (RT_W2, w2),
                     (RT_R1, r1), (RT_R2, r2)):
        rec = jnp.where(lane == pos, val, rec)
    route_ref[...] = rec


def _router(x2, router, tm=1024):
    t = x2.shape[0]
    rp = jnp.zeros((D_MODEL, LANES), F32).at[:, 0:N_EXPERTS].set(router)
    rh = rp.astype(BF16)
    r1 = rp - rh.astype(F32)
    rm = r1.astype(BF16)
    rl = (r1 - rm.astype(F32)).astype(BF16)
    r3 = jnp.stack([rh, rm, rl])
    return pl.pallas_call(
        _router_kernel,
        grid=(t // tm,),
        in_specs=[pl.BlockSpec((tm, D_MODEL), lambda i: (i, 0)),
                  pl.BlockSpec((3, D_MODEL, LANES), lambda i: (0, 0, 0))],
        out_specs=[pl.BlockSpec((tm, LANES), lambda i: (i, 0)),
                   pl.BlockSpec((1, LANES), lambda i: (0, 0))],
        out_shape=[jax.ShapeDtypeStruct((t, LANES), F32), jax.ShapeDtypeStruct((1, LANES), F32)],
        scratch_shapes=[pltpu.VMEM((1, LANES), F32)],
        compiler_params=_cparams("arbitrary"),
        name="router",
    )(x2, r3)


MOE_TM = 1024


def _route_plan(route, counts, t):
    cnt = counts[0, :N_EXPERTS].astype(jnp.int32)
    padded = ((cnt + MOE_TM - 1) // MOE_TM) * MOE_TM
    ends = jnp.cumsum(padded)
    offs = ends - padded
    experts = jnp.arange(N_EXPERTS, dtype=jnp.int32)

    def dest(e_lane, r_lane):
        e = route[:, e_lane].astype(jnp.int32)
        off = jnp.sum(jnp.where(e[:, None] == experts[None, :], offs[None, :], 0), axis=1)
        return off + route[:, r_lane].astype(jnp.int32)

    n_tiles = (2 * t) // MOE_TM + N_EXPERTS
    n_used = ends[-1] // MOE_TM
    tile = jnp.minimum(jnp.arange(n_tiles, dtype=jnp.int32), n_used - 1)
    tile_expert = jnp.sum((tile[:, None] * MOE_TM >= ends[None, :]).astype(jnp.int32), axis=1)
    return dest(RT_E1, RT_R1), dest(RT_E2, RT_R2), tile_expert, n_used.reshape(1), n_tiles


def _dispatch_kernel(d1_ref, d2_ref, x_ref, zero_ref, xs_ref, sem):
    del zero_ref
    tm = x_ref.shape[0]
    base = pl.program_id(0) * tm

    def copies(r):
        src = x_ref.at[pl.ds(r, 1), :]
        return (pltpu.make_async_copy(src, xs_ref.at[pl.ds(d1_ref[base + r], 1), :], sem),
                pltpu.make_async_copy(src, xs_ref.at[pl.ds(d2_ref[base + r], 1), :], sem))

    def start(r, c):
        for cp in copies(r):
            cp.start()
        return c

    def wait(r, c):
        for cp in copies(r):
            cp.wait()
        return c

    lax.fori_loop(0, tm, start, 0)
    lax.fori_loop(0, tm, wait, 0)


def _dispatch(x2, d1, d2, n_rows, tm=512):
    t = x2.shape[0]
    zeros = jnp.zeros((n_rows, D_MODEL), F32)
    return pl.pallas_call(
        _dispatch_kernel,
        grid_spec=pltpu.PrefetchScalarGridSpec(
            num_scalar_prefetch=2,
            grid=(t // tm,),
            in_specs=[pl.BlockSpec((tm, D_MODEL), lambda i, d1, d2: (i, 0)),
                      pl.BlockSpec(memory_space=pl.ANY)],
            out_specs=pl.BlockSpec(memory_space=pl.ANY),
            scratch_shapes=[pltpu.SemaphoreType.DMA(())]),
        out_shape=jax.ShapeDtypeStruct((n_rows, D_MODEL), F32),
        input_output_aliases={3: 0},
        compiler_params=_cparams("arbitrary"),
        name="moe_dispatch",
    )(d1, d2, x2, zeros)


def _moe_ffn_kernel(te_ref, nu_ref, x_ref, wg_ref, wu_ref, wd_ref, o_ref, xb_ref, acc_ref):
    del te_ref
    i = pl.program_id(0)
    f = pl.program_id(1)

    @pl.when(i < nu_ref[0])
    def _():
        @pl.when(f == 0)
        def _():
            xb_ref[...] = x_ref[...].astype(BF16)
            acc_ref[...] = jnp.zeros_like(acc_ref)

        xb = xb_ref[...]
        g = _dot(xb, wg_ref[...].astype(BF16))
        u = _dot(xb, wu_ref[...].astype(BF16))
        acc_ref[...] += _dot((_silu(g) * u).astype(BF16), wd_ref[...].astype(BF16))

        @pl.when(f == pl.num_programs(1) - 1)
        def _():
            o_ref[...] = acc_ref[...]

    @pl.when(i >= nu_ref[0])
    def _():
        o_ref[...] = jnp.zeros_like(o_ref)


def _moe_ffn(xs, tile_expert, n_used, n_tiles, wg, wu, wd, tf=512):
    dff = wg.shape[2]
    nf = dff // tf
    row = lambda i, f, te, nu: (jnp.minimum(i, nu[0] - 1), 0)
    fcl = lambda i, f, nu: jnp.where(i < nu[0], f, nf - 1)
    return pl.pallas_call(
        _moe_ffn_kernel,
        grid_spec=pltpu.PrefetchScalarGridSpec(
            num_scalar_prefetch=2,
            grid=(n_tiles, nf),
            in_specs=[pl.BlockSpec((MOE_TM, D_MODEL), row),
                      pl.BlockSpec((None, D_MODEL, tf), lambda i, f, te, nu: (te[i], 0, fcl(i, f, nu))),
                      pl.BlockSpec((None, D_MODEL, tf), lambda i, f, te, nu: (te[i], 0, fcl(i, f, nu))),
                      pl.BlockSpec((None, tf, D_MODEL), lambda i, f, te, nu: (te[i], fcl(i, f, nu), 0))],
            out_specs=pl.BlockSpec((MOE_TM, D_MODEL), lambda i, f, te, nu: (i, 0)),
            scratch_shapes=[pltpu.VMEM((MOE_TM, D_MODEL), BF16), pltpu.VMEM((MOE_TM, D_MODEL), F32)]),
        out_shape=jax.ShapeDtypeStruct(xs.shape, F32),
        compiler_params=_cparams("arbitrary", "arbitrary"),
        name="moe_ffn",
    )(tile_expert, n_used, xs, wg, wu, wd)


def _combine_kernel(d1_ref, d2_ref, x_ref, route_ref, ys_ref, lnw_ref, lnb_ref, o_ref, b1_ref, b2_ref, sem):
    tm = x_ref.shape[0]
    base = pl.program_id(0) * tm

    def copies(r):
        return (pltpu.make_async_copy(ys_ref.at[pl.ds(d1_ref[base + r], 1), :], b1_ref.at[pl.ds(r, 1), :], sem),
                pltpu.make_async_copy(ys_ref.at[pl.ds(d2_ref[base + r], 1), :], b2_ref.at[pl.ds(r, 1), :], sem))

    def start(r, c):
        for cp in copies(r):
            cp.start()
        return c

    def wait(r, c):
        for cp in copies(r):
            cp.wait()
        return c

    lax.fori_loop(0, tm, start, 0)
    lax.fori_loop(0, tm, wait, 0)
    rt = route_ref[...]
    y = rt[:, RT_W1:RT_W1 + 1] * b1_ref[...] + rt[:, RT_W2:RT_W2 + 1] * b2_ref[...]
    o_ref[...] = _layer_norm(DN_ALPHA * x_ref[...] + y, lnw_ref[...], lnb_ref[...])


def _combine_ln(x2, route, ys, d1, d2, ln_w, ln_b, tm=256):
    t = x2.shape[0]
    const = lambda shape: pl.BlockSpec(shape, lambda i, d1, d2: (0,) * len(shape))
    return pl.pallas_call(
        _combine_kernel,
        grid_spec=pltpu.PrefetchScalarGridSpec(
            num_scalar_prefetch=2,
            grid=(t // tm,),
            in_specs=[pl.BlockSpec((tm, D_MODEL), lambda i, d1, d2: (i, 0)),
                      pl.BlockSpec((tm, LANES), lambda i, d1, d2: (i, 0)),
                      pl.BlockSpec(memory_space=pl.ANY),
                      const((1, D_MODEL)), const((1, D_MODEL))],
            out_specs=pl.BlockSpec((tm, D_MODEL), lambda i, d1, d2: (i, 0)),
            scratch_shapes=[pltpu.VMEM((tm, D_MODEL), F32), pltpu.VMEM((tm, D_MODEL), F32),
                            pltpu.SemaphoreType.DMA(())]),
        out_shape=jax.ShapeDtypeStruct((t, D_MODEL), F32),
        compiler_params=_cparams("arbitrary"),
        name="moe_combine_ln",
    )(d1, d2, x2, route, ys, ln_w.reshape(1, -1), ln_b.reshape(1, -1))


def _moe_ln(x2, router, wg, wu, wd, ln_w, ln_b):
    t = x2.shape[0]
    route, counts = _router(x2, router)
    d1, d2, tile_expert, n_used, n_tiles = _route_plan(route, counts, t)
    xs = _dispatch(x2, d1, d2, n_tiles * MOE_TM)
    ys = _moe_ffn(xs, tile_expert, n_used, n_tiles, wg, wu, wd)
    return _combine_ln(x2, route, ys, d1, d2, ln_w, ln_b)


def _layout_w_in(w):
    sizes = (REC_W, REC_W, REC_W, REC_W, MLA_Q_RANK, MLA_KV_RANK, MLA_ROPE,
             REC_W, REC_W, REC_W, REC_W, N_RHEADS, N_RHEADS)
    offs = np.concatenate([[0], np.cumsum(sizes)])
    part = lambda j: w[:, offs[j]:offs[j + 1]]
    z = lambda n: jnp.zeros((w.shape[0], n), w.dtype)
    misc = jnp.concatenate([part(11), part(12), z(MISC_KR - 2 * N_RHEADS), part(6),
                            z(LANES - MISC_KR - MLA_ROPE)], axis=1)
    cols = [part(0), part(1), part(2), part(3), part(7), part(8), part(9), part(10), part(4), part(5), misc]
    return jnp.concatenate(cols, axis=1).astype(BF16)


def _layout_mla(w_uq, w_ukv):
    r = w_uq.shape[0]
    uq = w_uq.reshape(r, MLA_HEADS, MLA_NOPE + MLA_ROPE)
    uq = jnp.pad(uq, ((0, 0), (0, 0), (0, LANES - MLA_NOPE - MLA_ROPE))).reshape(r, MLA_HEADS * LANES)
    r = w_ukv.shape[0]
    ukv = w_ukv.reshape(r, MLA_HEADS, MLA_NOPE + MLA_V)
    uk = jnp.pad(ukv[:, :, :MLA_NOPE], ((0, 0), (0, 0), (0, LANES - MLA_NOPE))).reshape(r, MLA_HEADS * LANES)
    uvt = ukv[:, :, MLA_NOPE:].reshape(r, MLA_W).T
    return uq.astype(BF16), uk.astype(BF16), uvt.astype(BF16)


def kernel(x, w_in, ret_gn_w, mla_q_norm_w, mla_w_uq, mla_kv_norm_w, mla_w_ukv, mlstm_conv_w, mlstm_conv_b,
           mlstm_b_i, mlstm_b_f, mlstm_gn_w, w_out, ln1_w, ln1_b, ffn_w_gate, ffn_w_up, ffn_w_down,
           moe_router, moe_w_gate, moe_w_up, moe_w_down, ln2_w, ln2_b):
    bsz, seq, d = x.shape
    t = bsz * seq
    cos_r, sin_r, cos_m, sin_m = _rope_tables(seq)
    x2 = x.reshape(t, d)
    for l in range(DEPTH):
        h2 = _inproj(x2, _layout_w_in(w_in[l]))
        h3 = h2.reshape(bsz, seq, D_IN_PAD)
        ret = _retention(h3, cos_r, sin_r, ret_gn_w[l]).reshape(t, REC_W)
        ml = _mlstm(h3, mlstm_conv_w[l], mlstm_conv_b[l], mlstm_b_i[l], mlstm_b_f[l],
                    mlstm_gn_w[l]).reshape(t, REC_W)
        wuq, wuk, wuvt = _layout_mla(mla_w_uq[l], mla_w_ukv[l])
        q, k, vt = _mla_prep(h2, seq, cos_m, sin_m, mla_q_norm_w[l], mla_kv_norm_w[l], wuq, wuk, wuvt)
        att = _mla_attn(q, k, vt, bsz, seq)
        x2 = _outproj_ln(ret, att, ml, x2, w_out[l].astype(BF16), ln1_w[l], ln1_b[l])
        if l % 2 == 0:
            j = l // 2
            x2 = _ffn_ln(x2, ffn_w_gate[j].astype(BF16), ffn_w_up[j].astype(BF16),
                         ffn_w_down[j].astype(BF16), ln2_w[l], ln2_b[l])
        else:
            j = l // 2
            x2 = _moe_ln(x2, moe_router[j], moe_w_gate[j], moe_w_up[j], moe_w_down[j], ln2_w[l], ln2_b[l])
    return x2.reshape(bsz, seq, d)
```

```python
import functools

import numpy as np
import jax
import jax.numpy as jnp
from jax import lax
from jax.experimental import pallas as pl
from jax.experimental.pallas import tpu as pltpu

F32 = jnp.float32
BF16 = jnp.bfloat16

D_MODEL = 1024
DEPTH = 2
ROPE_BASE = 10000.0
LN_EPS = 1e-5
NEG = -1e30

HEAD_DIM = 64
N_RHEADS = 4
REC_W = N_RHEADS * HEAD_DIM
MLA_HEADS = 8
MLA_NOPE = 64
MLA_ROPE = 32
MLA_V = 64
MLA_Q_RANK = 256
MLA_KV_RANK = 128
MLA_W = MLA_HEADS * MLA_V
MLSTM_CONV = 4
MASK_CHUNK = 64
REC_CHUNK = 256
ATT_TILE = 256
N_EXPERTS = 8
LANES = 128

DN_ALPHA = (2 * DEPTH) ** 0.25
LOG2_E = 1.4426950408889634

COL_RET = 0
COL_MLSTM = 1024
COL_CQ = 2048
COL_CKV = 2304
COL_MISC = 2432
D_IN_PAD = 2560
MISC_KR = 64

VMEM_LIMIT = 56 * 1024 * 1024


def _cparams(*sem):
    return pltpu.CompilerParams(dimension_semantics=sem, vmem_limit_bytes=VMEM_LIMIT)


def _layer_norm(y, w, b):
    mu = jnp.mean(y, axis=-1, keepdims=True)
    d = y - mu
    var = jnp.mean(d * d, axis=-1, keepdims=True)
    return d * lax.rsqrt(var + LN_EPS) * w + b


def _silu(x):
    return x * (1.0 / (1.0 + jnp.exp(-x)))


def _sigmoid(x):
    return 1.0 / (1.0 + jnp.exp(-x))


def _split3(x):
    hi = x.astype(BF16)
    r1 = x - hi.astype(F32)
    mid = r1.astype(BF16)
    lo = (r1 - mid.astype(F32)).astype(BF16)
    return hi, mid, lo


def _dot(a, b):
    return jnp.dot(a, b, preferred_element_type=F32)


def _dot_nt(a, b):
    return lax.dot_general(a, b, (((1,), (1,)), ((), ())), preferred_element_type=F32)


def _dot_exact_rhs(x, m_bf16):
    hi, mid, lo = _split3(x)
    return _dot(hi, m_bf16) + _dot(mid, m_bf16) + _dot(lo, m_bf16)


def _swap_halves(x, half):
    n = x.shape[-1]
    lane = lax.broadcasted_iota(jnp.int32, x.shape, x.ndim - 1)
    first = (lane % (2 * half)) < half
    return jnp.where(first, pltpu.roll(x, n - half, x.ndim - 1), pltpu.roll(x, half, x.ndim - 1))


def _head_norm(x, avg_bf16, w):
    mu = _dot_exact_rhs(x, avg_bf16)
    d = x - mu
    var = _dot_exact_rhs(d * d, avg_bf16)
    return d * lax.rsqrt(var + LN_EPS) * w


def _expand_heads(v, lane_head):
    out = jnp.zeros((v.shape[0], REC_W), F32)
    for h in range(N_RHEADS):
        out = jnp.where(lane_head == h, v[:, h:h + 1], out)
    return out


def _rope_tables(seq):
    pos = np.arange(seq, dtype=np.float64)[:, None]
    half = HEAD_DIM // 2
    inv = ROPE_BASE ** (-np.arange(half, dtype=np.float64) / half)
    ang = pos * inv[None, :]
    cos_r = np.tile(np.concatenate([np.cos(ang), np.cos(ang)], -1), (1, N_RHEADS))
    sin_r = np.tile(np.concatenate([-np.sin(ang), np.sin(ang)], -1), (1, N_RHEADS))
    half = MLA_ROPE // 2
    inv = ROPE_BASE ** (-np.arange(half, dtype=np.float64) / half)
    ang = pos * inv[None, :]
    cos_m = np.ones((seq, LANES))
    sin_m = np.zeros((seq, LANES))
    cos_m[:, MISC_KR:MISC_KR + MLA_ROPE] = np.concatenate([np.cos(ang), np.cos(ang)], -1)
    sin_m[:, MISC_KR:MISC_KR + MLA_ROPE] = np.concatenate([-np.sin(ang), np.sin(ang)], -1)
    f = lambda a: jnp.asarray(a.astype(np.float32))
    return f(cos_r), f(sin_r), f(cos_m), f(sin_m)


def _retention_tables():
    L = REC_CHUNK
    log_gamma = np.log(1.0 - 2.0 ** (-5.0 - np.arange(N_RHEADS, dtype=np.float64)))
    idx = np.arange(L, dtype=np.float64)
    diff = idx[:, None] - idx[None, :]
    dmask = np.where(diff >= 0, np.exp(diff[None] * log_gamma[:, None, None]), 0.0)
    lane_lg = np.repeat(log_gamma, HEAD_DIM)[None, :]
    qw = np.exp((idx + 1.0)[:, None] * lane_lg)
    kw = np.exp((L - 1 - idx)[:, None] * lane_lg)
    cd = np.exp(L * lane_lg)
    f = lambda a: jnp.asarray(a.astype(np.float32))
    return f(dmask), f(qw), f(kw), f(cd)


def _head_avg_matrix():
    h = np.arange(REC_W) // HEAD_DIM
    return jnp.asarray((h[:, None] == h[None, :]).astype(np.float32) / HEAD_DIM, dtype=BF16)


def _inproj_kernel(x_ref, w_ref, o_ref):
    o_ref[...] = _dot(x_ref[...].astype(BF16), w_ref[...])


def _inproj(x2, w_bf16, tm=512):
    t, k = x2.shape
    n = w_bf16.shape[1]
    return pl.pallas_call(
        _inproj_kernel,
        grid=(t // tm,),
        in_specs=[pl.BlockSpec((tm, k), lambda i: (i, 0)),
                  pl.BlockSpec((k, n), lambda i: (0, 0))],
        out_specs=pl.BlockSpec((tm, n), lambda i: (i, 0)),
        out_shape=jax.ShapeDtypeStruct((t, n), F32),
        compiler_params=_cparams("parallel"),
        name="inproj",
    )(x2, w_bf16)


def _retention_kernel(h_ref, cos_ref, sin_ref, dmask_ref, qw_ref, kw_ref, cd_ref, avg_ref, gnw_ref,
                      o_ref, state_ref):
    L = REC_CHUNK
    nc = h_ref.shape[0] // L
    lane_head = lax.broadcasted_iota(jnp.int32, (1, REC_W), 1) // HEAD_DIM
    row_head = lax.broadcasted_iota(jnp.int32, (REC_W, REC_W), 0) // HEAD_DIM
    col_head = lax.broadcasted_iota(jnp.int32, (REC_W, REC_W), 1) // HEAD_DIM
    block_diag = row_head == col_head
    state_ref[...] = jnp.zeros_like(state_ref)

    def chunk(c, carry):
        r0 = pl.multiple_of(c * L, L)
        rows = pl.ds(r0, L)
        cos = cos_ref[rows, :]
        sin = sin_ref[rows, :]
        q = h_ref[rows, 0:REC_W]
        k = h_ref[rows, REC_W:2 * REC_W]
        v = h_ref[rows, 2 * REC_W:3 * REC_W]
        g = h_ref[rows, 3 * REC_W:4 * REC_W]
        q = q * cos + _swap_halves(q, HEAD_DIM // 2) * sin
        k = (k * cos + _swap_halves(k, HEAD_DIM // 2) * sin) * (HEAD_DIM ** -0.5)
        kb = k.astype(BF16)
        vb = v.astype(BF16)
        inner = jnp.zeros((L, REC_W), F32)
        for h in range(N_RHEADS):
            mh = lane_head == h
            qh = jnp.where(mh, q, 0.0).astype(BF16)
            s = _dot_nt(qh, kb) * dmask_ref[h]
            o = _dot(s.astype(BF16), vb)
            inner = jnp.where(mh, o, inner)
        state = state_ref[...]
        cross = _dot(q.astype(BF16), state.astype(BF16)) * qw_ref[...]
        ret = inner + cross
        kwt = (k * kw_ref[...]).T.astype(BF16)
        loc = _dot(kwt, vb)
        state_ref[...] = cd_ref[...] * state + jnp.where(block_diag, loc, 0.0)
        o_ref[rows, :] = _silu(g) * _head_norm(ret, avg_ref[...], gnw_ref[...])
        return carry

    lax.fori_loop(0, nc, chunk, 0)


def _retention(h3, cos_r, sin_r, gn_w):
    b, s, _ = h3.shape
    dmask, qw, kw, cd = _retention_tables()
    avg = _head_avg_matrix()
    const = lambda shape: pl.BlockSpec(shape, lambda i: (0,) * len(shape))
    return pl.pallas_call(
        _retention_kernel,
        grid=(b,),
        in_specs=[pl.BlockSpec((None, s, 4 * REC_W), lambda i: (i, 0, COL_RET // (4 * REC_W))),
                  const((s, REC_W)), const((s, REC_W)),
                  const((N_RHEADS, REC_CHUNK, REC_CHUNK)), const((REC_CHUNK, REC_W)),
                  const((REC_CHUNK, REC_W)), const((1, REC_W)), const((REC_W, REC_W)), const((1, REC_W))],
        out_specs=pl.BlockSpec((None, s, REC_W), lambda i: (i, 0, 0)),
        out_shape=jax.ShapeDtypeStruct((b, s, REC_W), F32),
        scratch_shapes=[pltpu.VMEM((REC_W, REC_W), F32)],
        compiler_params=_cparams("parallel"),
        name="retention",
    )(h3, cos_r, sin_r, dmask, qw, kw, cd, avg, gn_w.reshape(1, REC_W))


def _mlstm_kernel(h_ref, misc_ref, convw_ref, convb_ref, bif_ref, avg_ref, gnw_ref,
                  o_ref, c_ref, n_ref, m_ref):
    L = REC_CHUNK
    s_len = h_ref.shape[0]
    nc = s_len // L
    lane_head = lax.broadcasted_iota(jnp.int32, (1, REC_W), 1) // HEAD_DIM
    row_head = lax.broadcasted_iota(jnp.int32, (REC_W, REC_W), 0) // HEAD_DIM
    col_head = lax.broadcasted_iota(jnp.int32, (REC_W, REC_W), 1) // HEAD_DIM
    block_diag = row_head == col_head
    n_mask = (lax.broadcasted_iota(jnp.int32, (REC_W, LANES), 0) // HEAD_DIM
              == lax.broadcasted_iota(jnp.int32, (REC_W, LANES), 1))
    lane128 = lax.broadcasted_iota(jnp.int32, (1, LANES), 1)
    gate_lanes = lane128 < N_RHEADS
    ri = lax.broadcasted_iota(jnp.int32, (L, L), 0)
    ci = lax.broadcasted_iota(jnp.int32, (L, L), 1)
    causal = ri >= ci
    tri = causal.astype(BF16)

    row8 = lax.broadcasted_iota(jnp.int32, (8, 2 * REC_W), 0)

    c_ref[...] = jnp.zeros_like(c_ref)
    n_ref[...] = jnp.zeros_like(n_ref)
    m_ref[...] = jnp.full(m_ref.shape, NEG, F32)

    def chunk(c, carry):
        r0 = pl.multiple_of(c * L, L)
        rows = pl.ds(r0, L)

        x = h_ref[rows, 0:2 * REC_W]
        prev = h_ref[pl.ds(pl.multiple_of(jnp.maximum(r0 - 8, 0), 8), 8), 0:2 * REC_W]
        prev = jnp.where(c > 0, prev, 0.0)
        acc = x * convw_ref[MLSTM_CONV - 1:MLSTM_CONV, :] + convb_ref[...]
        for sft in range(1, MLSTM_CONV):
            xs = pltpu.roll(x, sft, 0)
            head = jnp.where(row8 < sft, pltpu.roll(prev, sft, 0), xs[0:8, :])
            xs = jnp.concatenate([head, xs[8:, :]], axis=0)
            acc = acc + xs * convw_ref[MLSTM_CONV - 1 - sft:MLSTM_CONV - sft, :]
        qk = _silu(acc)
        q = qk[:, 0:REC_W]
        k = qk[:, REC_W:2 * REC_W] * (HEAD_DIM ** -0.5)
        v = h_ref[rows, 2 * REC_W:3 * REC_W]
        og = h_ref[rows, 3 * REC_W:4 * REC_W]

        gates = misc_ref[rows, :] + bif_ref[...]
        ic_c = jnp.where(gate_lanes, gates, 0.0)
        fg = pltpu.roll(gates, LANES - N_RHEADS, 1)
        lf_c = jnp.where(gate_lanes, jnp.minimum(fg, 0.0) - jnp.log1p(jnp.exp(-jnp.abs(fg))), 0.0)
        qb = q.astype(BF16)
        kb = k.astype(BF16)
        vb = v.astype(BF16)

        bcum = _dot_exact_rhs_left(tri, lf_c)
        gsum = bcum[L - 1:L, :]
        m_s = m_ref[...]
        a = gsum - bcum + ic_c
        amax = jnp.max(a, axis=0, keepdims=True)
        rvec = (ic_c - bcum).T
        inter = bcum + m_s

        c_state = c_ref[...]
        n_state = n_ref[...]
        q_c = _dot(qb, c_state.astype(BF16))
        q_n = _dot(qb, n_state.astype(BF16))

        hcell = jnp.zeros((L, REC_W), F32)
        for h in range(N_RHEADS):
            mh = lane_head == h
            log_d = jnp.where(causal, bcum[:, h:h + 1] + rvec[h:h + 1, :], NEG)
            inter_h = inter[:, h:h + 1]
            m_t = jnp.maximum(inter_h, jnp.max(log_d, axis=-1, keepdims=True))
            dw = jnp.exp(log_d - m_t)
            qh = jnp.where(mh, q, 0.0).astype(BF16)
            cmat = _dot_nt(qh, kb) * dw
            si = jnp.exp(inter_h - m_t)
            num = _dot(cmat.astype(BF16), vb) + si * q_c
            den = jnp.sum(cmat, axis=-1, keepdims=True) + si * q_n[:, h:h + 1]
            denom = jnp.maximum(jnp.abs(den), jnp.exp(-m_t))
            hcell = jnp.where(mh, num / denom, hcell)

        wa = jnp.exp(a - amax)
        m_new = jnp.maximum(gsum + m_s, amax)
        sp = jnp.exp(gsum + m_s - m_new)
        sl = jnp.exp(amax - m_new)
        kwt = (k * _expand_heads(wa, lane_head)).T.astype(BF16)
        loc_c = _dot(kwt, vb)
        loc_n = _dot(kwt, jnp.ones((L, LANES), BF16))
        c_ref[...] = (c_state * _expand_heads(sp, lane_head)
                      + jnp.where(block_diag, loc_c, 0.0) * _expand_heads(sl, lane_head))
        n_ref[...] = n_state * sp + jnp.where(n_mask, loc_n, 0.0) * sl
        m_ref[...] = m_new

        o_ref[rows, :] = _head_norm(_sigmoid(og) * hcell, avg_ref[...], gnw_ref[...])
        return carry

    lax.fori_loop(0, nc, chunk, 0)


def _dot_exact_rhs_left(m_bf16, x):
    hi, mid, lo = _split3(x)
    return _dot(m_bf16, hi) + _dot(m_bf16, mid) + _dot(m_bf16, lo)


def _mlstm(h3, conv_w, conv_b, b_i, b_f, gn_w):
    b, s, _ = h3.shape
    avg = _head_avg_matrix()
    bif = jnp.zeros((1, LANES), F32).at[0, 0:N_RHEADS].set(b_i).at[0, N_RHEADS:2 * N_RHEADS].set(b_f)
    const = lambda shape: pl.BlockSpec(shape, lambda i: (0,) * len(shape))
    return pl.pallas_call(
        _mlstm_kernel,
        grid=(b,),
        in_specs=[pl.BlockSpec((None, s, 4 * REC_W), lambda i: (i, 0, COL_MLSTM // (4 * REC_W))),
                  pl.BlockSpec((None, s, LANES), lambda i: (i, 0, COL_MISC // LANES)),
                  const((MLSTM_CONV, 2 * REC_W)), const((1, 2 * REC_W)), const((1, LANES)),
                  const((REC_W, REC_W)), const((1, REC_W))],
        out_specs=pl.BlockSpec((None, s, REC_W), lambda i: (i, 0, 0)),
        out_shape=jax.ShapeDtypeStruct((b, s, REC_W), F32),
        scratch_shapes=[pltpu.VMEM((REC_W, REC_W), F32),
                        pltpu.VMEM((REC_W, LANES), F32), pltpu.VMEM((1, LANES), F32)],
        compiler_params=_cparams("parallel"),
        name="mlstm",
    )(h3, h3, conv_w, conv_b.reshape(1, -1), bif, avg, gn_w.reshape(1, REC_W))


def _mla_prep_kernel(cq_ref, ckv_ref, cos_ref, sin_ref, qnw_ref, kvnw_ref, wuq_ref, wuk_ref, wuvt_ref,
                     q_ref, k_ref, vt_ref):
    scale = (MLA_NOPE + MLA_ROPE) ** -0.5 * LOG2_E
    cos = cos_ref[...]
    sin = sin_ref[...]
    cos_all = jnp.concatenate([cos] * MLA_HEADS, axis=1)
    sin_all = jnp.concatenate([sin] * MLA_HEADS, axis=1)

    cq = cq_ref[...]
    cqn = cq * lax.rsqrt(jnp.mean(cq * cq, axis=-1, keepdims=True) + LN_EPS) * qnw_ref[...]
    q = _dot(cqn.astype(BF16), wuq_ref[...])
    q = q * cos_all + _swap_halves(q, MLA_ROPE // 2) * sin_all
    q_ref[...] = (q * scale).astype(BF16)

    ckv = ckv_ref[:, 0:MLA_KV_RANK]
    kvn = ckv * lax.rsqrt(jnp.mean(ckv * ckv, axis=-1, keepdims=True) + LN_EPS) * kvnw_ref[...]
    kvb = kvn.astype(BF16)
    misc = ckv_ref[:, MLA_KV_RANK:MLA_KV_RANK + LANES]
    lane = lax.broadcasted_iota(jnp.int32, (1, LANES), 1)
    kr = misc * cos + _swap_halves(misc, MLA_ROPE // 2) * sin
    kr = jnp.where((lane >= MISC_KR) & (lane < MISC_KR + MLA_ROPE), kr, 0.0)
    k = _dot(kvb, wuk_ref[...]) + jnp.concatenate([kr] * MLA_HEADS, axis=1)
    k_ref[...] = k.astype(BF16)
    vt = _dot_nt(wuvt_ref[...], kvb).astype(BF16)
    for j in range(vt_ref.shape[0]):
        vt_ref[j] = vt[:, j * ATT_TILE:(j + 1) * ATT_TILE]


def _mla_prep(h2, seq, cos_m, sin_m, qn_w, kvn_w, wuq, wuk, wuvt, tm=512):
    t = h2.shape[0]
    nblk = seq // tm
    hw = MLA_HEADS * LANES
    const = lambda shape: pl.BlockSpec(shape, lambda i: (0,) * len(shape))
    return pl.pallas_call(
        _mla_prep_kernel,
        grid=(t // tm,),
        in_specs=[pl.BlockSpec((tm, MLA_Q_RANK), lambda i: (i, COL_CQ // MLA_Q_RANK)),
                  pl.BlockSpec((tm, 2 * LANES), lambda i: (i, COL_CKV // (2 * LANES))),
                  pl.BlockSpec((tm, LANES), lambda i: (i % nblk, 0)),
                  pl.BlockSpec((tm, LANES), lambda i: (i % nblk, 0)),
                  const((1, MLA_Q_RANK)), const((1, MLA_KV_RANK)),
                  const((MLA_Q_RANK, hw)), const((MLA_KV_RANK, hw)), const((MLA_W, MLA_KV_RANK))],
        out_specs=[pl.BlockSpec((tm, hw), lambda i: (i, 0)), pl.BlockSpec((tm, hw), lambda i: (i, 0)),
                   pl.BlockSpec((tm // ATT_TILE, MLA_W, ATT_TILE), lambda i: (i, 0, 0))],
        out_shape=[jax.ShapeDtypeStruct((t, hw), BF16), jax.ShapeDtypeStruct((t, hw), BF16),
                   jax.ShapeDtypeStruct((t // ATT_TILE, MLA_W, ATT_TILE), BF16)],
        compiler_params=_cparams("parallel"),
        name="mla_prep",
    )(h2, h2, cos_m, sin_m, qn_w.reshape(1, -1), kvn_w.reshape(1, -1), wuq, wuk, wuvt)


def _mla_attn_kernel(q_ref, k_ref, vt_ref, o_ref):
    tq = ATT_TILE
    qi = pl.program_id(1)
    key = lax.broadcasted_iota(jnp.int32, (tq, tq), 0)
    qry = lax.broadcasted_iota(jnp.int32, (tq, tq), 1)
    diag_ok = key < (qry // MASK_CHUNK + 1) * MASK_CHUNK

    def step(kt, carry, masked):
        rows = pl.ds(pl.multiple_of(kt * tq, tq), tq)
        sts = []
        for j in range(MLA_HEADS):
            cols = slice(j * LANES, (j + 1) * LANES)
            sts.append(_dot_nt(k_ref[rows, cols], q_ref[:, cols]))
        pts = []
        for j in range(MLA_HEADS):
            m, l, _ = carry[j]
            st = sts[j]
            if masked:
                st = jnp.where(diag_ok, st, NEG)
            m_new = jnp.maximum(m, jnp.max(st, axis=0, keepdims=True))
            alpha = jnp.exp2(m - m_new)
            pt = jnp.exp2(st - m_new)
            l = alpha * l + jnp.sum(pt, axis=0, keepdims=True)
            pts.append((m_new, l, alpha, pt.astype(BF16)))
        new = []
        for j in range(MLA_HEADS):
            m_new, l, alpha, pt = pts[j]
            acc = alpha * carry[j][2] + _dot(vt_ref[kt, j * MLA_V:(j + 1) * MLA_V, :], pt)
            new.append((m_new, l, acc))
        return tuple(new)

    init = tuple((jnp.full((1, tq), NEG, F32), jnp.zeros((1, tq), F32), jnp.zeros((MLA_V, tq), F32))
                 for _ in range(MLA_HEADS))
    carry = lax.fori_loop(0, qi, lambda kt, c: step(kt, c, False), init)
    carry = step(qi, carry, True)
    for p in range(MLA_HEADS // 2):
        (_, l0, a0), (_, l1, a1) = carry[2 * p], carry[2 * p + 1]
        o_ref[:, p * LANES:(p + 1) * LANES] = jnp.concatenate([a0 / l0, a1 / l1], axis=0).T


def _mla_attn(q, k, vt, batch, seq):
    t = q.shape[0]
    nq = seq // ATT_TILE
    hw = MLA_HEADS * LANES
    return pl.pallas_call(
        _mla_attn_kernel,
        grid=(batch, nq),
        in_specs=[pl.BlockSpec((ATT_TILE, hw), lambda b, i: (b * nq + i, 0)),
                  pl.BlockSpec((seq, hw), lambda b, i: (b, 0)),
                  pl.BlockSpec((nq, MLA_W, ATT_TILE), lambda b, i: (b, 0, 0))],
        out_specs=pl.BlockSpec((ATT_TILE, MLA_W), lambda b, i: (b * nq + i, 0)),
        out_shape=jax.ShapeDtypeStruct((t, MLA_W), F32),
        compiler_params=_cparams("parallel", "arbitrary"),
        name="mla_attn",
    )(q, k, vt)


def _outproj_kernel(ret_ref, att_ref, ml_ref, x_ref, w_ref, lnw_ref, lnb_ref, o_ref):
    mix = (_dot(ret_ref[...].astype(BF16), w_ref[0:REC_W, :])
           + _dot(att_ref[...].astype(BF16), w_ref[REC_W:REC_W + MLA_W, :])
           + _dot(ml_ref[...].astype(BF16), w_ref[REC_W + MLA_W:, :]))
    o_ref[...] = _layer_norm(DN_ALPHA * x_ref[...] + mix, lnw_ref[...], lnb_ref[...])


def _outproj_ln(ret, att, ml, x2, w_bf16, ln_w, ln_b, tm=512):
    t = x2.shape[0]
    const = lambda shape: pl.BlockSpec(shape, lambda i: (0,) * len(shape))
    row = lambda w: pl.BlockSpec((tm, w), lambda i: (i, 0))
    return pl.pallas_call(
        _outproj_kernel,
        grid=(t // tm,),
        in_specs=[row(REC_W), row(MLA_W), row(REC_W), row(D_MODEL),
                  const((D_MODEL, D_MODEL)), const((1, D_MODEL)), const((1, D_MODEL))],
        out_specs=row(D_MODEL),
        out_shape=jax.ShapeDtypeStruct((t, D_MODEL), F32),
        compiler_params=_cparams("parallel"),
        name="outproj_ln",
    )(ret, att, ml, x2, w_bf16, ln_w.reshape(1, -1), ln_b.reshape(1, -1))


MXU_COLS = 256


def _swiglu_chunk(xb, wg_ref, wu_ref, wd_ref):
    tf = wg_ref.shape[1]
    parts = [slice(s, min(s + MXU_COLS, tf)) for s in range(0, tf, MXU_COLS)]
    gu = [(_dot(xb, wg_ref[:, p].astype(BF16)), _dot(xb, wu_ref[:, p].astype(BF16))) for p in parts]
    hs = [(_silu(g) * u).astype(BF16) for g, u in gu]
    out = _dot(hs[0], wd_ref[parts[0], :].astype(BF16))
    for h, p in zip(hs[1:], parts[1:]):
        out = out + _dot(h, wd_ref[p, :].astype(BF16))
    return out


def _ffn_kernel(x_ref, wg_ref, wu_ref, wd_ref, lnw_ref, lnb_ref, o_ref, xb_ref, acc_ref):
    f = pl.program_id(1)

    @pl.when(f == 0)
    def _():
        xb_ref[...] = x_ref[...].astype(BF16)
        acc_ref[...] = jnp.zeros_like(acc_ref)

    acc_ref[...] += _swiglu_chunk(xb_ref[...], wg_ref, wu_ref, wd_ref)

    @pl.when(f == pl.num_programs(1) - 1)
    def _():
        o_ref[...] = _layer_norm(DN_ALPHA * x_ref[...] + acc_ref[...], lnw_ref[...], lnb_ref[...])


def _ffn_ln(x2, wg, wu, wd, ln_w, ln_b, tm=1024, tf=1408):
    t = x2.shape[0]
    dff = wg.shape[1]
    const = lambda shape: pl.BlockSpec(shape, lambda i, f: (0,) * len(shape))
    return pl.pallas_call(
        _ffn_kernel,
        grid=(t // tm, dff // tf),
        in_specs=[pl.BlockSpec((tm, D_MODEL), lambda i, f: (i, 0)),
                  pl.BlockSpec((D_MODEL, tf), lambda i, f: (0, f)),
                  pl.BlockSpec((D_MODEL, tf), lambda i, f: (0, f)),
                  pl.BlockSpec((tf, D_MODEL), lambda i, f: (f, 0)),
                  const((1, D_MODEL)), const((1, D_MODEL))],
        out_specs=pl.BlockSpec((tm, D_MODEL), lambda i, f: (i, 0)),
        out_shape=jax.ShapeDtypeStruct((t, D_MODEL), F32),
        scratch_shapes=[pltpu.VMEM((tm, D_MODEL), BF16), pltpu.VMEM((tm, D_MODEL), F32)],
        compiler_params=_cparams("parallel", "arbitrary"),
        name="ffn_ln",
    )(x2, wg, wu, wd, ln_w.reshape(1, -1), ln_b.reshape(1, -1))


RT_E1, RT_E2, RT_W1, RT_W2, RT_R1, RT_R2 = 0, 1, 2, 3, 4, 5


def _router_kernel(x_ref, r_ref, route_ref, count_ref, carry_ref):
    @pl.when(pl.program_id(0) == 0)
    def _():
        carry_ref[...] = jnp.zeros_like(carry_ref)

    x = x_ref[...]
    xh = x.astype(BF16)
    xm = (x - xh.astype(F32)).astype(BF16)
    rh, rm = r_ref[0], r_ref[1]
    logits = _dot(xh, rh) + (_dot(xh, rm) + _dot(xm, rh))
    tm = logits.shape[0]
    lane = lax.broadcasted_iota(jnp.int32, logits.shape, 1)
    lg = jnp.where(lane < N_EXPERTS, logits, -jnp.inf)
    m1 = jnp.max(lg, axis=-1, keepdims=True)
    i1 = jnp.min(jnp.where(lg == m1, lane, LANES), axis=-1, keepdims=True)
    lg2 = jnp.where(lane == i1, -jnp.inf, lg)
    m2 = jnp.max(lg2, axis=-1, keepdims=True)
    i2 = jnp.min(jnp.where(lg2 == m2, lane, LANES), axis=-1, keepdims=True)
    e2 = jnp.exp(m2 - m1)
    w1 = 1.0 / (1.0 + e2)
    w2 = e2 / (1.0 + e2)

    sel = ((lane == i1) | (lane == i2)).astype(F32)
    before = (lax.broadcasted_iota(jnp.int32, (tm, tm), 0) > lax.broadcasted_iota(jnp.int32, (tm, tm), 1))
    ranks = _dot(before.astype(BF16), sel.astype(BF16)) + carry_ref[...]
    r1 = jnp.sum(jnp.where(lane == i1, ranks, 0.0), axis=-1, keepdims=True)
    r2 = jnp.sum(jnp.where(lane == i2, ranks, 0.0), axis=-1, keepdims=True)
    carry_ref[...] += jnp.sum(sel, axis=0, keepdims=True)
    count_ref[...] = carry_ref[...]

    rec = jnp.zeros(logits.shape, F32)
    for pos, val in ((RT_E1, i1.astype(F32)), (RT_E2, i2.astype(F32)), (RT_W1, w1), (RT_W2, w2),
                     (RT_R1, r1), (RT_R2, r2)):
        rec = jnp.where(lane == pos, val, rec)
    route_ref[...] = rec


def _router(x2, router, tm=1024):
    t = x2.shape[0]
    rp = jnp.zeros((D_MODEL, LANES), F32).at[:, 0:N_EXPERTS].set(router)
    rh = rp.astype(BF16)
    rm = (rp - rh.astype(F32)).astype(BF16)
    r3 = jnp.stack([rh, rm])
    return pl.pallas_call(
        _router_kernel,
        grid=(t // tm,),
        in_specs=[pl.BlockSpec((tm, D_MODEL), lambda i: (i, 0)),
                  pl.BlockSpec((2, D_MODEL, LANES), lambda i: (0, 0, 0))],
        out_specs=[pl.BlockSpec((tm, LANES), lambda i: (i, 0)),
                   pl.BlockSpec((1, LANES), lambda i: (0, 0))],
        out_shape=[jax.ShapeDtypeStruct((t, LANES), F32), jax.ShapeDtypeStruct((1, LANES), F32)],
        scratch_shapes=[pltpu.VMEM((1, LANES), F32)],
        compiler_params=_cparams("arbitrary"),
        name="router",
    )(x2, r3)


MOE_TM = 1024


def _route_plan(route, counts, t):
    cnt = counts[0, :N_EXPERTS].astype(jnp.int32)
    padded = ((cnt + MOE_TM - 1) // MOE_TM) * MOE_TM
    ends = jnp.cumsum(padded)
    offs = ends - padded
    experts = jnp.arange(N_EXPERTS, dtype=jnp.int32)

    def dest(e_lane, r_lane):
        e = route[:, e_lane].astype(jnp.int32)
        off = jnp.sum(jnp.where(e[:, None] == experts[None, :], offs[None, :], 0), axis=1)
        return off + route[:, r_lane].astype(jnp.int32)

    n_tiles = (2 * t) // MOE_TM + N_EXPERTS
    n_used = ends[-1] // MOE_TM
    tile = jnp.minimum(jnp.arange(n_tiles, dtype=jnp.int32), n_used - 1)
    tile_expert = jnp.sum((tile[:, None] * MOE_TM >= ends[None, :]).astype(jnp.int32), axis=1)
    group_end = jnp.sum(jnp.where(tile_expert[:, None] == experts[None, :], (offs + cnt)[None, :], 0), axis=1)
    tile_rows = jnp.clip(group_end - tile * MOE_TM, 0, MOE_TM)
    return dest(RT_E1, RT_R1), dest(RT_E2, RT_R2), tile_expert, n_used.reshape(1), tile_rows, n_tiles


def _dispatch_kernel(d1_ref, d2_ref, x_ref, zero_ref, xs_ref, sem):
    del zero_ref
    tm = x_ref.shape[0]
    base = pl.program_id(0) * tm

    def copies(r):
        src = x_ref.at[pl.ds(r, 1), :]
        return (pltpu.make_async_copy(src, xs_ref.at[pl.ds(d1_ref[base + r], 1), :], sem),
                pltpu.make_async_copy(src, xs_ref.at[pl.ds(d2_ref[base + r], 1), :], sem))

    def start(r, c):
        for cp in copies(r):
            cp.start()
        return c

    def wait(r, c):
        for cp in copies(r):
            cp.wait()
        return c

    lax.fori_loop(0, tm, start, 0, unroll=8)
    lax.fori_loop(0, tm, wait, 0, unroll=8)


def _dispatch(x2, d1, d2, n_rows, tm=512):
    t = x2.shape[0]
    zeros = jnp.zeros((n_rows, D_MODEL), F32)
    return pl.pallas_call(
        _dispatch_kernel,
        grid_spec=pltpu.PrefetchScalarGridSpec(
            num_scalar_prefetch=2,
            grid=(t // tm,),
            in_specs=[pl.BlockSpec((tm, D_MODEL), lambda i, d1, d2: (i, 0)),
                      pl.BlockSpec(memory_space=pl.ANY)],
            out_specs=pl.BlockSpec(memory_space=pl.ANY),
            scratch_shapes=[pltpu.SemaphoreType.DMA(())]),
        out_shape=jax.ShapeDtypeStruct((n_rows, D_MODEL), F32),
        input_output_aliases={3: 0},
        compiler_params=_cparams("arbitrary"),
        name="moe_dispatch",
    )(d1, d2, x2, zeros)


def _moe_ffn_kernel(te_ref, nu_ref, nr_ref, x_ref, wg_ref, wu_ref, wd_ref, o_ref, xb_ref, acc_ref):
    del te_ref
    i = pl.program_id(0)
    f = pl.program_id(1)
    used = i < nu_ref[0]
    half = MOE_TM // 2

    def swiglu_rows(rows):
        @pl.when(f == 0)
        def _():
            xb_ref[0:rows, :] = x_ref[0:rows, :].astype(BF16)
            acc_ref[...] = jnp.zeros_like(acc_ref)

        acc_ref[0:rows, :] += _swiglu_chunk(xb_ref[0:rows, :], wg_ref, wu_ref, wd_ref)

    @pl.when(used & (nr_ref[i] > half))
    def _():
        swiglu_rows(MOE_TM)

    @pl.when(used & (nr_ref[i] <= half))
    def _():
        swiglu_rows(half)

    @pl.when(used & (f == pl.num_programs(1) - 1))
    def _():
        o_ref[...] = acc_ref[...]

    @pl.when(jnp.logical_not(used))
    def _():
        o_ref[...] = jnp.zeros_like(o_ref)


def _moe_ffn(xs, tile_expert, n_used, tile_rows, n_tiles, wg, wu, wd, tf=512):
    dff = wg.shape[2]
    nf = dff // tf
    row = lambda i, f, te, nu, nr: (jnp.minimum(i, nu[0] - 1), 0)
    fcl = lambda i, f, nu: jnp.where(i < nu[0], f, nf - 1)
    return pl.pallas_call(
        _moe_ffn_kernel,
        grid_spec=pltpu.PrefetchScalarGridSpec(
            num_scalar_prefetch=3,
            grid=(n_tiles, nf),
            in_specs=[pl.BlockSpec((MOE_TM, D_MODEL), row),
                      pl.BlockSpec((None, D_MODEL, tf), lambda i, f, te, nu, nr: (te[i], 0, fcl(i, f, nu))),
                      pl.BlockSpec((None, D_MODEL, tf), lambda i, f, te, nu, nr: (te[i], 0, fcl(i, f, nu))),
                      pl.BlockSpec((None, tf, D_MODEL), lambda i, f, te, nu, nr: (te[i], fcl(i, f, nu), 0))],
            out_specs=pl.BlockSpec((MOE_TM, D_MODEL), lambda i, f, te, nu, nr: (i, 0)),
            scratch_shapes=[pltpu.VMEM((MOE_TM, D_MODEL), BF16), pltpu.VMEM((MOE_TM, D_MODEL), F32)]),
        out_shape=jax.ShapeDtypeStruct(xs.shape, F32),
        compiler_params=_cparams("arbitrary", "arbitrary"),
        name="moe_ffn",
    )(tile_expert, n_used, tile_rows, xs, wg, wu, wd)


def _combine_kernel(d1_ref, d2_ref, x_ref, route_ref, ys_ref, lnw_ref, lnb_ref, o_ref, b1_ref, b2_ref, sem):
    tm = x_ref.shape[0]
    base = pl.program_id(0) * tm

    def copies(r):
        return (pltpu.make_async_copy(ys_ref.at[pl.ds(d1_ref[base + r], 1), :], b1_ref.at[pl.ds(r, 1), :], sem),
                pltpu.make_async_copy(ys_ref.at[pl.ds(d2_ref[base + r], 1), :], b2_ref.at[pl.ds(r, 1), :], sem))

    def start(r, c):
        for cp in copies(r):
            cp.start()
        return c

    def wait(r, c):
        for cp in copies(r):
            cp.wait()
        return c

    lax.fori_loop(0, tm, start, 0, unroll=8)
    lax.fori_loop(0, tm, wait, 0, unroll=8)
    rt = route_ref[...]
    y = rt[:, RT_W1:RT_W1 + 1] * b1_ref[...] + rt[:, RT_W2:RT_W2 + 1] * b2_ref[...]
    o_ref[...] = _layer_norm(DN_ALPHA * x_ref[...] + y, lnw_ref[...], lnb_ref[...])


def _combine_ln(x2, route, ys, d1, d2, ln_w, ln_b, tm=256):
    t = x2.shape[0]
    const = lambda shape: pl.BlockSpec(shape, lambda i, d1, d2: (0,) * len(shape))
    return pl.pallas_call(
        _combine_kernel,
        grid_spec=pltpu.PrefetchScalarGridSpec(
            num_scalar_prefetch=2,
            grid=(t // tm,),
            in_specs=[pl.BlockSpec((tm, D_MODEL), lambda i, d1, d2: (i, 0)),
                      pl.BlockSpec((tm, LANES), lambda i, d1, d2: (i, 0)),
                      pl.BlockSpec(memory_space=pl.ANY),
                      const((1, D_MODEL)), const((1, D_MODEL))],
            out_specs=pl.BlockSpec((tm, D_MODEL), lambda i, d1, d2: (i, 0)),
            scratch_shapes=[pltpu.VMEM((tm, D_MODEL), F32), pltpu.VMEM((tm, D_MODEL), F32),
                            pltpu.SemaphoreType.DMA(())]),
        out_shape=jax.ShapeDtypeStruct((t, D_MODEL), F32),
        compiler_params=_cparams("arbitrary"),
        name="moe_combine_ln",
    )(d1, d2, x2, route, ys, ln_w.reshape(1, -1), ln_b.reshape(1, -1))


def _moe_ln(x2, router, wg, wu, wd, ln_w, ln_b):
    t = x2.shape[0]
    route, counts = _router(x2, router)
    d1, d2, tile_expert, n_used, tile_rows, n_tiles = _route_plan(route, counts, t)
    xs = _dispatch(x2, d1, d2, n_tiles * MOE_TM)
    ys = _moe_ffn(xs, tile_expert, n_used, tile_rows, n_tiles, wg, wu, wd)
    return _combine_ln(x2, route, ys, d1, d2, ln_w, ln_b)


def _layout_w_in(w):
    sizes = (REC_W, REC_W, REC_W, REC_W, MLA_Q_RANK, MLA_KV_RANK, MLA_ROPE,
             REC_W, REC_W, REC_W, REC_W, N_RHEADS, N_RHEADS)
    offs = np.concatenate([[0], np.cumsum(sizes)])
    part = lambda j: w[:, offs[j]:offs[j + 1]]
    z = lambda n: jnp.zeros((w.shape[0], n), w.dtype)
    misc = jnp.concatenate([part(11), part(12), z(MISC_KR - 2 * N_RHEADS), part(6),
                            z(LANES - MISC_KR - MLA_ROPE)], axis=1)
    cols = [part(0), part(1), part(2), part(3), part(7), part(8), part(9), part(10), part(4), part(5), misc]
    return jnp.concatenate(cols, axis=1).astype(BF16)


def _layout_mla(w_uq, w_ukv):
    r = w_uq.shape[0]
    uq = w_uq.reshape(r, MLA_HEADS, MLA_NOPE + MLA_ROPE)
    uq = jnp.pad(uq, ((0, 0), (0, 0), (0, LANES - MLA_NOPE - MLA_ROPE))).reshape(r, MLA_HEADS * LANES)
    r = w_ukv.shape[0]
    ukv = w_ukv.reshape(r, MLA_HEADS, MLA_NOPE + MLA_V)
    uk = jnp.pad(ukv[:, :, :MLA_NOPE], ((0, 0), (0, 0), (0, LANES - MLA_NOPE))).reshape(r, MLA_HEADS * LANES)
    uvt = ukv[:, :, MLA_NOPE:].reshape(r, MLA_W).T
    return uq.astype(BF16), uk.astype(BF16), uvt.astype(BF16)


def kernel(x, w_in, ret_gn_w, mla_q_norm_w, mla_w_uq, mla_kv_norm_w, mla_w_ukv, mlstm_conv_w, mlstm_conv_b,
           mlstm_b_i, mlstm_b_f, mlstm_gn_w, w_out, ln1_w, ln1_b, ffn_w_gate, ffn_w_up, ffn_w_down,
           moe_router, moe_w_gate, moe_w_up, moe_w_down, ln2_w, ln2_b):
    bsz, seq, d = x.shape
    t = bsz * seq
    cos_r, sin_r, cos_m, sin_m = _rope_tables(seq)
    x2 = x.reshape(t, d)
    for l in range(DEPTH):
        h2 = _inproj(x2, _layout_w_in(w_in[l]))
        h3 = h2.reshape(bsz, seq, D_IN_PAD)
        ret = _retention(h3, cos_r, sin_r, ret_gn_w[l]).reshape(t, REC_W)
        ml = _mlstm(h3, mlstm_conv_w[l], mlstm_conv_b[l], mlstm_b_i[l], mlstm_b_f[l],
                    mlstm_gn_w[l]).reshape(t, REC_W)
        wuq, wuk, wuvt = _layout_mla(mla_w_uq[l], mla_w_ukv[l])
        q, k, vt = _mla_prep(h2, seq, cos_m, sin_m, mla_q_norm_w[l], mla_kv_norm_w[l], wuq, wuk, wuvt)
        att = _mla_attn(q, k, vt, bsz, seq)
        x2 = _outproj_ln(ret, att, ml, x2, w_out[l].astype(BF16), ln1_w[l], ln1_b[l])
        if l % 2 == 0:
            j = l // 2
            x2 = _ffn_ln(x2, ffn_w_gate[j].astype(BF16), ffn_w_up[j].astype(BF16),
                         ffn_w_down[j].astype(BF16), ln2_w[l], ln2_b[l])
        else:
            j = l // 2
            x2 = _moe_ln(x2, moe_router[j], moe_w_gate[j], moe_w_up[j], moe_w_down[j], ln2_w[l], ln2_b[l])
    return x2.reshape(bsz, seq, d)
```

```python
import functools

import numpy as np
import jax
import jax.numpy as jnp
from jax import lax
from jax.experimental import pallas as pl
from jax.experimental.pallas import tpu as pltpu

F32 = jnp.float32
BF16 = jnp.bfloat16

D_MODEL = 1024
DEPTH = 2
ROPE_BASE = 10000.0
LN_EPS = 1e-5
NEG = -1e30

HEAD_DIM = 64
N_RHEADS = 4
REC_W = N_RHEADS * HEAD_DIM
MLA_HEADS = 8
MLA_NOPE = 64
MLA_ROPE = 32
MLA_V = 64
MLA_Q_RANK = 256
MLA_KV_RANK = 128
MLA_W = MLA_HEADS * MLA_V
MLSTM_CONV = 4
MASK_CHUNK = 64
REC_CHUNK = 256
ATT_TILE = 256
ATT_GROUP = 4
ATT_VROWS = 80
N_EXPERTS = 8
LANES = 128

DN_ALPHA = (2 * DEPTH) ** 0.25
LOG2_E = 1.4426950408889634

COL_RET = 0
COL_MLSTM = 1024
COL_CQ = 2048
COL_CKV = 2304
COL_MISC = 2432
D_IN_PAD = 2560
MISC_KR = 64

VMEM_LIMIT = 56 * 1024 * 1024


def _cparams(*sem):
    return pltpu.CompilerParams(dimension_semantics=sem, vmem_limit_bytes=VMEM_LIMIT)


def _layer_norm(y, w, b):
    mu = jnp.mean(y, axis=-1, keepdims=True)
    d = y - mu
    var = jnp.mean(d * d, axis=-1, keepdims=True)
    return d * lax.rsqrt(var + LN_EPS) * w + b


def _silu(x):
    return x * (1.0 / (1.0 + jnp.exp(-x)))


def _sigmoid(x):
    return 1.0 / (1.0 + jnp.exp(-x))


def _split2(x):
    hi = x.astype(BF16)
    lo = (x - hi.astype(F32)).astype(BF16)
    return hi, lo


def _dot(a, b):
    return jnp.dot(a, b, preferred_element_type=F32)


def _dot_nt(a, b):
    return lax.dot_general(a, b, (((1,), (1,)), ((), ())), preferred_element_type=F32)


def _dot_exact_rhs(x, m_bf16):
    hi, lo = _split2(x)
    return _dot(hi, m_bf16) + _dot(lo, m_bf16)


def _swap_halves(x, half):
    n = x.shape[-1]
    lane = lax.broadcasted_iota(jnp.int32, x.shape, x.ndim - 1)
    first = (lane % (2 * half)) < half
    return jnp.where(first, pltpu.roll(x, n - half, x.ndim - 1), pltpu.roll(x, half, x.ndim - 1))


def _head_norm(x, avg_bf16, w):
    mu = _dot_exact_rhs(x, avg_bf16)
    d = x - mu
    var = _dot_exact_rhs(d * d, avg_bf16)
    return d * lax.rsqrt(var + LN_EPS) * w


def _expand_heads(v, lane_head):
    out = jnp.zeros((v.shape[0], REC_W), F32)
    for h in range(N_RHEADS):
        out = jnp.where(lane_head == h, v[:, h:h + 1], out)
    return out


def _rope_tables(seq):
    pos = np.arange(seq, dtype=np.float64)[:, None]
    half = HEAD_DIM // 2
    inv = ROPE_BASE ** (-np.arange(half, dtype=np.float64) / half)
    ang = pos * inv[None, :]
    cos_r = np.tile(np.concatenate([np.cos(ang), np.cos(ang)], -1), (1, N_RHEADS))
    sin_r = np.tile(np.concatenate([-np.sin(ang), np.sin(ang)], -1), (1, N_RHEADS))
    half = MLA_ROPE // 2
    inv = ROPE_BASE ** (-np.arange(half, dtype=np.float64) / half)
    ang = pos * inv[None, :]
    cos_m = np.ones((seq, LANES))
    sin_m = np.zeros((seq, LANES))
    cos_m[:, MISC_KR:MISC_KR + MLA_ROPE] = np.concatenate([np.cos(ang), np.cos(ang)], -1)
    sin_m[:, MISC_KR:MISC_KR + MLA_ROPE] = np.concatenate([-np.sin(ang), np.sin(ang)], -1)
    f = lambda a: jnp.asarray(a.astype(np.float32))
    return f(cos_r), f(sin_r), f(cos_m), f(sin_m)


def _retention_tables():
    L = REC_CHUNK
    log_gamma = np.log(1.0 - 2.0 ** (-5.0 - np.arange(N_RHEADS, dtype=np.float64)))
    idx = np.arange(L, dtype=np.float64)
    diff = idx[:, None] - idx[None, :]
    dmask = np.where(diff >= 0, np.exp(diff[None] * log_gamma[:, None, None]), 0.0)
    lane_lg = np.repeat(log_gamma, HEAD_DIM)[None, :]
    qw = np.exp((idx + 1.0)[:, None] * lane_lg)
    kw = np.exp((L - 1 - idx)[:, None] * lane_lg)
    cd = np.exp(L * lane_lg)
    f = lambda a: jnp.asarray(a.astype(np.float32))
    return f(dmask), f(qw), f(kw), f(cd)


def _head_avg_matrix():
    h = np.arange(REC_W) // HEAD_DIM
    return jnp.asarray((h[:, None] == h[None, :]).astype(np.float32) / HEAD_DIM, dtype=BF16)


def _inproj_kernel(x_ref, w_ref, o_ref):
    o_ref[...] = _dot(x_ref[...].astype(BF16), w_ref[...])


def _inproj(x2, w_bf16, tm=512):
    t, k = x2.shape
    n = w_bf16.shape[1]
    return pl.pallas_call(
        _inproj_kernel,
        grid=(t // tm,),
        in_specs=[pl.BlockSpec((tm, k), lambda i: (i, 0)),
                  pl.BlockSpec((k, n), lambda i: (0, 0))],
        out_specs=pl.BlockSpec((tm, n), lambda i: (i, 0)),
        out_shape=jax.ShapeDtypeStruct((t, n), F32),
        compiler_params=_cparams("parallel"),
        name="inproj",
    )(x2, w_bf16)


def _retention_kernel(h_ref, cos_ref, sin_ref, dmask_ref, qw_ref, kw_ref, cd_ref, avg_ref, gnw_ref,
                      o_ref, state_ref):
    L = REC_CHUNK
    nc = h_ref.shape[0] // L
    lane_head = lax.broadcasted_iota(jnp.int32, (1, REC_W), 1) // HEAD_DIM
    row_head = lax.broadcasted_iota(jnp.int32, (REC_W, REC_W), 0) // HEAD_DIM
    col_head = lax.broadcasted_iota(jnp.int32, (REC_W, REC_W), 1) // HEAD_DIM
    block_diag = row_head == col_head
    state_ref[...] = jnp.zeros_like(state_ref)

    def chunk(c, carry):
        r0 = pl.multiple_of(c * L, L)
        rows = pl.ds(r0, L)
        cos = cos_ref[rows, :]
        sin = sin_ref[rows, :]
        q = h_ref[rows, 0:REC_W]
        k = h_ref[rows, REC_W:2 * REC_W]
        v = h_ref[rows, 2 * REC_W:3 * REC_W]
        g = h_ref[rows, 3 * REC_W:4 * REC_W]
        q = q * cos + _swap_halves(q, HEAD_DIM // 2) * sin
        k = (k * cos + _swap_halves(k, HEAD_DIM // 2) * sin) * (HEAD_DIM ** -0.5)
        kb = k.astype(BF16)
        vb = v.astype(BF16)
        heads = range(N_RHEADS)
        scores = [_dot_nt(jnp.where(lane_head == h, q, 0.0).astype(BF16), kb) for h in heads]
        probs = [(scores[h] * dmask_ref[h]).astype(BF16) for h in heads]
        inner = jnp.zeros((L, REC_W), F32)
        for h in heads:
            inner = jnp.where(lane_head == h, _dot(probs[h], vb), inner)
        state = state_ref[...]
        cross = _dot(q.astype(BF16), state.astype(BF16)) * qw_ref[...]
        ret = inner + cross
        kwt = (k * kw_ref[...]).T.astype(BF16)
        loc = _dot(kwt, vb)
        state_ref[...] = cd_ref[...] * state + jnp.where(block_diag, loc, 0.0)
        o_ref[rows, :] = _silu(g) * _head_norm(ret, avg_ref[...], gnw_ref[...])
        return carry

    lax.fori_loop(0, nc, chunk, 0)


def _retention(h3, cos_r, sin_r, gn_w):
    b, s, _ = h3.shape
    dmask, qw, kw, cd = _retention_tables()
    avg = _head_avg_matrix()
    const = lambda shape: pl.BlockSpec(shape, lambda i: (0,) * len(shape))
    return pl.pallas_call(
        _retention_kernel,
        grid=(b,),
        in_specs=[pl.BlockSpec((None, s, 4 * REC_W), lambda i: (i, 0, COL_RET // (4 * REC_W))),
                  const((s, REC_W)), const((s, REC_W)),
                  const((N_RHEADS, REC_CHUNK, REC_CHUNK)), const((REC_CHUNK, REC_W)),
                  const((REC_CHUNK, REC_W)), const((1, REC_W)), const((REC_W, REC_W)), const((1, REC_W))],
        out_specs=pl.BlockSpec((None, s, REC_W), lambda i: (i, 0, 0)),
        out_shape=jax.ShapeDtypeStruct((b, s, REC_W), F32),
        scratch_shapes=[pltpu.VMEM((REC_W, REC_W), F32)],
        compiler_params=_cparams("parallel"),
        name="retention",
    )(h3, cos_r, sin_r, dmask, qw, kw, cd, avg, gn_w.reshape(1, REC_W))


def _mlstm_kernel(h_ref, misc_ref, convw_ref, convb_ref, bif_ref, avg_ref, gnw_ref,
                  o_ref, c_ref, n_ref, m_ref):
    L = REC_CHUNK
    s_len = h_ref.shape[0]
    nc = s_len // L
    lane_head = lax.broadcasted_iota(jnp.int32, (1, REC_W), 1) // HEAD_DIM
    row_head = lax.broadcasted_iota(jnp.int32, (REC_W, REC_W), 0) // HEAD_DIM
    col_head = lax.broadcasted_iota(jnp.int32, (REC_W, REC_W), 1) // HEAD_DIM
    block_diag = row_head == col_head
    n_mask = (lax.broadcasted_iota(jnp.int32, (REC_W, LANES), 0) // HEAD_DIM
              == lax.broadcasted_iota(jnp.int32, (REC_W, LANES), 1))
    lane128 = lax.broadcasted_iota(jnp.int32, (1, LANES), 1)
    gate_lanes = lane128 < N_RHEADS
    ri = lax.broadcasted_iota(jnp.int32, (L, L), 0)
    ci = lax.broadcasted_iota(jnp.int32, (L, L), 1)
    causal = ri >= ci
    tri = causal.astype(BF16)

    row8 = lax.broadcasted_iota(jnp.int32, (8, 2 * REC_W), 0)

    c_ref[...] = jnp.zeros_like(c_ref)
    n_ref[...] = jnp.zeros_like(n_ref)
    m_ref[...] = jnp.full(m_ref.shape, NEG, F32)

    def chunk(c, carry):
        r0 = pl.multiple_of(c * L, L)
        rows = pl.ds(r0, L)

        x = h_ref[rows, 0:2 * REC_W]
        prev = h_ref[pl.ds(pl.multiple_of(jnp.maximum(r0 - 8, 0), 8), 8), 0:2 * REC_W]
        prev = jnp.where(c > 0, prev, 0.0)
        acc = x * convw_ref[MLSTM_CONV - 1:MLSTM_CONV, :] + convb_ref[...]
        for sft in range(1, MLSTM_CONV):
            xs = pltpu.roll(x, sft, 0)
            head = jnp.where(row8 < sft, pltpu.roll(prev, sft, 0), xs[0:8, :])
            xs = jnp.concatenate([head, xs[8:, :]], axis=0)
            acc = acc + xs * convw_ref[MLSTM_CONV - 1 - sft:MLSTM_CONV - sft, :]
        qk = _silu(acc)
        q = qk[:, 0:REC_W]
        k = qk[:, REC_W:2 * REC_W] * (HEAD_DIM ** -0.5)
        v = h_ref[rows, 2 * REC_W:3 * REC_W]
        og = h_ref[rows, 3 * REC_W:4 * REC_W]

        gates = misc_ref[rows, :] + bif_ref[...]
        ic_c = jnp.where(gate_lanes, gates, 0.0)
        fg = pltpu.roll(gates, LANES - N_RHEADS, 1)
        lf_c = jnp.where(gate_lanes, jnp.minimum(fg, 0.0) - jnp.log1p(jnp.exp(-jnp.abs(fg))), 0.0)
        qb = q.astype(BF16)
        kb = k.astype(BF16)
        vb = v.astype(BF16)

        bcum = _dot_exact_rhs_left(tri, lf_c)
        gsum = bcum[L - 1:L, :]
        m_s = m_ref[...]
        a = gsum - bcum + ic_c
        amax = jnp.max(a, axis=0, keepdims=True)
        rvec = (ic_c - bcum).T
        inter = bcum + m_s

        c_state = c_ref[...]
        n_state = n_ref[...]
        q_c = _dot(qb, c_state.astype(BF16))
        q_n = _dot(qb, n_state.astype(BF16))

        heads = range(N_RHEADS)
        scores = [_dot_nt(jnp.where(lane_head == h, q, 0.0).astype(BF16), kb) for h in heads]
        cmats, stats = [], []
        for h in heads:
            log_d = jnp.where(causal, bcum[:, h:h + 1] + rvec[h:h + 1, :], NEG)
            inter_h = inter[:, h:h + 1]
            m_t = jnp.maximum(inter_h, jnp.max(log_d, axis=-1, keepdims=True))
            cmat = scores[h] * jnp.exp(log_d - m_t)
            si = jnp.exp(inter_h - m_t)
            den = jnp.sum(cmat, axis=-1, keepdims=True) + si * q_n[:, h:h + 1]
            cmats.append(cmat.astype(BF16))
            stats.append((si, jnp.maximum(jnp.abs(den), jnp.exp(-m_t))))
        hcell = jnp.zeros((L, REC_W), F32)
        for h in heads:
            si, denom = stats[h]
            num = _dot(cmats[h], vb) + si * q_c
            hcell = jnp.where(lane_head == h, num / denom, hcell)

        wa = jnp.exp(a - amax)
        m_new = jnp.maximum(gsum + m_s, amax)
        sp = jnp.exp(gsum + m_s - m_new)
        sl = jnp.exp(amax - m_new)
        kwt = (k * _expand_heads(wa, lane_head)).T.astype(BF16)
        loc_c = _dot(kwt, vb)
        loc_n = _dot(kwt, jnp.ones((L, LANES), BF16))
        c_ref[...] = (c_state * _expand_heads(sp, lane_head)
                      + jnp.where(block_diag, loc_c, 0.0) * _expand_heads(sl, lane_head))
        n_ref[...] = n_state * sp + jnp.where(n_mask, loc_n, 0.0) * sl
        m_ref[...] = m_new

        o_ref[rows, :] = _head_norm(_sigmoid(og) * hcell, avg_ref[...], gnw_ref[...])
        return carry

    lax.fori_loop(0, nc, chunk, 0)


def _dot_exact_rhs_left(m_bf16, x):
    hi, lo = _split2(x)
    return _dot(m_bf16, hi) + _dot(m_bf16, lo)


def _mlstm(h3, conv_w, conv_b, b_i, b_f, gn_w):
    b, s, _ = h3.shape
    avg = _head_avg_matrix()
    bif = jnp.zeros((1, LANES), F32).at[0, 0:N_RHEADS].set(b_i).at[0, N_RHEADS:2 * N_RHEADS].set(b_f)
    const = lambda shape: pl.BlockSpec(shape, lambda i: (0,) * len(shape))
    return pl.pallas_call(
        _mlstm_kernel,
        grid=(b,),
        in_specs=[pl.BlockSpec((None, s, 4 * REC_W), lambda i: (i, 0, COL_MLSTM // (4 * REC_W))),
                  pl.BlockSpec((None, s, LANES), lambda i: (i, 0, COL_MISC // LANES)),
                  const((MLSTM_CONV, 2 * REC_W)), const((1, 2 * REC_W)), const((1, LANES)),
                  const((REC_W, REC_W)), const((1, REC_W))],
        out_specs=pl.BlockSpec((None, s, REC_W), lambda i: (i, 0, 0)),
        out_shape=jax.ShapeDtypeStruct((b, s, REC_W), F32),
        scratch_shapes=[pltpu.VMEM((REC_W, REC_W), F32),
                        pltpu.VMEM((REC_W, LANES), F32), pltpu.VMEM((1, LANES), F32)],
        compiler_params=_cparams("parallel"),
        name="mlstm",
    )(h3, h3, conv_w, conv_b.reshape(1, -1), bif, avg, gn_w.reshape(1, REC_W))


def _mla_prep_kernel(cq_ref, ckv_ref, cos_ref, sin_ref, qnw_ref, kvnw_ref, wuq_ref, wuk_ref, wuvt_ref,
                     q_ref, k_ref, vt_ref):
    scale = (MLA_NOPE + MLA_ROPE) ** -0.5 * LOG2_E
    cos = cos_ref[...]
    sin = sin_ref[...]
    cos_all = jnp.concatenate([cos] * MLA_HEADS, axis=1)
    sin_all = jnp.concatenate([sin] * MLA_HEADS, axis=1)

    cq = cq_ref[...]
    cqn = cq * lax.rsqrt(jnp.mean(cq * cq, axis=-1, keepdims=True) + LN_EPS) * qnw_ref[...]
    q = _dot(cqn.astype(BF16), wuq_ref[...])
    q = q * cos_all + _swap_halves(q, MLA_ROPE // 2) * sin_all
    q_ref[...] = (q * scale).astype(BF16)

    ckv = ckv_ref[:, 0:MLA_KV_RANK]
    kvn = ckv * lax.rsqrt(jnp.mean(ckv * ckv, axis=-1, keepdims=True) + LN_EPS) * kvnw_ref[...]
    kvb = kvn.astype(BF16)
    misc = ckv_ref[:, MLA_KV_RANK:MLA_KV_RANK + LANES]
    lane = lax.broadcasted_iota(jnp.int32, (1, LANES), 1)
    kr = misc * cos + _swap_halves(misc, MLA_ROPE // 2) * sin
    kr = jnp.where((lane >= MISC_KR) & (lane < MISC_KR + MLA_ROPE), kr, 0.0)
    k = _dot(kvb, wuk_ref[...]) + jnp.concatenate([kr] * MLA_HEADS, axis=1)
    k_ref[...] = k.astype(BF16)
    vt = _dot_nt(wuvt_ref[...], kvb).astype(BF16)
    ones = jnp.ones((ATT_VROWS - MLA_V, ATT_TILE), BF16)
    for j in range(vt_ref.shape[0]):
        for h in range(MLA_HEADS):
            vt_ref[j, h * ATT_VROWS:h * ATT_VROWS + MLA_V, :] = vt[h * MLA_V:(h + 1) * MLA_V,
                                                                   j * ATT_TILE:(j + 1) * ATT_TILE]
            vt_ref[j, h * ATT_VROWS + MLA_V:(h + 1) * ATT_VROWS, :] = ones


def _mla_prep(h2, seq, cos_m, sin_m, qn_w, kvn_w, wuq, wuk, wuvt, tm=512):
    t = h2.shape[0]
    nblk = seq // tm
    hw = MLA_HEADS * LANES
    const = lambda shape: pl.BlockSpec(shape, lambda i: (0,) * len(shape))
    return pl.pallas_call(
        _mla_prep_kernel,
        grid=(t // tm,),
        in_specs=[pl.BlockSpec((tm, MLA_Q_RANK), lambda i: (i, COL_CQ // MLA_Q_RANK)),
                  pl.BlockSpec((tm, 2 * LANES), lambda i: (i, COL_CKV // (2 * LANES))),
                  pl.BlockSpec((tm, LANES), lambda i: (i % nblk, 0)),
                  pl.BlockSpec((tm, LANES), lambda i: (i % nblk, 0)),
                  const((1, MLA_Q_RANK)), const((1, MLA_KV_RANK)),
                  const((MLA_Q_RANK, hw)), const((MLA_KV_RANK, hw)), const((MLA_W, MLA_KV_RANK))],
        out_specs=[pl.BlockSpec((tm, hw), lambda i: (i, 0)), pl.BlockSpec((tm, hw), lambda i: (i, 0)),
                   pl.BlockSpec((tm // ATT_TILE, MLA_HEADS * ATT_VROWS, ATT_TILE), lambda i: (i, 0, 0))],
        out_shape=[jax.ShapeDtypeStruct((t, hw), BF16), jax.ShapeDtypeStruct((t, hw), BF16),
                   jax.ShapeDtypeStruct((t // ATT_TILE, MLA_HEADS * ATT_VROWS, ATT_TILE), BF16)],
        compiler_params=_cparams("parallel"),
        name="mla_prep",
    )(h2, h2, cos_m, sin_m, qn_w.reshape(1, -1), kvn_w.reshape(1, -1), wuq, wuk, wuvt)


def _mla_attn_kernel(q_ref, k_ref, vt_ref, o_ref):
    tq = ATT_TILE
    qi = pl.program_id(1)
    key = lax.broadcasted_iota(jnp.int32, (tq, tq), 0)
    qry = lax.broadcasted_iota(jnp.int32, (tq, tq), 1)
    diag_ok = key < (qry // MASK_CHUNK + 1) * MASK_CHUNK

    def step(kt, carry, masked):
        rows = pl.ds(pl.multiple_of(kt * tq, tq), tq)
        sts = []
        for j in range(MLA_HEADS):
            cols = slice(j * LANES, (j + 1) * LANES)
            sts.append(_dot_nt(k_ref[rows, cols], q_ref[:, cols]))
        new = []
        for g0 in range(0, MLA_HEADS, ATT_GROUP):
            pts = []
            for j in range(g0, g0 + ATT_GROUP):
                m = carry[j][0]
                st = sts[j]
                if masked:
                    st = jnp.where(diag_ok, st, NEG)
                m_new = jnp.maximum(m, jnp.max(st, axis=0, keepdims=True))
                pts.append((m_new, jnp.exp2(m - m_new), jnp.exp2(st - m_new).astype(BF16)))
            for j, (m_new, alpha, pt) in zip(range(g0, g0 + ATT_GROUP), pts):
                acc = alpha * carry[j][1] + _dot(vt_ref[kt, j * ATT_VROWS:(j + 1) * ATT_VROWS, :], pt)
                new.append((m_new, acc))
        return tuple(new)

    init = tuple((jnp.full((1, tq), NEG, F32), jnp.zeros((ATT_VROWS, tq), F32)) for _ in range(MLA_HEADS))
    carry = lax.fori_loop(0, qi, lambda kt, c: step(kt, c, False), init)
    carry = step(qi, carry, True)
    outs = [acc[0:MLA_V, :] / acc[MLA_V:MLA_V + 1, :] for _, acc in carry]
    for p in range(MLA_HEADS // 2):
        o_ref[:, p * LANES:(p + 1) * LANES] = jnp.concatenate([outs[2 * p], outs[2 * p + 1]], axis=0).T


def _mla_attn(q, k, vt, batch, seq):
    t = q.shape[0]
    nq = seq // ATT_TILE
    hw = MLA_HEADS * LANES
    return pl.pallas_call(
        _mla_attn_kernel,
        grid=(batch, nq),
        in_specs=[pl.BlockSpec((ATT_TILE, hw), lambda b, i: (b * nq + i, 0)),
                  pl.BlockSpec((seq, hw), lambda b, i: (b, 0)),
                  pl.BlockSpec((nq, MLA_HEADS * ATT_VROWS, ATT_TILE), lambda b, i: (b, 0, 0))],
        out_specs=pl.BlockSpec((ATT_TILE, MLA_W), lambda b, i: (b * nq + i, 0)),
        out_shape=jax.ShapeDtypeStruct((t, MLA_W), F32),
        compiler_params=_cparams("parallel", "arbitrary"),
        name="mla_attn",
    )(q, k, vt)


def _outproj_kernel(ret_ref, att_ref, ml_ref, x_ref, w_ref, lnw_ref, lnb_ref, o_ref):
    mix = (_dot(ret_ref[...].astype(BF16), w_ref[0:REC_W, :])
           + _dot(att_ref[...].astype(BF16), w_ref[REC_W:REC_W + MLA_W, :])
           + _dot(ml_ref[...].astype(BF16), w_ref[REC_W + MLA_W:, :]))
    o_ref[...] = _layer_norm(DN_ALPHA * x_ref[...] + mix, lnw_ref[...], lnb_ref[...])


def _outproj_ln(ret, att, ml, x2, w_bf16, ln_w, ln_b, tm=512):
    t = x2.shape[0]
    const = lambda shape: pl.BlockSpec(shape, lambda i: (0,) * len(shape))
    row = lambda w: pl.BlockSpec((tm, w), lambda i: (i, 0))
    return pl.pallas_call(
        _outproj_kernel,
        grid=(t // tm,),
        in_specs=[row(REC_W), row(MLA_W), row(REC_W), row(D_MODEL),
                  const((D_MODEL, D_MODEL)), const((1, D_MODEL)), const((1, D_MODEL))],
        out_specs=row(D_MODEL),
        out_shape=jax.ShapeDtypeStruct((t, D_MODEL), F32),
        compiler_params=_cparams("parallel"),
        name="outproj_ln",
    )(ret, att, ml, x2, w_bf16, ln_w.reshape(1, -1), ln_b.reshape(1, -1))


MXU_COLS = 256


def _swiglu_chunk(xb, wg_ref, wu_ref, wd_ref):
    tf = wg_ref.shape[1]
    parts = [slice(s, min(s + MXU_COLS, tf)) for s in range(0, tf, MXU_COLS)]
    gu = [(_dot(xb, wg_ref[:, p].astype(BF16)), _dot(xb, wu_ref[:, p].astype(BF16))) for p in parts]
    hs = [(_silu(g) * u).astype(BF16) for g, u in gu]
    out = _dot(hs[0], wd_ref[parts[0], :].astype(BF16))
    for h, p in zip(hs[1:], parts[1:]):
        out = out + _dot(h, wd_ref[p, :].astype(BF16))
    return out


def _ffn_kernel(x_ref, wg_ref, wu_ref, wd_ref, lnw_ref, lnb_ref, o_ref, xb_ref, acc_ref):
    f = pl.program_id(1)

    @pl.when(f == 0)
    def _():
        xb_ref[...] = x_ref[...].astype(BF16)
        acc_ref[...] = jnp.zeros_like(acc_ref)

    acc_ref[...] += _swiglu_chunk(xb_ref[...], wg_ref, wu_ref, wd_ref)

    @pl.when(f == pl.num_programs(1) - 1)
    def _():
        o_ref[...] = _layer_norm(DN_ALPHA * x_ref[...] + acc_ref[...], lnw_ref[...], lnb_ref[...])


def _ffn_ln(x2, wg, wu, wd, ln_w, ln_b, tm=1024, tf=1408):
    t = x2.shape[0]
    dff = wg.shape[1]
    const = lambda shape: pl.BlockSpec(shape, lambda i, f: (0,) * len(shape))
    return pl.pallas_call(
        _ffn_kernel,
        grid=(t // tm, dff // tf),
        in_specs=[pl.BlockSpec((tm, D_MODEL), lambda i, f: (i, 0)),
                  pl.BlockSpec((D_MODEL, tf), lambda i, f: (0, f)),
                  pl.BlockSpec((D_MODEL, tf), lambda i, f: (0, f)),
                  pl.BlockSpec((tf, D_MODEL), lambda i, f: (f, 0)),
                  const((1, D_MODEL)), const((1, D_MODEL))],
        out_specs=pl.BlockSpec((tm, D_MODEL), lambda i, f: (i, 0)),
        out_shape=jax.ShapeDtypeStruct((t, D_MODEL), F32),
        scratch_shapes=[pltpu.VMEM((tm, D_MODEL), BF16), pltpu.VMEM((tm, D_MODEL), F32)],
        compiler_params=_cparams("parallel", "arbitrary"),
        name="ffn_ln",
    )(x2, wg, wu, wd, ln_w.reshape(1, -1), ln_b.reshape(1, -1))


RT_E1, RT_E2, RT_W1, RT_W2, RT_R1, RT_R2 = 0, 1, 2, 3, 4, 5


def _router_kernel(x_ref, r_ref, route_ref, count_ref, carry_ref):
    @pl.when(pl.program_id(0) == 0)
    def _():
        carry_ref[...] = jnp.zeros_like(carry_ref)

    x = x_ref[...]
    xh = x.astype(BF16)
    xm = (x - xh.astype(F32)).astype(BF16)
    rh, rm = r_ref[0], r_ref[1]
    logits = _dot(xh, rh) + (_dot(xh, rm) + _dot(xm, rh))
    tm = logits.shape[0]
    lane = lax.broadcasted_iota(jnp.int32, logits.shape, 1)
    lg = jnp.where(lane < N_EXPERTS, logits, -jnp.inf)
    m1 = jnp.max(lg, axis=-1, keepdims=True)
    i1 = jnp.min(jnp.where(lg == m1, lane, LANES), axis=-1, keepdims=True)
    lg2 = jnp.where(lane == i1, -jnp.inf, lg)
    m2 = jnp.max(lg2, axis=-1, keepdims=True)
    i2 = jnp.min(jnp.where(lg2 == m2, lane, LANES), axis=-1, keepdims=True)
    e2 = jnp.exp(m2 - m1)
    w1 = 1.0 / (1.0 + e2)
    w2 = e2 / (1.0 + e2)

    sel = ((lane == i1) | (lane == i2)).astype(F32)
    before = (lax.broadcasted_iota(jnp.int32, (tm, tm), 0) > lax.broadcasted_iota(jnp.int32, (tm, tm), 1))
    ranks = _dot(before.astype(BF16), sel.astype(BF16)) + carry_ref[...]
    r1 = jnp.sum(jnp.where(lane == i1, ranks, 0.0), axis=-1, keepdims=True)
    r2 = jnp.sum(jnp.where(lane == i2, ranks, 0.0), axis=-1, keepdims=True)
    carry_ref[...] += jnp.sum(sel, axis=0, keepdims=True)
    count_ref[...] = carry_ref[...]

    rec = jnp.zeros(logits.shape, F32)
    for pos, val in ((RT_E1, i1.astype(F32)), (RT_E2, i2.astype(F32)), (RT_W1, w1), (RT_W2, w2),
                     (RT_R1, r1), (RT_R2, r2)):
        rec = jnp.where(lane == pos, val, rec)
    route_ref[...] = rec


def _router(x2, router, tm=1024):
    t = x2.shape[0]
    rp = jnp.zeros((D_MODEL, LANES), F32).at[:, 0:N_EXPERTS].set(router)
    rh = rp.astype(BF16)
    rm = (rp - rh.astype(F32)).astype(BF16)
    r3 = jnp.stack([rh, rm])
    return pl.pallas_call(
        _router_kernel,
        grid=(t // tm,),
        in_specs=[pl.BlockSpec((tm, D_MODEL), lambda i: (i, 0)),
                  pl.BlockSpec((2, D_MODEL, LANES), lambda i: (0, 0, 0))],
        out_specs=[pl.BlockSpec((tm, LANES), lambda i: (i, 0)),
                   pl.BlockSpec((1, LANES), lambda i: (0, 0))],
        out_shape=[jax.ShapeDtypeStruct((t, LANES), F32), jax.ShapeDtypeStruct((1, LANES), F32)],
        scratch_shapes=[pltpu.VMEM((1, LANES), F32)],
        compiler_params=_cparams("arbitrary"),
        name="router",
    )(x2, r3)


MOE_TM = 1024


def _route_plan(route, counts, t):
    cnt = counts[0, :N_EXPERTS].astype(jnp.int32)
    padded = ((cnt + MOE_TM - 1) // MOE_TM) * MOE_TM
    ends = jnp.cumsum(padded)
    offs = ends - padded
    experts = jnp.arange(N_EXPERTS, dtype=jnp.int32)

    def dest(e_lane, r_lane):
        e = route[:, e_lane].astype(jnp.int32)
        off = jnp.sum(jnp.where(e[:, None] == experts[None, :], offs[None, :], 0), axis=1)
        return off + route[:, r_lane].astype(jnp.int32)

    n_tiles = (2 * t) // MOE_TM + N_EXPERTS
    n_used = ends[-1] // MOE_TM
    tile = jnp.minimum(jnp.arange(n_tiles, dtype=jnp.int32), n_used - 1)
    tile_expert = jnp.sum((tile[:, None] * MOE_TM >= ends[None, :]).astype(jnp.int32), axis=1)
    group_end = jnp.sum(jnp.where(tile_expert[:, None] == experts[None, :], (offs + cnt)[None, :], 0), axis=1)
    tile_rows = jnp.clip(group_end - tile * MOE_TM, 0, MOE_TM)
    return dest(RT_E1, RT_R1), dest(RT_E2, RT_R2), tile_expert, n_used.reshape(1), tile_rows, n_tiles


def _dispatch_kernel(d1_ref, d2_ref, x_ref, zero_ref, xs_ref, sem):
    del zero_ref
    tm = x_ref.shape[0]
    base = pl.program_id(0) * tm

    def copies(r):
        src = x_ref.at[pl.ds(r, 1), :]
        return (pltpu.make_async_copy(src, xs_ref.at[pl.ds(d1_ref[base + r], 1), :], sem),
                pltpu.make_async_copy(src, xs_ref.at[pl.ds(d2_ref[base + r], 1), :], sem))

    def start(r, c):
        for cp in copies(r):
            cp.start()
        return c

    def wait(r, c):
        for cp in copies(r):
            cp.wait()
        return c

    lax.fori_loop(0, tm, start, 0, unroll=8)
    lax.fori_loop(0, tm, wait, 0, unroll=8)


def _dispatch(x2, d1, d2, n_rows, tm=512):
    t = x2.shape[0]
    zeros = jnp.zeros((n_rows, D_MODEL), F32)
    return pl.pallas_call(
        _dispatch_kernel,
        grid_spec=pltpu.PrefetchScalarGridSpec(
            num_scalar_prefetch=2,
            grid=(t // tm,),
            in_specs=[pl.BlockSpec((tm, D_MODEL), lambda i, d1, d2: (i, 0)),
                      pl.BlockSpec(memory_space=pl.ANY)],
            out_specs=pl.BlockSpec(memory_space=pl.ANY),
            scratch_shapes=[pltpu.SemaphoreType.DMA(())]),
        out_shape=jax.ShapeDtypeStruct((n_rows, D_MODEL), F32),
        input_output_aliases={3: 0},
        compiler_params=_cparams("arbitrary"),
        name="moe_dispatch",
    )(d1, d2, x2, zeros)


def _moe_ffn_kernel(te_ref, nu_ref, nr_ref, x_ref, wg_ref, wu_ref, wd_ref, o_ref, xb_ref, acc_ref):
    del te_ref
    i = pl.program_id(0)
    f = pl.program_id(1)
    used = i < nu_ref[0]
    half = MOE_TM // 2

    def swiglu_rows(rows):
        @pl.when(f == 0)
        def _():
            xb_ref[0:rows, :] = x_ref[0:rows, :].astype(BF16)
            acc_ref[...] = jnp.zeros_like(acc_ref)

        acc_ref[0:rows, :] += _swiglu_chunk(xb_ref[0:rows, :], wg_ref, wu_ref, wd_ref)

    @pl.when(used & (nr_ref[i] > half))
    def _():
        swiglu_rows(MOE_TM)

    @pl.when(used & (nr_ref[i] <= half))
    def _():
        swiglu_rows(half)

    @pl.when(used & (f == pl.num_programs(1) - 1))
    def _():
        o_ref[...] = acc_ref[...]

    @pl.when(jnp.logical_not(used))
    def _():
        o_ref[...] = jnp.zeros_like(o_ref)


def _moe_ffn(xs, tile_expert, n_used, tile_rows, n_tiles, wg, wu, wd, tf=512):
    dff = wg.shape[2]
    nf = dff // tf
    row = lambda i, f, te, nu, nr: (jnp.minimum(i, nu[0] - 1), 0)
    fcl = lambda i, f, nu: jnp.where(i < nu[0], f, nf - 1)
    return pl.pallas_call(
        _moe_ffn_kernel,
        grid_spec=pltpu.PrefetchScalarGridSpec(
            num_scalar_prefetch=3,
            grid=(n_tiles, nf),
            in_specs=[pl.BlockSpec((MOE_TM, D_MODEL), row),
                      pl.BlockSpec((None, D_MODEL, tf), lambda i, f, te, nu, nr: (te[i], 0, fcl(i, f, nu))),
                      pl.BlockSpec((None, D_MODEL, tf), lambda i, f, te, nu, nr: (te[i], 0, fcl(i, f, nu))),
                      pl.BlockSpec((None, tf, D_MODEL), lambda i, f, te, nu, nr: (te[i], fcl(i, f, nu), 0))],
            out_specs=pl.BlockSpec((MOE_TM, D_MODEL), lambda i, f, te, nu, nr: (i, 0)),
            scratch_shapes=[pltpu.VMEM((MOE_TM, D_MODEL), BF16), pltpu.VMEM((MOE_TM, D_MODEL), F32)]),
        out_shape=jax.ShapeDtypeStruct(xs.shape, F32),
        compiler_params=_cparams("arbitrary", "arbitrary"),
        name="moe_ffn",
    )(tile_expert, n_used, tile_rows, xs, wg, wu, wd)


def _combine_kernel(d1_ref, d2_ref, x_ref, route_ref, ys_ref, lnw_ref, lnb_ref, o_ref, b1_ref, b2_ref, sem):
    tm = x_ref.shape[0]
    i = pl.program_id(0)
    n = pl.num_programs(0)

    def copies(step, r):
        slot = step % 2
        base = step * tm
        return (pltpu.make_async_copy(ys_ref.at[pl.ds(d1_ref[base + r], 1), :],
                                      b1_ref.at[slot, pl.ds(r, 1), :], sem.at[slot]),
                pltpu.make_async_copy(ys_ref.at[pl.ds(d2_ref[base + r], 1), :],
                                      b2_ref.at[slot, pl.ds(r, 1), :], sem.at[slot]))

    def start_tile(step):
        def body(r, c):
            for cp in copies(step, r):
                cp.start()
            return c
        lax.fori_loop(0, tm, body, 0, unroll=8)

    def wait_tile(step):
        def body(r, c):
            for cp in copies(step, r):
                cp.wait()
            return c
        lax.fori_loop(0, tm, body, 0, unroll=8)

    @pl.when(i == 0)
    def _():
        start_tile(i)

    @pl.when(i + 1 < n)
    def _():
        start_tile(i + 1)

    wait_tile(i)
    slot = i % 2
    rt = route_ref[...]
    y = rt[:, RT_W1:RT_W1 + 1] * b1_ref[slot] + rt[:, RT_W2:RT_W2 + 1] * b2_ref[slot]
    o_ref[...] = _layer_norm(DN_ALPHA * x_ref[...] + y, lnw_ref[...], lnb_ref[...])


def _combine_ln(x2, route, ys, d1, d2, ln_w, ln_b, tm=256):
    t = x2.shape[0]
    const = lambda shape: pl.BlockSpec(shape, lambda i, d1, d2: (0,) * len(shape))
    return pl.pallas_call(
        _combine_kernel,
        grid_spec=pltpu.PrefetchScalarGridSpec(
            num_scalar_prefetch=2,
            grid=(t // tm,),
            in_specs=[pl.BlockSpec((tm, D_MODEL), lambda i, d1, d2: (i, 0)),
                      pl.BlockSpec((tm, LANES), lambda i, d1, d2: (i, 0)),
                      pl.BlockSpec(memory_space=pl.ANY),
                      const((1, D_MODEL)), const((1, D_MODEL))],
            out_specs=pl.BlockSpec((tm, D_MODEL), lambda i, d1, d2: (i, 0)),
            scratch_shapes=[pltpu.VMEM((2, tm, D_MODEL), F32), pltpu.VMEM((2, tm, D_MODEL), F32),
                            pltpu.SemaphoreType.DMA((2,))]),
        out_shape=jax.ShapeDtypeStruct((t, D_MODEL), F32),
        compiler_params=_cparams("arbitrary"),
        name="moe_combine_ln",
    )(d1, d2, x2, route, ys, ln_w.reshape(1, -1), ln_b.reshape(1, -1))


def _moe_ln(x2, router, wg, wu, wd, ln_w, ln_b):
    t = x2.shape[0]
    route, counts = _router(x2, router)
    d1, d2, tile_expert, n_used, tile_rows, n_tiles = _route_plan(route, counts, t)
    xs = _dispatch(x2, d1, d2, n_tiles * MOE_TM)
    ys = _moe_ffn(xs, tile_expert, n_used, tile_rows, n_tiles, wg, wu, wd)
    return _combine_ln(x2, route, ys, d1, d2, ln_w, ln_b)


def _layout_w_in(w):
    sizes = (REC_W, REC_W, REC_W, REC_W, MLA_Q_RANK, MLA_KV_RANK, MLA_ROPE,
             REC_W, REC_W, REC_W, REC_W, N_RHEADS, N_RHEADS)
    offs = np.concatenate([[0], np.cumsum(sizes)])
    part = lambda j: w[:, offs[j]:offs[j + 1]]
    z = lambda n: jnp.zeros((w.shape[0], n), w.dtype)
    misc = jnp.concatenate([part(11), part(12), z(MISC_KR - 2 * N_RHEADS), part(6),
                            z(LANES - MISC_KR - MLA_ROPE)], axis=1)
    cols = [part(0), part(1), part(2), part(3), part(7), part(8), part(9), part(10), part(4), part(5), misc]
    return jnp.concatenate(cols, axis=1).astype(BF16)


def _layout_mla(w_uq, w_ukv):
    r = w_uq.shape[0]
    uq = w_uq.reshape(r, MLA_HEADS, MLA_NOPE + MLA_ROPE)
    uq = jnp.pad(uq, ((0, 0), (0, 0), (0, LANES - MLA_NOPE - MLA_ROPE))).reshape(r, MLA_HEADS * LANES)
    r = w_ukv.shape[0]
    ukv = w_ukv.reshape(r, MLA_HEADS, MLA_NOPE + MLA_V)
    uk = jnp.pad(ukv[:, :, :MLA_NOPE], ((0, 0), (0, 0), (0, LANES - MLA_NOPE))).reshape(r, MLA_HEADS * LANES)
    uvt = ukv[:, :, MLA_NOPE:].reshape(r, MLA_W).T
    return uq.astype(BF16), uk.astype(BF16), uvt.astype(BF16)


def kernel(x, w_in, ret_gn_w, mla_q_norm_w, mla_w_uq, mla_kv_norm_w, mla_w_ukv, mlstm_conv_w, mlstm_conv_b,
           mlstm_b_i, mlstm_b_f, mlstm_gn_w, w_out, ln1_w, ln1_b, ffn_w_gate, ffn_w_up, ffn_w_down,
           moe_router, moe_w_gate, moe_w_up, moe_w_down, ln2_w, ln2_b):
    bsz, seq, d = x.shape
    t = bsz * seq
    cos_r, sin_r, cos_m, sin_m = _rope_tables(seq)
    x2 = x.reshape(t, d)
    for l in range(DEPTH):
        h2 = _inproj(x2, _layout_w_in(w_in[l]))
        h3 = h2.reshape(bsz, seq, D_IN_PAD)
        ret = _retention(h3, cos_r, sin_r, ret_gn_w[l]).reshape(t, REC_W)
        ml = _mlstm(h3, mlstm_conv_w[l], mlstm_conv_b[l], mlstm_b_i[l], mlstm_b_f[l],
                    mlstm_gn_w[l]).reshape(t, REC_W)
        wuq, wuk, wuvt = _layout_mla(mla_w_uq[l], mla_w_ukv[l])
        q, k, vt = _mla_prep(h2, seq, cos_m, sin_m, mla_q_norm_w[l], mla_kv_norm_w[l], wuq, wuk, wuvt)
        att = _mla_attn(q, k, vt, bsz, seq)
        x2 = _outproj_ln(ret, att, ml, x2, w_out[l].astype(BF16), ln1_w[l], ln1_b[l])
        if l % 2 == 0:
            j = l // 2
            x2 = _ffn_ln(x2, ffn_w_gate[j].astype(BF16), ffn_w_up[j].astype(BF16),
                         ffn_w_down[j].astype(BF16), ln2_w[l], ln2_b[l])
        else:
            j = l // 2
            x2 = _moe_ln(x2, moe_router[j], moe_w_gate[j], moe_w_up[j], moe_w_down[j], ln2_w[l], ln2_b[l])
    return x2.reshape(bsz, seq, d)
```

```python
import functools

import numpy as np
import jax
import jax.numpy as jnp
from jax import lax
from jax.experimental import pallas as pl
from jax.experimental.pallas import tpu as pltpu

F32 = jnp.float32
BF16 = jnp.bfloat16

D_MODEL = 1024
DEPTH = 2
ROPE_BASE = 10000.0
LN_EPS = 1e-5
NEG = -1e30

HEAD_DIM = 64
N_RHEADS = 4
REC_W = N_RHEADS * HEAD_DIM
MLA_HEADS = 8
MLA_NOPE = 64
MLA_ROPE = 32
MLA_V = 64
MLA_Q_RANK = 256
MLA_KV_RANK = 128
MLA_W = MLA_HEADS * MLA_V
MLSTM_CONV = 4
MASK_CHUNK = 64
REC_CHUNK = 256
REC_SEQS = 2
MLSTM_SEQS = 1
ATT_TILE = 256
ATT_GROUP = 4
ATT_VROWS = 80
N_EXPERTS = 8
LANES = 128

DN_ALPHA = (2 * DEPTH) ** 0.25
LOG2_E = 1.4426950408889634

COL_RET = 0
COL_MLSTM = 1024
COL_CQ = 2048
COL_CKV = 2304
COL_MISC = 2432
D_IN_PAD = 2560
MISC_KR = 64

VMEM_LIMIT = 56 * 1024 * 1024


def _cparams(*sem):
    return pltpu.CompilerParams(dimension_semantics=sem, vmem_limit_bytes=VMEM_LIMIT)


def _layer_norm(y, w, b):
    mu = jnp.mean(y, axis=-1, keepdims=True)
    d = y - mu
    var = jnp.mean(d * d, axis=-1, keepdims=True)
    return d * lax.rsqrt(var + LN_EPS) * w + b


def _silu(x):
    return x * (1.0 / (1.0 + jnp.exp(-x)))


def _sigmoid(x):
    return 1.0 / (1.0 + jnp.exp(-x))


def _split2(x):
    hi = x.astype(BF16)
    lo = (x - hi.astype(F32)).astype(BF16)
    return hi, lo


def _dot(a, b):
    return jnp.dot(a, b, preferred_element_type=F32)


def _dot_nt(a, b):
    return lax.dot_general(a, b, (((1,), (1,)), ((), ())), preferred_element_type=F32)


def _dot_exact_rhs(x, m_bf16):
    hi, lo = _split2(x)
    return _dot(hi, m_bf16) + _dot(lo, m_bf16)


def _swap_halves(x, half):
    n = x.shape[-1]
    lane = lax.broadcasted_iota(jnp.int32, x.shape, x.ndim - 1)
    first = (lane % (2 * half)) < half
    return jnp.where(first, pltpu.roll(x, n - half, x.ndim - 1), pltpu.roll(x, half, x.ndim - 1))


def _head_norm(x, avg_bf16, w):
    mu = _dot_exact_rhs(x, avg_bf16)
    d = x - mu
    var = _dot_exact_rhs(d * d, avg_bf16)
    return d * lax.rsqrt(var + LN_EPS) * w


def _expand_heads(v, lane_head):
    out = jnp.zeros((v.shape[0], REC_W), F32)
    for h in range(N_RHEADS):
        out = jnp.where(lane_head == h, v[:, h:h + 1], out)
    return out


def _rope_tables(seq):
    pos = np.arange(seq, dtype=np.float64)[:, None]
    half = HEAD_DIM // 2
    inv = ROPE_BASE ** (-np.arange(half, dtype=np.float64) / half)
    ang = pos * inv[None, :]
    cos_r = np.tile(np.concatenate([np.cos(ang), np.cos(ang)], -1), (1, N_RHEADS))
    sin_r = np.tile(np.concatenate([-np.sin(ang), np.sin(ang)], -1), (1, N_RHEADS))
    half = MLA_ROPE // 2
    inv = ROPE_BASE ** (-np.arange(half, dtype=np.float64) / half)
    ang = pos * inv[None, :]
    cos_m = np.ones((seq, LANES))
    sin_m = np.zeros((seq, LANES))
    cos_m[:, MISC_KR:MISC_KR + MLA_ROPE] = np.concatenate([np.cos(ang), np.cos(ang)], -1)
    sin_m[:, MISC_KR:MISC_KR + MLA_ROPE] = np.concatenate([-np.sin(ang), np.sin(ang)], -1)
    f = lambda a: jnp.asarray(a.astype(np.float32))
    return f(cos_r), f(sin_r), f(cos_m), f(sin_m)


def _retention_tables():
    L = REC_CHUNK
    log_gamma = np.log(1.0 - 2.0 ** (-5.0 - np.arange(N_RHEADS, dtype=np.float64)))
    idx = np.arange(L, dtype=np.float64)
    diff = idx[:, None] - idx[None, :]
    dmask = np.where(diff >= 0, np.exp(diff[None] * log_gamma[:, None, None]), 0.0)
    lane_lg = np.repeat(log_gamma, HEAD_DIM)[None, :]
    qw = np.exp((idx + 1.0)[:, None] * lane_lg)
    kw = np.exp((L - 1 - idx)[:, None] * lane_lg)
    cd = np.exp(L * lane_lg)
    f = lambda a: jnp.asarray(a.astype(np.float32))
    return f(dmask), f(qw), f(kw), f(cd)


def _head_avg_matrix():
    h = np.arange(REC_W) // HEAD_DIM
    return jnp.asarray((h[:, None] == h[None, :]).astype(np.float32) / HEAD_DIM, dtype=BF16)


def _inproj_kernel(x_ref, w_ref, o_ref):
    o_ref[...] = _dot(x_ref[...].astype(BF16), w_ref[...])


def _inproj(x2, w_bf16, tm=512):
    t, k = x2.shape
    n = w_bf16.shape[1]
    return pl.pallas_call(
        _inproj_kernel,
        grid=(t // tm,),
        in_specs=[pl.BlockSpec((tm, k), lambda i: (i, 0)),
                  pl.BlockSpec((k, n), lambda i: (0, 0))],
        out_specs=pl.BlockSpec((tm, n), lambda i: (i, 0)),
        out_shape=jax.ShapeDtypeStruct((t, n), F32),
        compiler_params=_cparams("parallel"),
        name="inproj",
    )(x2, w_bf16)


def _retention_kernel(h_ref, cos_ref, sin_ref, dmask_ref, qw_ref, kw_ref, cd_ref, avg_ref, gnw_ref,
                      o_ref, state_ref):
    L = REC_CHUNK
    nb = h_ref.shape[0]
    nc = h_ref.shape[1] // L
    seqs = range(nb)
    heads = range(N_RHEADS)
    lane_head = lax.broadcasted_iota(jnp.int32, (1, REC_W), 1) // HEAD_DIM
    row_head = lax.broadcasted_iota(jnp.int32, (REC_W, REC_W), 0) // HEAD_DIM
    col_head = lax.broadcasted_iota(jnp.int32, (REC_W, REC_W), 1) // HEAD_DIM
    block_diag = row_head == col_head
    state_ref[...] = jnp.zeros_like(state_ref)

    def chunk(c, carry):
        r0 = pl.multiple_of(c * L, L)
        rows = pl.ds(r0, L)
        cos = cos_ref[rows, :]
        sin = sin_ref[rows, :]
        qs, ks, kbs, vbs = [], [], [], []
        for s in seqs:
            q = h_ref[s, rows, 0:REC_W]
            k = h_ref[s, rows, REC_W:2 * REC_W]
            qs.append(q * cos + _swap_halves(q, HEAD_DIM // 2) * sin)
            k = (k * cos + _swap_halves(k, HEAD_DIM // 2) * sin) * (HEAD_DIM ** -0.5)
            ks.append(k)
            kbs.append(k.astype(BF16))
            vbs.append(h_ref[s, rows, 2 * REC_W:3 * REC_W].astype(BF16))
        scores = [[_dot_nt(jnp.where(lane_head == h, qs[s], 0.0).astype(BF16), kbs[s]) for h in heads]
                  for s in seqs]
        probs = [[(scores[s][h] * dmask_ref[h]).astype(BF16) for h in heads] for s in seqs]
        rets = []
        for s in seqs:
            inner = jnp.zeros((L, REC_W), F32)
            for h in heads:
                inner = jnp.where(lane_head == h, _dot(probs[s][h], vbs[s]), inner)
            state = state_ref[s]
            cross = _dot(qs[s].astype(BF16), state.astype(BF16)) * qw_ref[...]
            rets.append(inner + cross)
            kwt = (ks[s] * kw_ref[...]).T.astype(BF16)
            loc = _dot(kwt, vbs[s])
            state_ref[s] = cd_ref[...] * state + jnp.where(block_diag, loc, 0.0)
        for s in seqs:
            g = h_ref[s, rows, 3 * REC_W:4 * REC_W]
            o_ref[s, rows, :] = _silu(g) * _head_norm(rets[s], avg_ref[...], gnw_ref[...])
        return carry

    lax.fori_loop(0, nc, chunk, 0)


def _retention(h3, cos_r, sin_r, gn_w):
    b, s, _ = h3.shape
    dmask, qw, kw, cd = _retention_tables()
    avg = _head_avg_matrix()
    const = lambda shape: pl.BlockSpec(shape, lambda i: (0,) * len(shape))
    return pl.pallas_call(
        _retention_kernel,
        grid=(b // REC_SEQS,),
        in_specs=[pl.BlockSpec((REC_SEQS, s, 4 * REC_W), lambda i: (i, 0, COL_RET // (4 * REC_W))),
                  const((s, REC_W)), const((s, REC_W)),
                  const((N_RHEADS, REC_CHUNK, REC_CHUNK)), const((REC_CHUNK, REC_W)),
                  const((REC_CHUNK, REC_W)), const((1, REC_W)), const((REC_W, REC_W)), const((1, REC_W))],
        out_specs=pl.BlockSpec((REC_SEQS, s, REC_W), lambda i: (i, 0, 0)),
        out_shape=jax.ShapeDtypeStruct((b, s, REC_W), F32),
        scratch_shapes=[pltpu.VMEM((REC_SEQS, REC_W, REC_W), F32)],
        compiler_params=_cparams("parallel"),
        name="retention",
    )(h3, cos_r, sin_r, dmask, qw, kw, cd, avg, gn_w.reshape(1, REC_W))


def _mlstm_kernel(h_ref, misc_ref, convw_ref, convb_ref, bif_ref, avg_ref, gnw_ref,
                  o_ref, c_ref, n_ref, m_ref):
    L = REC_CHUNK
    nb = h_ref.shape[0]
    nc = h_ref.shape[1] // L
    seqs = range(nb)
    heads = range(N_RHEADS)
    lane_head = lax.broadcasted_iota(jnp.int32, (1, REC_W), 1) // HEAD_DIM
    row_head = lax.broadcasted_iota(jnp.int32, (REC_W, REC_W), 0) // HEAD_DIM
    col_head = lax.broadcasted_iota(jnp.int32, (REC_W, REC_W), 1) // HEAD_DIM
    block_diag = row_head == col_head
    n_mask = (lax.broadcasted_iota(jnp.int32, (REC_W, LANES), 0) // HEAD_DIM
              == lax.broadcasted_iota(jnp.int32, (REC_W, LANES), 1))
    lane128 = lax.broadcasted_iota(jnp.int32, (1, LANES), 1)
    gate_lanes = lane128 < N_RHEADS
    ri = lax.broadcasted_iota(jnp.int32, (L, L), 0)
    ci = lax.broadcasted_iota(jnp.int32, (L, L), 1)
    causal = ri >= ci
    tri = causal.astype(BF16)
    row8 = lax.broadcasted_iota(jnp.int32, (8, 2 * REC_W), 0)

    c_ref[...] = jnp.zeros_like(c_ref)
    n_ref[...] = jnp.zeros_like(n_ref)
    m_ref[...] = jnp.full(m_ref.shape, NEG, F32)

    def prepare(s, c, r0, rows):
        x = h_ref[s, rows, 0:2 * REC_W]
        prev = h_ref[s, pl.ds(pl.multiple_of(jnp.maximum(r0 - 8, 0), 8), 8), 0:2 * REC_W]
        prev = jnp.where(c > 0, prev, 0.0)
        acc = x * convw_ref[MLSTM_CONV - 1:MLSTM_CONV, :] + convb_ref[...]
        for sft in range(1, MLSTM_CONV):
            xs = pltpu.roll(x, sft, 0)
            head = jnp.where(row8 < sft, pltpu.roll(prev, sft, 0), xs[0:8, :])
            xs = jnp.concatenate([head, xs[8:, :]], axis=0)
            acc = acc + xs * convw_ref[MLSTM_CONV - 1 - sft:MLSTM_CONV - sft, :]
        qk = _silu(acc)
        q = qk[:, 0:REC_W]
        k = qk[:, REC_W:2 * REC_W] * (HEAD_DIM ** -0.5)

        gates = misc_ref[s, rows, :] + bif_ref[...]
        ic_c = jnp.where(gate_lanes, gates, 0.0)
        fg = pltpu.roll(gates, LANES - N_RHEADS, 1)
        lf_c = jnp.where(gate_lanes, jnp.minimum(fg, 0.0) - jnp.log1p(jnp.exp(-jnp.abs(fg))), 0.0)

        bcum = _dot_exact_rhs_left(tri, lf_c)
        gsum = bcum[L - 1:L, :]
        m_s = m_ref[s]
        a = gsum - bcum + ic_c
        qb = q.astype(BF16)
        c_state = c_ref[s]
        n_state = n_ref[s]
        return dict(q=q, k=k, qb=qb, kb=k.astype(BF16), vb=h_ref[s, rows, 2 * REC_W:3 * REC_W].astype(BF16),
                    bcum=bcum, gsum=gsum, m_s=m_s, a=a, amax=jnp.max(a, axis=0, keepdims=True),
                    rvec=(ic_c - bcum).T,
                    inter=bcum + m_s,
                    c_state=c_state, n_state=n_state,
                    q_c=_dot(qb, c_state.astype(BF16)),
                    q_n=_dot(qb, n_state.astype(BF16)))

    def decay(d, scores):
        cmats, stats = [], []
        for h in heads:
            log_d = jnp.where(causal, d["bcum"][:, h:h + 1] + d["rvec"][h:h + 1, :], NEG)
            inter_h = d["inter"][:, h:h + 1]
            m_t = jnp.maximum(inter_h, jnp.max(log_d, axis=-1, keepdims=True))
            cmat = scores[h] * jnp.exp(log_d - m_t)
            si = jnp.exp(inter_h - m_t)
            den = jnp.sum(cmat, axis=-1, keepdims=True) + si * d["q_n"][:, h:h + 1]
            cmats.append(cmat.astype(BF16))
            stats.append((si, jnp.maximum(jnp.abs(den), jnp.exp(-m_t))))
        return cmats, stats

    def cell(d, cmats, stats):
        hcell = jnp.zeros((L, REC_W), F32)
        for h in heads:
            si, denom = stats[h]
            num = _dot(cmats[h], d["vb"]) + si * d["q_c"]
            hcell = jnp.where(lane_head == h, num / denom, hcell)
        return hcell

    def carry_state(s, d):
        wa = jnp.exp(d["a"] - d["amax"])
        m_new = jnp.maximum(d["gsum"] + d["m_s"], d["amax"])
        sp = jnp.exp(d["gsum"] + d["m_s"] - m_new)
        sl = jnp.exp(d["amax"] - m_new)
        kwt = (d["k"] * _expand_heads(wa, lane_head)).T.astype(BF16)
        loc_c = _dot(kwt, d["vb"])
        loc_n = _dot(kwt, jnp.ones((L, LANES), BF16))
        c_ref[s] = (d["c_state"] * _expand_heads(sp, lane_head)
                    + jnp.where(block_diag, loc_c, 0.0) * _expand_heads(sl, lane_head))
        n_ref[s] = d["n_state"] * sp + jnp.where(n_mask, loc_n, 0.0) * sl
        m_ref[s] = m_new

    def chunk(c, carry):
        r0 = pl.multiple_of(c * L, L)
        rows = pl.ds(r0, L)
        ds = [prepare(s, c, r0, rows) for s in seqs]
        scores = [[_dot_nt(jnp.where(lane_head == h, d["q"], 0.0).astype(BF16), d["kb"]) for h in heads]
                  for d in ds]
        decs = [decay(d, sc) for d, sc in zip(ds, scores)]
        cells = [cell(d, *dec) for d, dec in zip(ds, decs)]
        for s in seqs:
            carry_state(s, ds[s])
        for s in seqs:
            og = h_ref[s, rows, 3 * REC_W:4 * REC_W]
            o_ref[s, rows, :] = _head_norm(_sigmoid(og) * cells[s], avg_ref[...], gnw_ref[...])
        return carry

    lax.fori_loop(0, nc, chunk, 0)


def _dot_exact_rhs_left(m_bf16, x):
    hi, lo = _split2(x)
    return _dot(m_bf16, hi) + _dot(m_bf16, lo)


def _mlstm(h3, conv_w, conv_b, b_i, b_f, gn_w):
    b, s, _ = h3.shape
    avg = _head_avg_matrix()
    bif = jnp.zeros((1, LANES), F32).at[0, 0:N_RHEADS].set(b_i).at[0, N_RHEADS:2 * N_RHEADS].set(b_f)
    const = lambda shape: pl.BlockSpec(shape, lambda i: (0,) * len(shape))
    return pl.pallas_call(
        _mlstm_kernel,
        grid=(b // MLSTM_SEQS,),
        in_specs=[pl.BlockSpec((MLSTM_SEQS, s, 4 * REC_W), lambda i: (i, 0, COL_MLSTM // (4 * REC_W))),
                  pl.BlockSpec((MLSTM_SEQS, s, LANES), lambda i: (i, 0, COL_MISC // LANES)),
                  const((MLSTM_CONV, 2 * REC_W)), const((1, 2 * REC_W)), const((1, LANES)),
                  const((REC_W, REC_W)), const((1, REC_W))],
        out_specs=pl.BlockSpec((MLSTM_SEQS, s, REC_W), lambda i: (i, 0, 0)),
        out_shape=jax.ShapeDtypeStruct((b, s, REC_W), F32),
        scratch_shapes=[pltpu.VMEM((MLSTM_SEQS, REC_W, REC_W), F32),
                        pltpu.VMEM((MLSTM_SEQS, REC_W, LANES), F32), pltpu.VMEM((MLSTM_SEQS, 1, LANES), F32)],
        compiler_params=_cparams("parallel"),
        name="mlstm",
    )(h3, h3, conv_w, conv_b.reshape(1, -1), bif, avg, gn_w.reshape(1, REC_W))


def _mla_prep_kernel(cq_ref, ckv_ref, cos_ref, sin_ref, qnw_ref, kvnw_ref, wuq_ref, wuk_ref, wuvt_ref,
                     q_ref, k_ref, vt_ref):
    scale = (MLA_NOPE + MLA_ROPE) ** -0.5 * LOG2_E
    cos = cos_ref[...]
    sin = sin_ref[...]
    cos_all = jnp.concatenate([cos] * MLA_HEADS, axis=1)
    sin_all = jnp.concatenate([sin] * MLA_HEADS, axis=1)

    cq = cq_ref[...]
    cqn = cq * lax.rsqrt(jnp.mean(cq * cq, axis=-1, keepdims=True) + LN_EPS) * qnw_ref[...]
    q = _dot(cqn.astype(BF16), wuq_ref[...])
    q = q * cos_all + _swap_halves(q, MLA_ROPE // 2) * sin_all
    q_ref[...] = (q * scale).astype(BF16)

    ckv = ckv_ref[:, 0:MLA_KV_RANK]
    kvn = ckv * lax.rsqrt(jnp.mean(ckv * ckv, axis=-1, keepdims=True) + LN_EPS) * kvnw_ref[...]
    kvb = kvn.astype(BF16)
    misc = ckv_ref[:, MLA_KV_RANK:MLA_KV_RANK + LANES]
    lane = lax.broadcasted_iota(jnp.int32, (1, LANES), 1)
    kr = misc * cos + _swap_halves(misc, MLA_ROPE // 2) * sin
    kr = jnp.where((lane >= MISC_KR) & (lane < MISC_KR + MLA_ROPE), kr, 0.0)
    k = _dot(kvb, wuk_ref[...]) + jnp.concatenate([kr] * MLA_HEADS, axis=1)
    k_ref[...] = k.astype(BF16)
    vt = _dot_nt(wuvt_ref[...], kvb).astype(BF16)
    ones = jnp.ones((ATT_VROWS - MLA_V, ATT_TILE), BF16)
    for j in range(vt_ref.shape[0]):
        for h in range(MLA_HEADS):
            vt_ref[j, h * ATT_VROWS:h * ATT_VROWS + MLA_V, :] = vt[h * MLA_V:(h + 1) * MLA_V,
                                                                   j * ATT_TILE:(j + 1) * ATT_TILE]
            vt_ref[j, h * ATT_VROWS + MLA_V:(h + 1) * ATT_VROWS, :] = ones


def _mla_prep(h2, seq, cos_m, sin_m, qn_w, kvn_w, wuq, wuk, wuvt, tm=512):
    t = h2.shape[0]
    nblk = seq // tm
    hw = MLA_HEADS * LANES
    const = lambda shape: pl.BlockSpec(shape, lambda i: (0,) * len(shape))
    return pl.pallas_call(
        _mla_prep_kernel,
        grid=(t // tm,),
        in_specs=[pl.BlockSpec((tm, MLA_Q_RANK), lambda i: (i, COL_CQ // MLA_Q_RANK)),
                  pl.BlockSpec((tm, 2 * LANES), lambda i: (i, COL_CKV // (2 * LANES))),
                  pl.BlockSpec((tm, LANES), lambda i: (i % nblk, 0)),
                  pl.BlockSpec((tm, LANES), lambda i: (i % nblk, 0)),
                  const((1, MLA_Q_RANK)), const((1, MLA_KV_RANK)),
                  const((MLA_Q_RANK, hw)), const((MLA_KV_RANK, hw)), const((MLA_W, MLA_KV_RANK))],
        out_specs=[pl.BlockSpec((tm, hw), lambda i: (i, 0)), pl.BlockSpec((tm, hw), lambda i: (i, 0)),
                   pl.BlockSpec((tm // ATT_TILE, MLA_HEADS * ATT_VROWS, ATT_TILE), lambda i: (i, 0, 0))],
        out_shape=[jax.ShapeDtypeStruct((t, hw), BF16), jax.ShapeDtypeStruct((t, hw), BF16),
                   jax.ShapeDtypeStruct((t // ATT_TILE, MLA_HEADS * ATT_VROWS, ATT_TILE), BF16)],
        compiler_params=_cparams("parallel"),
        name="mla_prep",
    )(h2, h2, cos_m, sin_m, qn_w.reshape(1, -1), kvn_w.reshape(1, -1), wuq, wuk, wuvt)


def _mla_attn_kernel(q_ref, k_ref, vt_ref, o_ref):
    tq = ATT_TILE
    qi = pl.program_id(1)
    key = lax.broadcasted_iota(jnp.int32, (tq, tq), 0)
    qry = lax.broadcasted_iota(jnp.int32, (tq, tq), 1)
    diag_ok = key < (qry // MASK_CHUNK + 1) * MASK_CHUNK

    def step(kt, carry, masked):
        rows = pl.ds(pl.multiple_of(kt * tq, tq), tq)
        sts = []
        for j in range(MLA_HEADS):
            cols = slice(j * LANES, (j + 1) * LANES)
            sts.append(_dot_nt(k_ref[rows, cols], q_ref[:, cols]))
        new = []
        for g0 in range(0, MLA_HEADS, ATT_GROUP):
            pts = []
            for j in range(g0, g0 + ATT_GROUP):
                m = carry[j][0]
                st = sts[j]
                if masked:
                    st = jnp.where(diag_ok, st, NEG)
                m_new = jnp.maximum(m, jnp.max(st, axis=0, keepdims=True))
                pts.append((m_new, jnp.exp2(m - m_new), jnp.exp2(st - m_new).astype(BF16)))
            for j, (m_new, alpha, pt) in zip(range(g0, g0 + ATT_GROUP), pts):
                acc = alpha * carry[j][1] + _dot(vt_ref[kt, j * ATT_VROWS:(j + 1) * ATT_VROWS, :], pt)
                new.append((m_new, acc))
        return tuple(new)

    init = tuple((jnp.full((1, tq), NEG, F32), jnp.zeros((ATT_VROWS, tq), F32)) for _ in range(MLA_HEADS))
    carry = lax.fori_loop(0, qi, lambda kt, c: step(kt, c, False), init)
    carry = step(qi, carry, True)
    outs = [acc[0:MLA_V, :] / acc[MLA_V:MLA_V + 1, :] for _, acc in carry]
    for p in range(MLA_HEADS // 2):
        o_ref[:, p * LANES:(p + 1) * LANES] = jnp.concatenate([outs[2 * p], outs[2 * p + 1]], axis=0).T


def _mla_attn(q, k, vt, batch, seq):
    t = q.shape[0]
    nq = seq // ATT_TILE
    hw = MLA_HEADS * LANES
    return pl.pallas_call(
        _mla_attn_kernel,
        grid=(batch, nq),
        in_specs=[pl.BlockSpec((ATT_TILE, hw), lambda b, i: (b * nq + i, 0)),
                  pl.BlockSpec((seq, hw), lambda b, i: (b, 0)),
                  pl.BlockSpec((nq, MLA_HEADS * ATT_VROWS, ATT_TILE), lambda b, i: (b, 0, 0))],
        out_specs=pl.BlockSpec((ATT_TILE, MLA_W), lambda b, i: (b * nq + i, 0)),
        out_shape=jax.ShapeDtypeStruct((t, MLA_W), F32),
        compiler_params=_cparams("parallel", "arbitrary"),
        name="mla_attn",
    )(q, k, vt)


def _outproj_kernel(ret_ref, att_ref, ml_ref, x_ref, w_ref, lnw_ref, lnb_ref, o_ref):
    mix = (_dot(ret_ref[...].astype(BF16), w_ref[0:REC_W, :])
           + _dot(att_ref[...].astype(BF16), w_ref[REC_W:REC_W + MLA_W, :])
           + _dot(ml_ref[...].astype(BF16), w_ref[REC_W + MLA_W:, :]))
    o_ref[...] = _layer_norm(DN_ALPHA * x_ref[...] + mix, lnw_ref[...], lnb_ref[...])


def _outproj_ln(ret, att, ml, x2, w_bf16, ln_w, ln_b, tm=512):
    t = x2.shape[0]
    const = lambda shape: pl.BlockSpec(shape, lambda i: (0,) * len(shape))
    row = lambda w: pl.BlockSpec((tm, w), lambda i: (i, 0))
    return pl.pallas_call(
        _outproj_kernel,
        grid=(t // tm,),
        in_specs=[row(REC_W), row(MLA_W), row(REC_W), row(D_MODEL),
                  const((D_MODEL, D_MODEL)), const((1, D_MODEL)), const((1, D_MODEL))],
        out_specs=row(D_MODEL),
        out_shape=jax.ShapeDtypeStruct((t, D_MODEL), F32),
        compiler_params=_cparams("parallel"),
        name="outproj_ln",
    )(ret, att, ml, x2, w_bf16, ln_w.reshape(1, -1), ln_b.reshape(1, -1))


MXU_COLS = 256
SWIGLU_GROUP = 4


def _swiglu_chunk(xb, wg_ref, wu_ref, wd_ref):
    tf = wg_ref.shape[1]
    parts = [slice(s, min(s + MXU_COLS, tf)) for s in range(0, tf, MXU_COLS)]
    out = None
    for g0 in range(0, len(parts), SWIGLU_GROUP):
        group = parts[g0:g0 + SWIGLU_GROUP]
        gu = [(_dot(xb, wg_ref[:, p].astype(BF16)), _dot(xb, wu_ref[:, p].astype(BF16))) for p in group]
        hs = [(_silu(g) * u).astype(BF16) for g, u in gu]
        for h, p in zip(hs, group):
            y = _dot(h, wd_ref[p, :].astype(BF16))
            out = y if out is None else out + y
    return out


def _ffn_kernel(x_ref, wg_ref, wu_ref, wd_ref, lnw_ref, lnb_ref, o_ref):
    x = x_ref[...]
    y = _swiglu_chunk(x.astype(BF16), wg_ref, wu_ref, wd_ref)
    o_ref[...] = _layer_norm(DN_ALPHA * x + y, lnw_ref[...], lnb_ref[...])


def _ffn_ln(x2, wg, wu, wd, ln_w, ln_b, tm=1024):
    t = x2.shape[0]
    dff = wg.shape[1]
    once = lambda shape: pl.BlockSpec(shape, lambda i: (0,) * len(shape), pipeline_mode=pl.Buffered(1))
    return pl.pallas_call(
        _ffn_kernel,
        grid=(t // tm,),
        in_specs=[pl.BlockSpec((tm, D_MODEL), lambda i: (i, 0)),
                  once((D_MODEL, dff)), once((D_MODEL, dff)), once((dff, D_MODEL)),
                  once((1, D_MODEL)), once((1, D_MODEL))],
        out_specs=pl.BlockSpec((tm, D_MODEL), lambda i: (i, 0)),
        out_shape=jax.ShapeDtypeStruct((t, D_MODEL), F32),
        compiler_params=_cparams("parallel"),
        name="ffn_ln",
    )(x2, wg, wu, wd, ln_w.reshape(1, -1), ln_b.reshape(1, -1))


RT_E1, RT_E2, RT_W1, RT_W2, RT_R1, RT_R2 = 0, 1, 2, 3, 4, 5


def _router_kernel(x_ref, r_ref, route_ref, count_ref, carry_ref):
    @pl.when(pl.program_id(0) == 0)
    def _():
        carry_ref[...] = jnp.zeros_like(carry_ref)

    x = x_ref[...]
    xh = x.astype(BF16)
    xm = (x - xh.astype(F32)).astype(BF16)
    rh, rm = r_ref[0], r_ref[1]
    logits = _dot(xh, rh) + (_dot(xh, rm) + _dot(xm, rh))
    tm = logits.shape[0]
    lane = lax.broadcasted_iota(jnp.int32, logits.shape, 1)
    lg = jnp.where(lane < N_EXPERTS, logits, -jnp.inf)
    m1 = jnp.max(lg, axis=-1, keepdims=True)
    i1 = jnp.min(jnp.where(lg == m1, lane, LANES), axis=-1, keepdims=True)
    lg2 = jnp.where(lane == i1, -jnp.inf, lg)
    m2 = jnp.max(lg2, axis=-1, keepdims=True)
    i2 = jnp.min(jnp.where(lg2 == m2, lane, LANES), axis=-1, keepdims=True)
    e2 = jnp.exp(m2 - m1)
    w1 = 1.0 / (1.0 + e2)
    w2 = e2 / (1.0 + e2)

    sel = ((lane == i1) | (lane == i2)).astype(F32)
    before = (lax.broadcasted_iota(jnp.int32, (tm, tm), 0) > lax.broadcasted_iota(jnp.int32, (tm, tm), 1))
    ranks = _dot(before.astype(BF16), sel.astype(BF16)) + carry_ref[...]
    r1 = jnp.sum(jnp.where(lane == i1, ranks, 0.0), axis=-1, keepdims=True)
    r2 = jnp.sum(jnp.where(lane == i2, ranks, 0.0), axis=-1, keepdims=True)
    carry_ref[...] += jnp.sum(sel, axis=0, keepdims=True)
    count_ref[...] = carry_ref[...]

    rec = jnp.zeros(logits.shape, F32)
    for pos, val in ((RT_E1, i1.astype(F32)), (RT_E2, i2.astype(F32)), (RT_W1, w1), (RT_W2, w2),
                     (RT_R1, r1), (RT_R2, r2)):
        rec = jnp.where(lane == pos, val, rec)
    route_ref[...] = rec


def _router(x2, router, tm=1024):
    t = x2.shape[0]
    rp = jnp.zeros((D_MODEL, LANES), F32).at[:, 0:N_EXPERTS].set(router)
    rh = rp.astype(BF16)
    rm = (rp - rh.astype(F32)).astype(BF16)
    r3 = jnp.stack([rh, rm])
    return pl.pallas_call(
        _router_kernel,
        grid=(t // tm,),
        in_specs=[pl.BlockSpec((tm, D_MODEL), lambda i: (i, 0)),
                  pl.BlockSpec((2, D_MODEL, LANES), lambda i: (0, 0, 0))],
        out_specs=[pl.BlockSpec((tm, LANES), lambda i: (i, 0)),
                   pl.BlockSpec((1, LANES), lambda i: (0, 0))],
        out_shape=[jax.ShapeDtypeStruct((t, LANES), F32), jax.ShapeDtypeStruct((1, LANES), F32)],
        scratch_shapes=[pltpu.VMEM((1, LANES), F32)],
        compiler_params=_cparams("arbitrary"),
        name="router",
    )(x2, r3)


MOE_TM = 1024


def _route_plan(route, counts, t):
    cnt = counts[0, :N_EXPERTS].astype(jnp.int32)
    padded = ((cnt + MOE_TM - 1) // MOE_TM) * MOE_TM
    ends = jnp.cumsum(padded)
    offs = ends - padded
    experts = jnp.arange(N_EXPERTS, dtype=jnp.int32)

    def dest(e_lane, r_lane):
        e = route[:, e_lane].astype(jnp.int32)
        off = jnp.sum(jnp.where(e[:, None] == experts[None, :], offs[None, :], 0), axis=1)
        return off + route[:, r_lane].astype(jnp.int32)

    n_tiles = (2 * t) // MOE_TM + N_EXPERTS
    n_used = ends[-1] // MOE_TM
    tile = jnp.minimum(jnp.arange(n_tiles, dtype=jnp.int32), n_used - 1)
    tile_expert = jnp.sum((tile[:, None] * MOE_TM >= ends[None, :]).astype(jnp.int32), axis=1)
    group_end = jnp.sum(jnp.where(tile_expert[:, None] == experts[None, :], (offs + cnt)[None, :], 0), axis=1)
    tile_rows = jnp.clip(group_end - tile * MOE_TM, 0, MOE_TM)
    return dest(RT_E1, RT_R1), dest(RT_E2, RT_R2), tile_expert, n_used.reshape(1), tile_rows, n_tiles


def _dispatch_kernel(d1_ref, d2_ref, x_ref, zero_ref, xs_ref, sem):
    del zero_ref
    tm = x_ref.shape[0]
    base = pl.program_id(0) * tm

    def copies(r):
        src = x_ref.at[pl.ds(r, 1), :]
        return (pltpu.make_async_copy(src, xs_ref.at[pl.ds(d1_ref[base + r], 1), :], sem),
                pltpu.make_async_copy(src, xs_ref.at[pl.ds(d2_ref[base + r], 1), :], sem))

    def start(r, c):
        for cp in copies(r):
            cp.start()
        return c

    def wait(r, c):
        for cp in copies(r):
            cp.wait()
        return c

    lax.fori_loop(0, tm, start, 0, unroll=8)
    lax.fori_loop(0, tm, wait, 0, unroll=8)


def _dispatch(x2, d1, d2, n_rows, tm=512):
    t = x2.shape[0]
    zeros = jnp.zeros((n_rows, D_MODEL), F32)
    return pl.pallas_call(
        _dispatch_kernel,
        grid_spec=pltpu.PrefetchScalarGridSpec(
            num_scalar_prefetch=2,
            grid=(t // tm,),
            in_specs=[pl.BlockSpec((tm, D_MODEL), lambda i, d1, d2: (i, 0)),
                      pl.BlockSpec(memory_space=pl.ANY)],
            out_specs=pl.BlockSpec(memory_space=pl.ANY),
            scratch_shapes=[pltpu.SemaphoreType.DMA(())]),
        out_shape=jax.ShapeDtypeStruct((n_rows, D_MODEL), F32),
        input_output_aliases={3: 0},
        compiler_params=_cparams("arbitrary"),
        name="moe_dispatch",
    )(d1, d2, x2, zeros)


def _moe_ffn_kernel(te_ref, nu_ref, nr_ref, x_ref, wg_ref, wu_ref, wd_ref, o_ref, xb_ref):
    del te_ref
    i = pl.program_id(0)
    f = pl.program_id(1)
    used = i < nu_ref[0]
    half = MOE_TM // 2

    @pl.when((f == 0) | jnp.logical_not(used))
    def _():
        o_ref[...] = jnp.zeros_like(o_ref)

    def swiglu_rows(rows):
        @pl.when(f == 0)
        def _():
            xb_ref[0:rows, :] = x_ref[0:rows, :].astype(BF16)

        o_ref[0:rows, :] += _swiglu_chunk(xb_ref[0:rows, :], wg_ref, wu_ref, wd_ref)

    @pl.when(used & (nr_ref[i] > half))
    def _():
        swiglu_rows(MOE_TM)

    @pl.when(used & (nr_ref[i] <= half))
    def _():
        swiglu_rows(half)


def _moe_ffn(xs, tile_expert, n_used, tile_rows, n_tiles, wg, wu, wd, tf=512):
    dff = wg.shape[2]
    nf = dff // tf
    row = lambda i, f, te, nu, nr: (jnp.minimum(i, nu[0] - 1), 0)
    fcl = lambda i, f, nu: jnp.where(i < nu[0], f, nf - 1)
    return pl.pallas_call(
        _moe_ffn_kernel,
        grid_spec=pltpu.PrefetchScalarGridSpec(
            num_scalar_prefetch=3,
            grid=(n_tiles, nf),
            in_specs=[pl.BlockSpec((MOE_TM, D_MODEL), row),
                      pl.BlockSpec((None, D_MODEL, tf), lambda i, f, te, nu, nr: (te[i], 0, fcl(i, f, nu))),
                      pl.BlockSpec((None, D_MODEL, tf), lambda i, f, te, nu, nr: (te[i], 0, fcl(i, f, nu))),
                      pl.BlockSpec((None, tf, D_MODEL), lambda i, f, te, nu, nr: (te[i], fcl(i, f, nu), 0))],
            out_specs=pl.BlockSpec((MOE_TM, D_MODEL), lambda i, f, te, nu, nr: (i, 0)),
            scratch_shapes=[pltpu.VMEM((MOE_TM, D_MODEL), BF16)]),
        out_shape=jax.ShapeDtypeStruct(xs.shape, F32),
        compiler_params=_cparams("arbitrary", "arbitrary"),
        name="moe_ffn",
    )(tile_expert, n_used, tile_rows, xs, wg, wu, wd)


def _combine_kernel(d1_ref, d2_ref, x_ref, route_ref, ys_ref, lnw_ref, lnb_ref, o_ref, b1_ref, b2_ref, sem):
    tm = x_ref.shape[0]
    i = pl.program_id(0)
    n = pl.num_programs(0)

    def copies(step, r):
        slot = step % 2
        base = step * tm
        return (pltpu.make_async_copy(ys_ref.at[pl.ds(d1_ref[base + r], 1), :],
                                      b1_ref.at[slot, pl.ds(r, 1), :], sem.at[slot]),
                pltpu.make_async_copy(ys_ref.at[pl.ds(d2_ref[base + r], 1), :],
                                      b2_ref.at[slot, pl.ds(r, 1), :], sem.at[slot]))

    def start_tile(step):
        def body(r, c):
            for cp in copies(step, r):
                cp.start()
            return c
        lax.fori_loop(0, tm, body, 0, unroll=8)

    def wait_tile(step):
        def body(r, c):
            for cp in copies(step, r):
                cp.wait()
            return c
        lax.fori_loop(0, tm, body, 0, unroll=8)

    @pl.when(i == 0)
    def _():
        start_tile(i)

    @pl.when(i + 1 < n)
    def _():
        start_tile(i + 1)

    wait_tile(i)
    slot = i % 2
    rt = route_ref[...]
    y = rt[:, RT_W1:RT_W1 + 1] * b1_ref[slot] + rt[:, RT_W2:RT_W2 + 1] * b2_ref[slot]
    o_ref[...] = _layer_norm(DN_ALPHA * x_ref[...] + y, lnw_ref[...], lnb_ref[...])


def _combine_ln(x2, route, ys, d1, d2, ln_w, ln_b, tm=256):
    t = x2.shape[0]
    const = lambda shape: pl.BlockSpec(shape, lambda i, d1, d2: (0,) * len(shape))
    return pl.pallas_call(
        _combine_kernel,
        grid_spec=pltpu.PrefetchScalarGridSpec(
            num_scalar_prefetch=2,
            grid=(t // tm,),
            in_specs=[pl.BlockSpec((tm, D_MODEL), lambda i, d1, d2: (i, 0)),
                      pl.BlockSpec((tm, LANES), lambda i, d1, d2: (i, 0)),
                      pl.BlockSpec(memory_space=pl.ANY),
                      const((1, D_MODEL)), const((1, D_MODEL))],
            out_specs=pl.BlockSpec((tm, D_MODEL), lambda i, d1, d2: (i, 0)),
            scratch_shapes=[pltpu.VMEM((2, tm, D_MODEL), F32), pltpu.VMEM((2, tm, D_MODEL), F32),
                            pltpu.SemaphoreType.DMA((2,))]),
        out_shape=jax.ShapeDtypeStruct((t, D_MODEL), F32),
        compiler_params=_cparams("arbitrary"),
        name="moe_combine_ln",
    )(d1, d2, x2, route, ys, ln_w.reshape(1, -1), ln_b.reshape(1, -1))


def _moe_ln(x2, router, wg, wu, wd, ln_w, ln_b):
    t = x2.shape[0]
    route, counts = _router(x2, router)
    d1, d2, tile_expert, n_used, tile_rows, n_tiles = _route_plan(route, counts, t)
    xs = _dispatch(x2, d1, d2, n_tiles * MOE_TM)
    ys = _moe_ffn(xs, tile_expert, n_used, tile_rows, n_tiles, wg, wu, wd)
    return _combine_ln(x2, route, ys, d1, d2, ln_w, ln_b)


def _layout_w_in(w):
    sizes = (REC_W, REC_W, REC_W, REC_W, MLA_Q_RANK, MLA_KV_RANK, MLA_ROPE,
             REC_W, REC_W, REC_W, REC_W, N_RHEADS, N_RHEADS)
    offs = np.concatenate([[0], np.cumsum(sizes)])
    part = lambda j: w[:, offs[j]:offs[j + 1]]
    z = lambda n: jnp.zeros((w.shape[0], n), w.dtype)
    misc = jnp.concatenate([part(11), part(12), z(MISC_KR - 2 * N_RHEADS), part(6),
                            z(LANES - MISC_KR - MLA_ROPE)], axis=1)
    cols = [part(0), part(1), part(2), part(3), part(7), part(8), part(9), part(10), part(4), part(5), misc]
    return jnp.concatenate(cols, axis=1).astype(BF16)


def _layout_mla(w_uq, w_ukv):
    r = w_uq.shape[0]
    uq = w_uq.reshape(r, MLA_HEADS, MLA_NOPE + MLA_ROPE)
    uq = jnp.pad(uq, ((0, 0), (0, 0), (0, LANES - MLA_NOPE - MLA_ROPE))).reshape(r, MLA_HEADS * LANES)
    r = w_ukv.shape[0]
    ukv = w_ukv.reshape(r, MLA_HEADS, MLA_NOPE + MLA_V)
    uk = jnp.pad(ukv[:, :, :MLA_NOPE], ((0, 0), (0, 0), (0, LANES - MLA_NOPE))).reshape(r, MLA_HEADS * LANES)
    uvt = ukv[:, :, MLA_NOPE:].reshape(r, MLA_W).T
    return uq.astype(BF16), uk.astype(BF16), uvt.astype(BF16)


def kernel(x, w_in, ret_gn_w, mla_q_norm_w, mla_w_uq, mla_kv_norm_w, mla_w_ukv, mlstm_conv_w, mlstm_conv_b,
           mlstm_b_i, mlstm_b_f, mlstm_gn_w, w_out, ln1_w, ln1_b, ffn_w_gate, ffn_w_up, ffn_w_down,
           moe_router, moe_w_gate, moe_w_up, moe_w_down, ln2_w, ln2_b):
    bsz, seq, d = x.shape
    t = bsz * seq
    cos_r, sin_r, cos_m, sin_m = _rope_tables(seq)
    x2 = x.reshape(t, d)
    for l in range(DEPTH):
        h2 = _inproj(x2, _layout_w_in(w_in[l]))
        h3 = h2.reshape(bsz, seq, D_IN_PAD)
        ret = _retention(h3, cos_r, sin_r, ret_gn_w[l]).reshape(t, REC_W)
        ml = _mlstm(h3, mlstm_conv_w[l], mlstm_conv_b[l], mlstm_b_i[l], mlstm_b_f[l],
                    mlstm_gn_w[l]).reshape(t, REC_W)
        wuq, wuk, wuvt = _layout_mla(mla_w_uq[l], mla_w_ukv[l])
        q, k, vt = _mla_prep(h2, seq, cos_m, sin_m, mla_q_norm_w[l], mla_kv_norm_w[l], wuq, wuk, wuvt)
        att = _mla_attn(q, k, vt, bsz, seq)
        x2 = _outproj_ln(ret, att, ml, x2, w_out[l].astype(BF16), ln1_w[l], ln1_b[l])
        if l % 2 == 0:
            j = l // 2
            x2 = _ffn_ln(x2, ffn_w_gate[j].astype(BF16), ffn_w_up[j].astype(BF16),
                         ffn_w_down[j].astype(BF16), ln2_w[l], ln2_b[l])
        else:
            j = l // 2
            x2 = _moe_ln(x2, moe_router[j], moe_w_gate[j], moe_w_up[j], moe_w_down[j], ln2_w[l], ln2_b[l])
    return x2.reshape(bsz, seq, d)
```

```python
import functools

import numpy as np
import jax
import jax.numpy as jnp
from jax import lax
from jax.experimental import pallas as pl
from jax.experimental.pallas import tpu as pltpu

F32 = jnp.float32
BF16 = jnp.bfloat16

D_MODEL = 1024
DEPTH = 2
ROPE_BASE = 10000.0
LN_EPS = 1e-5
NEG = -1e30

HEAD_DIM = 64
N_RHEADS = 4
REC_W = N_RHEADS * HEAD_DIM
MLA_HEADS = 8
MLA_NOPE = 64
MLA_ROPE = 32
MLA_V = 64
MLA_Q_RANK = 256
MLA_KV_RANK = 128
MLA_W = MLA_HEADS * MLA_V
MLSTM_CONV = 4
MASK_CHUNK = 64
REC_CHUNK = 256
REC_SEQS = 2
MLSTM_SEQS = 1
ATT_TILE = 256
ATT_GROUP = 4
ATT_VROWS = 80
N_EXPERTS = 8
LANES = 128

DN_ALPHA = (2 * DEPTH) ** 0.25
LOG2_E = 1.4426950408889634

COL_RET = 0
COL_MLSTM = 1024
COL_CQ = 2048
COL_CKV = 2304
COL_MISC = 2432
D_IN_PAD = 2560
MISC_KR = 64

VMEM_LIMIT = 56 * 1024 * 1024


def _cparams(*sem):
    return pltpu.CompilerParams(dimension_semantics=sem, vmem_limit_bytes=VMEM_LIMIT)


def _layer_norm(y, w, b):
    mu = jnp.mean(y, axis=-1, keepdims=True)
    d = y - mu
    var = jnp.mean(d * d, axis=-1, keepdims=True)
    return d * lax.rsqrt(var + LN_EPS) * w + b


def _silu(x):
    return x * (1.0 / (1.0 + jnp.exp(-x)))


def _sigmoid(x):
    return 1.0 / (1.0 + jnp.exp(-x))


def _split2(x):
    hi = x.astype(BF16)
    lo = (x - hi.astype(F32)).astype(BF16)
    return hi, lo


def _dot(a, b):
    return jnp.dot(a, b, preferred_element_type=F32)


def _dot_nt(a, b):
    return lax.dot_general(a, b, (((1,), (1,)), ((), ())), preferred_element_type=F32)


def _dot_exact_rhs(x, m_bf16):
    hi, lo = _split2(x)
    return _dot(hi, m_bf16) + _dot(lo, m_bf16)


def _swap_halves(x, half):
    n = x.shape[-1]
    lane = lax.broadcasted_iota(jnp.int32, x.shape, x.ndim - 1)
    first = (lane % (2 * half)) < half
    return jnp.where(first, pltpu.roll(x, n - half, x.ndim - 1), pltpu.roll(x, half, x.ndim - 1))


def _head_norm(x, avg_bf16, w):
    mu = _dot_exact_rhs(x, avg_bf16)
    d = x - mu
    var = _dot_exact_rhs(d * d, avg_bf16)
    return d * lax.rsqrt(var + LN_EPS) * w


def _expand_heads(v, lane_head):
    out = jnp.zeros((v.shape[0], REC_W), F32)
    for h in range(N_RHEADS):
        out = jnp.where(lane_head == h, v[:, h:h + 1], out)
    return out


def _rope_tables(seq):
    pos = np.arange(seq, dtype=np.float64)[:, None]
    half = HEAD_DIM // 2
    inv = ROPE_BASE ** (-np.arange(half, dtype=np.float64) / half)
    ang = pos * inv[None, :]
    cos_r = np.tile(np.concatenate([np.cos(ang), np.cos(ang)], -1), (1, N_RHEADS))
    sin_r = np.tile(np.concatenate([-np.sin(ang), np.sin(ang)], -1), (1, N_RHEADS))
    half = MLA_ROPE // 2
    inv = ROPE_BASE ** (-np.arange(half, dtype=np.float64) / half)
    ang = pos * inv[None, :]
    cos_m = np.ones((seq, LANES))
    sin_m = np.zeros((seq, LANES))
    cos_m[:, MISC_KR:MISC_KR + MLA_ROPE] = np.concatenate([np.cos(ang), np.cos(ang)], -1)
    sin_m[:, MISC_KR:MISC_KR + MLA_ROPE] = np.concatenate([-np.sin(ang), np.sin(ang)], -1)
    f = lambda a: jnp.asarray(a.astype(np.float32))
    return f(cos_r), f(sin_r), f(cos_m), f(sin_m)


def _retention_tables():
    L = REC_CHUNK
    log_gamma = np.log(1.0 - 2.0 ** (-5.0 - np.arange(N_RHEADS, dtype=np.float64)))
    idx = np.arange(L, dtype=np.float64)
    diff = idx[:, None] - idx[None, :]
    dmask = np.where(diff >= 0, np.exp(diff[None] * log_gamma[:, None, None]), 0.0)
    lane_lg = np.repeat(log_gamma, HEAD_DIM)[None, :]
    qw = np.exp((idx + 1.0)[:, None] * lane_lg)
    kw = np.exp((L - 1 - idx)[:, None] * lane_lg)
    cd = np.exp(L * lane_lg)
    f = lambda a: jnp.asarray(a.astype(np.float32))
    return f(dmask), f(qw), f(kw), f(cd)


def _head_avg_matrix():
    h = np.arange(REC_W) // HEAD_DIM
    return jnp.asarray((h[:, None] == h[None, :]).astype(np.float32) / HEAD_DIM, dtype=BF16)


def _inproj_kernel(x_ref, w_ref, o_ref):
    o_ref[...] = _dot(x_ref[...].astype(BF16), w_ref[...])


def _inproj(x2, w_bf16, tm=512):
    t, k = x2.shape
    n = w_bf16.shape[1]
    return pl.pallas_call(
        _inproj_kernel,
        grid=(t // tm,),
        in_specs=[pl.BlockSpec((tm, k), lambda i: (i, 0)),
                  pl.BlockSpec((k, n), lambda i: (0, 0))],
        out_specs=pl.BlockSpec((tm, n), lambda i: (i, 0)),
        out_shape=jax.ShapeDtypeStruct((t, n), F32),
        compiler_params=_cparams("parallel"),
        name="inproj",
    )(x2, w_bf16)


def _retention_kernel(h_ref, cos_ref, sin_ref, dmask_ref, qw_ref, kw_ref, cd_ref, avg_ref, gnw_ref,
                      o_ref, state_ref):
    L = REC_CHUNK
    nb = h_ref.shape[0]
    nc = h_ref.shape[1] // L
    seqs = range(nb)
    heads = range(N_RHEADS)
    lane_head = lax.broadcasted_iota(jnp.int32, (1, REC_W), 1) // HEAD_DIM
    row_head = lax.broadcasted_iota(jnp.int32, (REC_W, REC_W), 0) // HEAD_DIM
    col_head = lax.broadcasted_iota(jnp.int32, (REC_W, REC_W), 1) // HEAD_DIM
    block_diag = row_head == col_head
    state_ref[...] = jnp.zeros_like(state_ref)

    def chunk(c, carry):
        r0 = pl.multiple_of(c * L, L)
        rows = pl.ds(r0, L)
        cos = cos_ref[rows, :]
        sin = sin_ref[rows, :]
        qs, ks, kbs, vbs = [], [], [], []
        for s in seqs:
            q = h_ref[s, rows, 0:REC_W]
            k = h_ref[s, rows, REC_W:2 * REC_W]
            qs.append(q * cos + _swap_halves(q, HEAD_DIM // 2) * sin)
            k = (k * cos + _swap_halves(k, HEAD_DIM // 2) * sin) * (HEAD_DIM ** -0.5)
            ks.append(k)
            kbs.append(k.astype(BF16))
            vbs.append(h_ref[s, rows, 2 * REC_W:3 * REC_W].astype(BF16))
        scores = [[_dot_nt(jnp.where(lane_head == h, qs[s], 0.0).astype(BF16), kbs[s]) for h in heads]
                  for s in seqs]
        probs = [[(scores[s][h] * dmask_ref[h]).astype(BF16) for h in heads] for s in seqs]
        rets = []
        for s in seqs:
            inner = jnp.zeros((L, REC_W), F32)
            for h in heads:
                inner = jnp.where(lane_head == h, _dot(probs[s][h], vbs[s]), inner)
            state = state_ref[s]
            cross = _dot(qs[s].astype(BF16), state.astype(BF16)) * qw_ref[...]
            rets.append(inner + cross)
            kwt = (ks[s] * kw_ref[...]).T.astype(BF16)
            loc = _dot(kwt, vbs[s])
            state_ref[s] = cd_ref[...] * state + jnp.where(block_diag, loc, 0.0)
        for s in seqs:
            g = h_ref[s, rows, 3 * REC_W:4 * REC_W]
            o_ref[s, rows, :] = _silu(g) * _head_norm(rets[s], avg_ref[...], gnw_ref[...])
        return carry

    lax.fori_loop(0, nc, chunk, 0)


def _retention(h3, cos_r, sin_r, gn_w):
    b, s, _ = h3.shape
    dmask, qw, kw, cd = _retention_tables()
    avg = _head_avg_matrix()
    const = lambda shape: pl.BlockSpec(shape, lambda i: (0,) * len(shape))
    return pl.pallas_call(
        _retention_kernel,
        grid=(b // REC_SEQS,),
        in_specs=[pl.BlockSpec((REC_SEQS, s, 4 * REC_W), lambda i: (i, 0, COL_RET // (4 * REC_W))),
                  const((s, REC_W)), const((s, REC_W)),
                  const((N_RHEADS, REC_CHUNK, REC_CHUNK)), const((REC_CHUNK, REC_W)),
                  const((REC_CHUNK, REC_W)), const((1, REC_W)), const((REC_W, REC_W)), const((1, REC_W))],
        out_specs=pl.BlockSpec((REC_SEQS, s, REC_W), lambda i: (i, 0, 0)),
        out_shape=jax.ShapeDtypeStruct((b, s, REC_W), F32),
        scratch_shapes=[pltpu.VMEM((REC_SEQS, REC_W, REC_W), F32)],
        compiler_params=_cparams("parallel"),
        name="retention",
    )(h3, cos_r, sin_r, dmask, qw, kw, cd, avg, gn_w.reshape(1, REC_W))


def _mlstm_kernel(h_ref, misc_ref, convw_ref, convb_ref, bif_ref, avg_ref, gnw_ref,
                  o_ref, c_ref, n_ref, m_ref):
    L = REC_CHUNK
    nb = h_ref.shape[0]
    nc = h_ref.shape[1] // L
    seqs = range(nb)
    heads = range(N_RHEADS)
    lane_head = lax.broadcasted_iota(jnp.int32, (1, REC_W), 1) // HEAD_DIM
    row_head = lax.broadcasted_iota(jnp.int32, (REC_W, REC_W), 0) // HEAD_DIM
    col_head = lax.broadcasted_iota(jnp.int32, (REC_W, REC_W), 1) // HEAD_DIM
    block_diag = row_head == col_head
    n_mask = (lax.broadcasted_iota(jnp.int32, (REC_W, LANES), 0) // HEAD_DIM
              == lax.broadcasted_iota(jnp.int32, (REC_W, LANES), 1))
    lane128 = lax.broadcasted_iota(jnp.int32, (1, LANES), 1)
    gate_lanes = lane128 < N_RHEADS
    ri = lax.broadcasted_iota(jnp.int32, (L, L), 0)
    ci = lax.broadcasted_iota(jnp.int32, (L, L), 1)
    causal = ri >= ci
    tri = causal.astype(BF16)
    row8 = lax.broadcasted_iota(jnp.int32, (8, 2 * REC_W), 0)

    c_ref[...] = jnp.zeros_like(c_ref)
    n_ref[...] = jnp.zeros_like(n_ref)
    m_ref[...] = jnp.full(m_ref.shape, NEG, F32)

    def prepare(s, c, r0, rows):
        x = h_ref[s, rows, 0:2 * REC_W]
        prev = h_ref[s, pl.ds(pl.multiple_of(jnp.maximum(r0 - 8, 0), 8), 8), 0:2 * REC_W]
        prev = jnp.where(c > 0, prev, 0.0)
        acc = x * convw_ref[MLSTM_CONV - 1:MLSTM_CONV, :] + convb_ref[...]
        for sft in range(1, MLSTM_CONV):
            xs = pltpu.roll(x, sft, 0)
            head = jnp.where(row8 < sft, pltpu.roll(prev, sft, 0), xs[0:8, :])
            xs = jnp.concatenate([head, xs[8:, :]], axis=0)
            acc = acc + xs * convw_ref[MLSTM_CONV - 1 - sft:MLSTM_CONV - sft, :]
        qk = _silu(acc)
        q = qk[:, 0:REC_W]
        k = qk[:, REC_W:2 * REC_W] * (HEAD_DIM ** -0.5)

        gates = misc_ref[s, rows, :] + bif_ref[...]
        ic_c = jnp.where(gate_lanes, gates, 0.0)
        fg = pltpu.roll(gates, LANES - N_RHEADS, 1)
        lf_c = jnp.where(gate_lanes, jnp.minimum(fg, 0.0) - jnp.log1p(jnp.exp(-jnp.abs(fg))), 0.0)

        bcum = _dot_exact_rhs_left(tri, lf_c)
        gsum = bcum[L - 1:L, :]
        m_s = m_ref[s]
        a = gsum - bcum + ic_c
        qb = q.astype(BF16)
        c_state = c_ref[s]
        n_state = n_ref[s]
        return dict(q=q, k=k, qb=qb, kb=k.astype(BF16), vb=h_ref[s, rows, 2 * REC_W:3 * REC_W].astype(BF16),
                    bcum=bcum, gsum=gsum, m_s=m_s, a=a, amax=jnp.max(a, axis=0, keepdims=True),
                    rvec=(ic_c - bcum).T,
                    inter=bcum + m_s,
                    c_state=c_state, n_state=n_state,
                    q_c=_dot(qb, c_state.astype(BF16)),
                    q_n=_dot(qb, n_state.astype(BF16)))

    def decay(d, scores):
        cmats, stats = [], []
        for h in heads:
            log_d = jnp.where(causal, d["bcum"][:, h:h + 1] + d["rvec"][h:h + 1, :], NEG)
            inter_h = d["inter"][:, h:h + 1]
            m_t = jnp.maximum(inter_h, jnp.max(log_d, axis=-1, keepdims=True))
            cmat = scores[h] * jnp.exp(log_d - m_t)
            si = jnp.exp(inter_h - m_t)
            den = jnp.sum(cmat, axis=-1, keepdims=True) + si * d["q_n"][:, h:h + 1]
            cmats.append(cmat.astype(BF16))
            stats.append((si, jnp.maximum(jnp.abs(den), jnp.exp(-m_t))))
        return cmats, stats

    def cell(d, cmats, stats):
        hcell = jnp.zeros((L, REC_W), F32)
        for h in heads:
            si, denom = stats[h]
            num = _dot(cmats[h], d["vb"]) + si * d["q_c"]
            hcell = jnp.where(lane_head == h, num / denom, hcell)
        return hcell

    def carry_state(s, d):
        wa = jnp.exp(d["a"] - d["amax"])
        m_new = jnp.maximum(d["gsum"] + d["m_s"], d["amax"])
        sp = jnp.exp(d["gsum"] + d["m_s"] - m_new)
        sl = jnp.exp(d["amax"] - m_new)
        kwt = (d["k"] * _expand_heads(wa, lane_head)).T.astype(BF16)
        loc_c = _dot(kwt, d["vb"])
        loc_n = _dot(kwt, jnp.ones((L, LANES), BF16))
        c_ref[s] = (d["c_state"] * _expand_heads(sp, lane_head)
                    + jnp.where(block_diag, loc_c, 0.0) * _expand_heads(sl, lane_head))
        n_ref[s] = d["n_state"] * sp + jnp.where(n_mask, loc_n, 0.0) * sl
        m_ref[s] = m_new

    def chunk(c, carry):
        r0 = pl.multiple_of(c * L, L)
        rows = pl.ds(r0, L)
        ds = [prepare(s, c, r0, rows) for s in seqs]
        scores = [[_dot_nt(jnp.where(lane_head == h, d["q"], 0.0).astype(BF16), d["kb"]) for h in heads]
                  for d in ds]
        decs = [decay(d, sc) for d, sc in zip(ds, scores)]
        cells = [cell(d, *dec) for d, dec in zip(ds, decs)]
        for s in seqs:
            carry_state(s, ds[s])
        for s in seqs:
            og = h_ref[s, rows, 3 * REC_W:4 * REC_W]
            o_ref[s, rows, :] = _head_norm(_sigmoid(og) * cells[s], avg_ref[...], gnw_ref[...])
        return carry

    lax.fori_loop(0, nc, chunk, 0)


def _dot_exact_rhs_left(m_bf16, x):
    hi, lo = _split2(x)
    return _dot(m_bf16, hi) + _dot(m_bf16, lo)


def _mlstm(h3, conv_w, conv_b, b_i, b_f, gn_w):
    b, s, _ = h3.shape
    avg = _head_avg_matrix()
    bif = jnp.zeros((1, LANES), F32).at[0, 0:N_RHEADS].set(b_i).at[0, N_RHEADS:2 * N_RHEADS].set(b_f)
    const = lambda shape: pl.BlockSpec(shape, lambda i: (0,) * len(shape))
    return pl.pallas_call(
        _mlstm_kernel,
        grid=(b // MLSTM_SEQS,),
        in_specs=[pl.BlockSpec((MLSTM_SEQS, s, 4 * REC_W), lambda i: (i, 0, COL_MLSTM // (4 * REC_W))),
                  pl.BlockSpec((MLSTM_SEQS, s, LANES), lambda i: (i, 0, COL_MISC // LANES)),
                  const((MLSTM_CONV, 2 * REC_W)), const((1, 2 * REC_W)), const((1, LANES)),
                  const((REC_W, REC_W)), const((1, REC_W))],
        out_specs=pl.BlockSpec((MLSTM_SEQS, s, REC_W), lambda i: (i, 0, 0)),
        out_shape=jax.ShapeDtypeStruct((b, s, REC_W), F32),
        scratch_shapes=[pltpu.VMEM((MLSTM_SEQS, REC_W, REC_W), F32),
                        pltpu.VMEM((MLSTM_SEQS, REC_W, LANES), F32), pltpu.VMEM((MLSTM_SEQS, 1, LANES), F32)],
        compiler_params=_cparams("parallel"),
        name="mlstm",
    )(h3, h3, conv_w, conv_b.reshape(1, -1), bif, avg, gn_w.reshape(1, REC_W))


def _mla_prep_kernel(cq_ref, ckv_ref, cos_ref, sin_ref, qnw_ref, kvnw_ref, wuq_ref, wuk_ref, wuvt_ref,
                     q_ref, k_ref, vt_ref):
    scale = (MLA_NOPE + MLA_ROPE) ** -0.5 * LOG2_E
    cos = cos_ref[...]
    sin = sin_ref[...]
    cos_all = jnp.concatenate([cos] * MLA_HEADS, axis=1)
    sin_all = jnp.concatenate([sin] * MLA_HEADS, axis=1)

    cq = cq_ref[...]
    cqn = cq * lax.rsqrt(jnp.mean(cq * cq, axis=-1, keepdims=True) + LN_EPS) * qnw_ref[...]
    q = _dot(cqn.astype(BF16), wuq_ref[...])
    q = q * cos_all + _swap_halves(q, MLA_ROPE // 2) * sin_all
    q_ref[...] = (q * scale).astype(BF16)

    ckv = ckv_ref[:, 0:MLA_KV_RANK]
    kvn = ckv * lax.rsqrt(jnp.mean(ckv * ckv, axis=-1, keepdims=True) + LN_EPS) * kvnw_ref[...]
    kvb = kvn.astype(BF16)
    misc = ckv_ref[:, MLA_KV_RANK:MLA_KV_RANK + LANES]
    lane = lax.broadcasted_iota(jnp.int32, (1, LANES), 1)
    kr = misc * cos + _swap_halves(misc, MLA_ROPE // 2) * sin
    kr = jnp.where((lane >= MISC_KR) & (lane < MISC_KR + MLA_ROPE), kr, 0.0)
    k = _dot(kvb, wuk_ref[...]) + jnp.concatenate([kr] * MLA_HEADS, axis=1)
    k_ref[...] = k.astype(BF16)
    vt = _dot_nt(wuvt_ref[...], kvb).astype(BF16)
    ones = jnp.ones((ATT_VROWS - MLA_V, ATT_TILE), BF16)
    for j in range(vt_ref.shape[0]):
        for h in range(MLA_HEADS):
            vt_ref[j, h * ATT_VROWS:h * ATT_VROWS + MLA_V, :] = vt[h * MLA_V:(h + 1) * MLA_V,
                                                                   j * ATT_TILE:(j + 1) * ATT_TILE]
            vt_ref[j, h * ATT_VROWS + MLA_V:(h + 1) * ATT_VROWS, :] = ones


def _mla_prep(h2, seq, cos_m, sin_m, qn_w, kvn_w, wuq, wuk, wuvt, tm=512):
    t = h2.shape[0]
    nblk = seq // tm
    hw = MLA_HEADS * LANES
    const = lambda shape: pl.BlockSpec(shape, lambda i: (0,) * len(shape))
    return pl.pallas_call(
        _mla_prep_kernel,
        grid=(t // tm,),
        in_specs=[pl.BlockSpec((tm, MLA_Q_RANK), lambda i: (i, COL_CQ // MLA_Q_RANK)),
                  pl.BlockSpec((tm, 2 * LANES), lambda i: (i, COL_CKV // (2 * LANES))),
                  pl.BlockSpec((tm, LANES), lambda i: (i % nblk, 0)),
                  pl.BlockSpec((tm, LANES), lambda i: (i % nblk, 0)),
                  const((1, MLA_Q_RANK)), const((1, MLA_KV_RANK)),
                  const((MLA_Q_RANK, hw)), const((MLA_KV_RANK, hw)), const((MLA_W, MLA_KV_RANK))],
        out_specs=[pl.BlockSpec((tm, hw), lambda i: (i, 0)), pl.BlockSpec((tm, hw), lambda i: (i, 0)),
                   pl.BlockSpec((tm // ATT_TILE, MLA_HEADS * ATT_VROWS, ATT_TILE), lambda i: (i, 0, 0))],
        out_shape=[jax.ShapeDtypeStruct((t, hw), BF16), jax.ShapeDtypeStruct((t, hw), BF16),
                   jax.ShapeDtypeStruct((t // ATT_TILE, MLA_HEADS * ATT_VROWS, ATT_TILE), BF16)],
        compiler_params=_cparams("parallel"),
        name="mla_prep",
    )(h2, h2, cos_m, sin_m, qn_w.reshape(1, -1), kvn_w.reshape(1, -1), wuq, wuk, wuvt)


def _mla_attn_kernel(q_ref, k_ref, vt_ref, o_ref):
    tq = ATT_TILE
    qi = pl.program_id(1)
    key = lax.broadcasted_iota(jnp.int32, (tq, tq), 0)
    qry = lax.broadcasted_iota(jnp.int32, (tq, tq), 1)
    diag_ok = key < (qry // MASK_CHUNK + 1) * MASK_CHUNK

    def step(kt, carry, masked):
        rows = pl.ds(pl.multiple_of(kt * tq, tq), tq)
        sts = []
        for j in range(MLA_HEADS):
            cols = slice(j * LANES, (j + 1) * LANES)
            sts.append(_dot_nt(k_ref[rows, cols], q_ref[:, cols]))
        new = []
        for g0 in range(0, MLA_HEADS, ATT_GROUP):
            pts = []
            for j in range(g0, g0 + ATT_GROUP):
                m = carry[j][0]
                st = sts[j]
                if masked:
                    st = jnp.where(diag_ok, st, NEG)
                m_new = jnp.maximum(m, jnp.max(st, axis=0, keepdims=True))
                pts.append((m_new, jnp.exp2(m - m_new), jnp.exp2(st - m_new).astype(BF16)))
            for j, (m_new, alpha, pt) in zip(range(g0, g0 + ATT_GROUP), pts):
                acc = alpha * carry[j][1] + _dot(vt_ref[kt, j * ATT_VROWS:(j + 1) * ATT_VROWS, :], pt)
                new.append((m_new, acc))
        return tuple(new)

    init = tuple((jnp.full((1, tq), NEG, F32), jnp.zeros((ATT_VROWS, tq), F32)) for _ in range(MLA_HEADS))
    carry = lax.fori_loop(0, qi, lambda kt, c: step(kt, c, False), init)
    carry = step(qi, carry, True)
    outs = [acc[0:MLA_V, :] / acc[MLA_V:MLA_V + 1, :] for _, acc in carry]
    for p in range(MLA_HEADS // 2):
        o_ref[:, p * LANES:(p + 1) * LANES] = jnp.concatenate([outs[2 * p], outs[2 * p + 1]], axis=0).T


def _mla_attn(q, k, vt, batch, seq):
    t = q.shape[0]
    nq = seq // ATT_TILE
    hw = MLA_HEADS * LANES
    return pl.pallas_call(
        _mla_attn_kernel,
        grid=(batch, nq),
        in_specs=[pl.BlockSpec((ATT_TILE, hw), lambda b, i: (b * nq + i, 0)),
                  pl.BlockSpec((seq, hw), lambda b, i: (b, 0)),
                  pl.BlockSpec((nq, MLA_HEADS * ATT_VROWS, ATT_TILE), lambda b, i: (b, 0, 0))],
        out_specs=pl.BlockSpec((ATT_TILE, MLA_W), lambda b, i: (b * nq + i, 0)),
        out_shape=jax.ShapeDtypeStruct((t, MLA_W), F32),
        compiler_params=_cparams("parallel", "arbitrary"),
        name="mla_attn",
    )(q, k, vt)


def _outproj_kernel(ret_ref, att_ref, ml_ref, x_ref, w_ref, lnw_ref, lnb_ref, o_ref):
    mix = (_dot(ret_ref[...].astype(BF16), w_ref[0:REC_W, :])
           + _dot(att_ref[...].astype(BF16), w_ref[REC_W:REC_W + MLA_W, :])
           + _dot(ml_ref[...].astype(BF16), w_ref[REC_W + MLA_W:, :]))
    o_ref[...] = _layer_norm(DN_ALPHA * x_ref[...] + mix, lnw_ref[...], lnb_ref[...])


def _outproj_ln(ret, att, ml, x2, w_bf16, ln_w, ln_b, tm=512):
    t = x2.shape[0]
    const = lambda shape: pl.BlockSpec(shape, lambda i: (0,) * len(shape))
    row = lambda w: pl.BlockSpec((tm, w), lambda i: (i, 0))
    return pl.pallas_call(
        _outproj_kernel,
        grid=(t // tm,),
        in_specs=[row(REC_W), row(MLA_W), row(REC_W), row(D_MODEL),
                  const((D_MODEL, D_MODEL)), const((1, D_MODEL)), const((1, D_MODEL))],
        out_specs=row(D_MODEL),
        out_shape=jax.ShapeDtypeStruct((t, D_MODEL), F32),
        compiler_params=_cparams("parallel"),
        name="outproj_ln",
    )(ret, att, ml, x2, w_bf16, ln_w.reshape(1, -1), ln_b.reshape(1, -1))


MXU_COLS = 256
SWIGLU_GROUP = 4


def _swiglu_chunk(xb, wg_ref, wu_ref, wd_ref):
    tf = wg_ref.shape[1]
    parts = [slice(s, min(s + MXU_COLS, tf)) for s in range(0, tf, MXU_COLS)]
    out = None
    for g0 in range(0, len(parts), SWIGLU_GROUP):
        group = parts[g0:g0 + SWIGLU_GROUP]
        gu = [(_dot(xb, wg_ref[:, p].astype(BF16)), _dot(xb, wu_ref[:, p].astype(BF16))) for p in group]
        hs = [(_silu(g) * u).astype(BF16) for g, u in gu]
        for h, p in zip(hs, group):
            y = _dot(h, wd_ref[p, :].astype(BF16))
            out = y if out is None else out + y
    return out


def _ffn_kernel(x_ref, wg_ref, wu_ref, wd_ref, lnw_ref, lnb_ref, o_ref):
    x = x_ref[...]
    y = _swiglu_chunk(x.astype(BF16), wg_ref, wu_ref, wd_ref)
    o_ref[...] = _layer_norm(DN_ALPHA * x + y, lnw_ref[...], lnb_ref[...])


def _ffn_ln(x2, wg, wu, wd, ln_w, ln_b, tm=1024):
    t = x2.shape[0]
    dff = wg.shape[1]
    once = lambda shape: pl.BlockSpec(shape, lambda i: (0,) * len(shape), pipeline_mode=pl.Buffered(1))
    return pl.pallas_call(
        _ffn_kernel,
        grid=(t // tm,),
        in_specs=[pl.BlockSpec((tm, D_MODEL), lambda i: (i, 0)),
                  once((D_MODEL, dff)), once((D_MODEL, dff)), once((dff, D_MODEL)),
                  once((1, D_MODEL)), once((1, D_MODEL))],
        out_specs=pl.BlockSpec((tm, D_MODEL), lambda i: (i, 0)),
        out_shape=jax.ShapeDtypeStruct((t, D_MODEL), F32),
        compiler_params=_cparams("parallel"),
        name="ffn_ln",
    )(x2, wg, wu, wd, ln_w.reshape(1, -1), ln_b.reshape(1, -1))


RT_E1, RT_E2, RT_W1, RT_W2, RT_R1, RT_R2 = 0, 1, 2, 3, 4, 5


def _router_kernel(x_ref, r_ref, route_ref, count_ref, carry_ref):
    @pl.when(pl.program_id(0) == 0)
    def _():
        carry_ref[...] = jnp.zeros_like(carry_ref)

    x = x_ref[...]
    xh = x.astype(BF16)
    xm = (x - xh.astype(F32)).astype(BF16)
    rh, rm = r_ref[0], r_ref[1]
    logits = _dot(xh, rh) + (_dot(xh, rm) + _dot(xm, rh))
    tm = logits.shape[0]
    lane = lax.broadcasted_iota(jnp.int32, logits.shape, 1)
    lg = jnp.where(lane < N_EXPERTS, logits, -jnp.inf)
    m1 = jnp.max(lg, axis=-1, keepdims=True)
    i1 = jnp.min(jnp.where(lg == m1, lane, LANES), axis=-1, keepdims=True)
    lg2 = jnp.where(lane == i1, -jnp.inf, lg)
    m2 = jnp.max(lg2, axis=-1, keepdims=True)
    i2 = jnp.min(jnp.where(lg2 == m2, lane, LANES), axis=-1, keepdims=True)
    e2 = jnp.exp(m2 - m1)
    w1 = 1.0 / (1.0 + e2)
    w2 = e2 / (1.0 + e2)

    sel = ((lane == i1) | (lane == i2)).astype(F32)
    before = (lax.broadcasted_iota(jnp.int32, (tm, tm), 0) > lax.broadcasted_iota(jnp.int32, (tm, tm), 1))
    ranks = _dot(before.astype(BF16), sel.astype(BF16)) + carry_ref[...]
    r1 = jnp.sum(jnp.where(lane == i1, ranks, 0.0), axis=-1, keepdims=True)
    r2 = jnp.sum(jnp.where(lane == i2, ranks, 0.0), axis=-1, keepdims=True)
    carry_ref[...] += jnp.sum(sel, axis=0, keepdims=True)
    count_ref[...] = carry_ref[...]

    rec = jnp.zeros(logits.shape, F32)
    for pos, val in ((RT_E1, i1.astype(F32)), (RT_E2, i2.astype(F32)), (RT_W1, w1), (RT_W2, w2),
                     (RT_R1, r1), (RT_R2, r2)):
        rec = jnp.where(lane == pos, val, rec)
    route_ref[...] = rec


def _router(x2, router, tm=1024):
    t = x2.shape[0]
    rp = jnp.zeros((D_MODEL, LANES), F32).at[:, 0:N_EXPERTS].set(router)
    rh = rp.astype(BF16)
    rm = (rp - rh.astype(F32)).astype(BF16)
    r3 = jnp.stack([rh, rm])
    return pl.pallas_call(
        _router_kernel,
        grid=(t // tm,),
        in_specs=[pl.BlockSpec((tm, D_MODEL), lambda i: (i, 0)),
                  pl.BlockSpec((2, D_MODEL, LANES), lambda i: (0, 0, 0))],
        out_specs=[pl.BlockSpec((tm, LANES), lambda i: (i, 0)),
                   pl.BlockSpec((1, LANES), lambda i: (0, 0))],
        out_shape=[jax.ShapeDtypeStruct((t, LANES), F32), jax.ShapeDtypeStruct((1, LANES), F32)],
        scratch_shapes=[pltpu.VMEM((1, LANES), F32)],
        compiler_params=_cparams("arbitrary"),
        name="router",
    )(x2, r3)


MOE_TM = 1024


def _route_plan(route, counts, t):
    cnt = counts[0, :N_EXPERTS].astype(jnp.int32)
    padded = ((cnt + MOE_TM - 1) // MOE_TM) * MOE_TM
    ends = jnp.cumsum(padded)
    offs = ends - padded
    experts = jnp.arange(N_EXPERTS, dtype=jnp.int32)

    def dest(e_lane, r_lane):
        e = route[:, e_lane].astype(jnp.int32)
        off = jnp.sum(jnp.where(e[:, None] == experts[None, :], offs[None, :], 0), axis=1)
        return off + route[:, r_lane].astype(jnp.int32)

    n_tiles = (2 * t) // MOE_TM + N_EXPERTS
    n_used = ends[-1] // MOE_TM
    tile = jnp.minimum(jnp.arange(n_tiles, dtype=jnp.int32), n_used - 1)
    tile_expert = jnp.sum((tile[:, None] * MOE_TM >= ends[None, :]).astype(jnp.int32), axis=1)
    group_end = jnp.sum(jnp.where(tile_expert[:, None] == experts[None, :], (offs + cnt)[None, :], 0), axis=1)
    tile_rows = jnp.clip(group_end - tile * MOE_TM, 0, MOE_TM)
    return dest(RT_E1, RT_R1), dest(RT_E2, RT_R2), tile_expert, n_used.reshape(1), tile_rows, n_tiles


def _dispatch_kernel(d1_ref, d2_ref, x_ref, zero_ref, xs_ref, sem):
    del zero_ref
    tm = x_ref.shape[0]
    base = pl.program_id(0) * tm

    def copies(r):
        src = x_ref.at[pl.ds(r, 1), :]
        return (pltpu.make_async_copy(src, xs_ref.at[pl.ds(d1_ref[base + r], 1), :], sem),
                pltpu.make_async_copy(src, xs_ref.at[pl.ds(d2_ref[base + r], 1), :], sem))

    def start(r, c):
        for cp in copies(r):
            cp.start()
        return c

    def wait(r, c):
        for cp in copies(r):
            cp.wait()
        return c

    lax.fori_loop(0, tm, start, 0, unroll=True)
    lax.fori_loop(0, tm, wait, 0, unroll=True)


def _dispatch(x2, d1, d2, n_rows, tm=512):
    t = x2.shape[0]
    zeros = jnp.zeros((n_rows, D_MODEL), F32)
    return pl.pallas_call(
        _dispatch_kernel,
        grid_spec=pltpu.PrefetchScalarGridSpec(
            num_scalar_prefetch=2,
            grid=(t // tm,),
            in_specs=[pl.BlockSpec((tm, D_MODEL), lambda i, d1, d2: (i, 0)),
                      pl.BlockSpec(memory_space=pl.ANY)],
            out_specs=pl.BlockSpec(memory_space=pl.ANY),
            scratch_shapes=[pltpu.SemaphoreType.DMA(())]),
        out_shape=jax.ShapeDtypeStruct((n_rows, D_MODEL), F32),
        input_output_aliases={3: 0},
        compiler_params=_cparams("arbitrary"),
        name="moe_dispatch",
    )(d1, d2, x2, zeros)


def _moe_ffn_kernel(te_ref, nu_ref, nr_ref, x_ref, wg_ref, wu_ref, wd_ref, o_ref, xb_ref):
    del te_ref
    i = pl.program_id(0)
    f = pl.program_id(1)
    used = i < nu_ref[0]

    @pl.when((f == 0) | jnp.logical_not(used))
    def _():
        o_ref[...] = jnp.zeros_like(o_ref)

    def swiglu_rows(rows):
        @pl.when(f == 0)
        def _():
            xb_ref[0:rows, :] = x_ref[0:rows, :].astype(BF16)

        o_ref[0:rows, :] += _swiglu_chunk(xb_ref[0:rows, :], wg_ref, wu_ref, wd_ref)

    quarters = pl.cdiv(nr_ref[i], MOE_TM // 4)
    for nq in range(1, 5):
        @pl.when(used & (quarters == nq))
        def _(nq=nq):
            swiglu_rows(nq * (MOE_TM // 4))


def _moe_ffn(xs, tile_expert, n_used, tile_rows, n_tiles, wg, wu, wd, tf=512):
    dff = wg.shape[2]
    nf = dff // tf
    row = lambda i, f, te, nu, nr: (jnp.minimum(i, nu[0] - 1), 0)
    fcl = lambda i, f, nu: jnp.where(i < nu[0], f, nf - 1)
    return pl.pallas_call(
        _moe_ffn_kernel,
        grid_spec=pltpu.PrefetchScalarGridSpec(
            num_scalar_prefetch=3,
            grid=(n_tiles, nf),
            in_specs=[pl.BlockSpec((MOE_TM, D_MODEL), row),
                      pl.BlockSpec((None, D_MODEL, tf), lambda i, f, te, nu, nr: (te[i], 0, fcl(i, f, nu))),
                      pl.BlockSpec((None, D_MODEL, tf), lambda i, f, te, nu, nr: (te[i], 0, fcl(i, f, nu))),
                      pl.BlockSpec((None, tf, D_MODEL), lambda i, f, te, nu, nr: (te[i], fcl(i, f, nu), 0))],
            out_specs=pl.BlockSpec((MOE_TM, D_MODEL), lambda i, f, te, nu, nr: (i, 0)),
            scratch_shapes=[pltpu.VMEM((MOE_TM, D_MODEL), BF16)]),
        out_shape=jax.ShapeDtypeStruct(xs.shape, F32),
        compiler_params=_cparams("arbitrary", "arbitrary"),
        name="moe_ffn",
    )(tile_expert, n_used, tile_rows, xs, wg, wu, wd)


def _combine_kernel(d1_ref, d2_ref, x_ref, route_ref, ys_ref, lnw_ref, lnb_ref, o_ref, b1_ref, b2_ref, sem):
    tm = x_ref.shape[0]
    i = pl.program_id(0)
    n = pl.num_programs(0)

    def copies(step, r):
        slot = step % 2
        base = step * tm
        return (pltpu.make_async_copy(ys_ref.at[pl.ds(d1_ref[base + r], 1), :],
                                      b1_ref.at[slot, pl.ds(r, 1), :], sem.at[slot]),
                pltpu.make_async_copy(ys_ref.at[pl.ds(d2_ref[base + r], 1), :],
                                      b2_ref.at[slot, pl.ds(r, 1), :], sem.at[slot]))

    def start_tile(step):
        def body(r, c):
            for cp in copies(step, r):
                cp.start()
            return c
        lax.fori_loop(0, tm, body, 0, unroll=True)

    def wait_tile(step):
        def body(r, c):
            for cp in copies(step, r):
                cp.wait()
            return c
        lax.fori_loop(0, tm, body, 0, unroll=True)

    @pl.when(i == 0)
    def _():
        start_tile(i)

    @pl.when(i + 1 < n)
    def _():
        start_tile(i + 1)

    wait_tile(i)
    slot = i % 2
    rt = route_ref[...]
    y = rt[:, RT_W1:RT_W1 + 1] * b1_ref[slot] + rt[:, RT_W2:RT_W2 + 1] * b2_ref[slot]
    o_ref[...] = _layer_norm(DN_ALPHA * x_ref[...] + y, lnw_ref[...], lnb_ref[...])


def _combine_ln(x2, route, ys, d1, d2, ln_w, ln_b, tm=256):
    t = x2.shape[0]
    const = lambda shape: pl.BlockSpec(shape, lambda i, d1, d2: (0,) * len(shape))
    return pl.pallas_call(
        _combine_kernel,
        grid_spec=pltpu.PrefetchScalarGridSpec(
            num_scalar_prefetch=2,
            grid=(t // tm,),
            in_specs=[pl.BlockSpec((tm, D_MODEL), lambda i, d1, d2: (i, 0)),
                      pl.BlockSpec((tm, LANES), lambda i, d1, d2: (i, 0)),
                      pl.BlockSpec(memory_space=pl.ANY),
                      const((1, D_MODEL)), const((1, D_MODEL))],
            out_specs=pl.BlockSpec((tm, D_MODEL), lambda i, d1, d2: (i, 0)),
            scratch_shapes=[pltpu.VMEM((2, tm, D_MODEL), F32), pltpu.VMEM((2, tm, D_MODEL), F32),
                            pltpu.SemaphoreType.DMA((2,))]),
        out_shape=jax.ShapeDtypeStruct((t, D_MODEL), F32),
        compiler_params=_cparams("arbitrary"),
        name="moe_combine_ln",
    )(d1, d2, x2, route, ys, ln_w.reshape(1, -1), ln_b.reshape(1, -1))


def _moe_ln(x2, router, wg, wu, wd, ln_w, ln_b):
    t = x2.shape[0]
    route, counts = _router(x2, router)
    d1, d2, tile_expert, n_used, tile_rows, n_tiles = _route_plan(route, counts, t)
    xs = _dispatch(x2, d1, d2, n_tiles * MOE_TM)
    ys = _moe_ffn(xs, tile_expert, n_used, tile_rows, n_tiles, wg, wu, wd)
    return _combine_ln(x2, route, ys, d1, d2, ln_w, ln_b)


def _layout_w_in(w):
    sizes = (REC_W, REC_W, REC_W, REC_W, MLA_Q_RANK, MLA_KV_RANK, MLA_ROPE,
             REC_W, REC_W, REC_W, REC_W, N_RHEADS, N_RHEADS)
    offs = np.concatenate([[0], np.cumsum(sizes)])
    part = lambda j: w[:, offs[j]:offs[j + 1]]
    z = lambda n: jnp.zeros((w.shape[0], n), w.dtype)
    misc = jnp.concatenate([part(11), part(12), z(MISC_KR - 2 * N_RHEADS), part(6),
                            z(LANES - MISC_KR - MLA_ROPE)], axis=1)
    cols = [part(0), part(1), part(2), part(3), part(7), part(8), part(9), part(10), part(4), part(5), misc]
    return jnp.concatenate(cols, axis=1).astype(BF16)


def _layout_mla(w_uq, w_ukv):
    r = w_uq.shape[0]
    uq = w_uq.reshape(r, MLA_HEADS, MLA_NOPE + MLA_ROPE)
    uq = jnp.pad(uq, ((0, 0), (0, 0), (0, LANES - MLA_NOPE - MLA_ROPE))).reshape(r, MLA_HEADS * LANES)
    r = w_ukv.shape[0]
    ukv = w_ukv.reshape(r, MLA_HEADS, MLA_NOPE + MLA_V)
    uk = jnp.pad(ukv[:, :, :MLA_NOPE], ((0, 0), (0, 0), (0, LANES - MLA_NOPE))).reshape(r, MLA_HEADS * LANES)
    uvt = ukv[:, :, MLA_NOPE:].reshape(r, MLA_W).T
    return uq.astype(BF16), uk.astype(BF16), uvt.astype(BF16)


def kernel(x, w_in, ret_gn_w, mla_q_norm_w, mla_w_uq, mla_kv_norm_w, mla_w_ukv, mlstm_conv_w, mlstm_conv_b,
           mlstm_b_i, mlstm_b_f, mlstm_gn_w, w_out, ln1_w, ln1_b, ffn_w_gate, ffn_w_up, ffn_w_down,
           moe_router, moe_w_gate, moe_w_up, moe_w_down, ln2_w, ln2_b):
    bsz, seq, d = x.shape
    t = bsz * seq
    cos_r, sin_r, cos_m, sin_m = _rope_tables(seq)
    x2 = x.reshape(t, d)
    for l in range(DEPTH):
        h2 = _inproj(x2, _layout_w_in(w_in[l]))
        h3 = h2.reshape(bsz, seq, D_IN_PAD)
        ret = _retention(h3, cos_r, sin_r, ret_gn_w[l]).reshape(t, REC_W)
        ml = _mlstm(h3, mlstm_conv_w[l], mlstm_conv_b[l], mlstm_b_i[l], mlstm_b_f[l],
                    mlstm_gn_w[l]).reshape(t, REC_W)
        wuq, wuk, wuvt = _layout_mla(mla_w_uq[l], mla_w_ukv[l])
        q, k, vt = _mla_prep(h2, seq, cos_m, sin_m, mla_q_norm_w[l], mla_kv_norm_w[l], wuq, wuk, wuvt)
        att = _mla_attn(q, k, vt, bsz, seq)
        x2 = _outproj_ln(ret, att, ml, x2, w_out[l].astype(BF16), ln1_w[l], ln1_b[l])
        if l % 2 == 0:
            j = l // 2
            x2 = _ffn_ln(x2, ffn_w_gate[j].astype(BF16), ffn_w_up[j].astype(BF16),
                         ffn_w_down[j].astype(BF16), ln2_w[l], ln2_b[l])
        else:
            j = l // 2
            x2 = _moe_ln(x2, moe_router[j], moe_w_gate[j], moe_w_up[j], moe_w_down[j], ln2_w[l], ln2_b[l])
    return x2.reshape(bsz, seq, d)
```

```python
import functools

import numpy as np
import jax
import jax.numpy as jnp
from jax import lax
from jax.experimental import pallas as pl
from jax.experimental.pallas import tpu as pltpu

F32 = jnp.float32
BF16 = jnp.bfloat16

D_MODEL = 1024
DEPTH = 2
ROPE_BASE = 10000.0
LN_EPS = 1e-5
NEG = -1e30

HEAD_DIM = 64
N_RHEADS = 4
REC_W = N_RHEADS * HEAD_DIM
MLA_HEADS = 8
MLA_NOPE = 64
MLA_ROPE = 32
MLA_V = 64
MLA_Q_RANK = 256
MLA_KV_RANK = 128
MLA_W = MLA_HEADS * MLA_V
MLSTM_CONV = 4
MASK_CHUNK = 64
REC_CHUNK = 256
REC_SEQS = 2
MLSTM_SEQS = 1
ATT_TILE = 256
ATT_GROUP = 4
ATT_VROWS = 80
N_EXPERTS = 8
LANES = 128

DN_ALPHA = (2 * DEPTH) ** 0.25
LOG2_E = 1.4426950408889634

COL_RET = 0
COL_MLSTM = 1024
COL_CQ = 2048
COL_CKV = 2304
COL_MISC = 2432
D_IN_PAD = 2560
MISC_KR = 64

VMEM_LIMIT = 56 * 1024 * 1024


def _cparams(*sem):
    return pltpu.CompilerParams(dimension_semantics=sem, vmem_limit_bytes=VMEM_LIMIT)


def _layer_norm(y, w, b):
    mu = jnp.mean(y, axis=-1, keepdims=True)
    d = y - mu
    var = jnp.mean(d * d, axis=-1, keepdims=True)
    return d * lax.rsqrt(var + LN_EPS) * w + b


def _silu(x):
    return x * (1.0 / (1.0 + jnp.exp(-x)))


def _sigmoid(x):
    return 1.0 / (1.0 + jnp.exp(-x))


def _split2(x):
    hi = x.astype(BF16)
    lo = (x - hi.astype(F32)).astype(BF16)
    return hi, lo


def _dot(a, b):
    return jnp.dot(a, b, preferred_element_type=F32)


def _dot_nt(a, b):
    return lax.dot_general(a, b, (((1,), (1,)), ((), ())), preferred_element_type=F32)


def _dot_exact_rhs(x, m_bf16):
    hi, lo = _split2(x)
    return _dot(hi, m_bf16) + _dot(lo, m_bf16)


def _swap_halves(x, half):
    n = x.shape[-1]
    lane = lax.broadcasted_iota(jnp.int32, x.shape, x.ndim - 1)
    first = (lane % (2 * half)) < half
    return jnp.where(first, pltpu.roll(x, n - half, x.ndim - 1), pltpu.roll(x, half, x.ndim - 1))


def _head_norm(x, avg_bf16, w):
    mu = _dot_exact_rhs(x, avg_bf16)
    d = x - mu
    var = _dot_exact_rhs(d * d, avg_bf16)
    return d * lax.rsqrt(var + LN_EPS) * w


def _expand_heads(v, lane_head):
    out = jnp.zeros((v.shape[0], REC_W), F32)
    for h in range(N_RHEADS):
        out = jnp.where(lane_head == h, v[:, h:h + 1], out)
    return out


def _rope_tables(seq):
    pos = np.arange(seq, dtype=np.float64)[:, None]
    half = HEAD_DIM // 2
    inv = ROPE_BASE ** (-np.arange(half, dtype=np.float64) / half)
    ang = pos * inv[None, :]
    cos_r = np.tile(np.concatenate([np.cos(ang), np.cos(ang)], -1), (1, N_RHEADS))
    sin_r = np.tile(np.concatenate([-np.sin(ang), np.sin(ang)], -1), (1, N_RHEADS))
    half = MLA_ROPE // 2
    inv = ROPE_BASE ** (-np.arange(half, dtype=np.float64) / half)
    ang = pos * inv[None, :]
    cos_m = np.ones((seq, LANES))
    sin_m = np.zeros((seq, LANES))
    cos_m[:, MISC_KR:MISC_KR + MLA_ROPE] = np.concatenate([np.cos(ang), np.cos(ang)], -1)
    sin_m[:, MISC_KR:MISC_KR + MLA_ROPE] = np.concatenate([-np.sin(ang), np.sin(ang)], -1)
    f = lambda a: jnp.asarray(a.astype(np.float32))
    return f(cos_r), f(sin_r), f(cos_m), f(sin_m)


def _retention_tables():
    L = REC_CHUNK
    log_gamma = np.log(1.0 - 2.0 ** (-5.0 - np.arange(N_RHEADS, dtype=np.float64)))
    idx = np.arange(L, dtype=np.float64)
    diff = idx[:, None] - idx[None, :]
    dmask = np.where(diff >= 0, np.exp(diff[None] * log_gamma[:, None, None]), 0.0)
    lane_lg = np.repeat(log_gamma, HEAD_DIM)[None, :]
    qw = np.exp((idx + 1.0)[:, None] * lane_lg)
    kw = np.exp((L - 1 - idx)[:, None] * lane_lg)
    cd = np.exp(L * lane_lg)
    f = lambda a: jnp.asarray(a.astype(np.float32))
    return f(dmask), f(qw), f(kw), f(cd)


def _head_avg_matrix():
    h = np.arange(REC_W) // HEAD_DIM
    return jnp.asarray((h[:, None] == h[None, :]).astype(np.float32) / HEAD_DIM, dtype=BF16)


def _inproj_kernel(x_ref, w_ref, o_ref):
    o_ref[...] = _dot(x_ref[...].astype(BF16), w_ref[...])


def _inproj(x2, w_bf16, tm=512):
    t, k = x2.shape
    n = w_bf16.shape[1]
    return pl.pallas_call(
        _inproj_kernel,
        grid=(t // tm,),
        in_specs=[pl.BlockSpec((tm, k), lambda i: (i, 0)),
                  pl.BlockSpec((k, n), lambda i: (0, 0))],
        out_specs=pl.BlockSpec((tm, n), lambda i: (i, 0)),
        out_shape=jax.ShapeDtypeStruct((t, n), F32),
        compiler_params=_cparams("parallel"),
        name="inproj",
    )(x2, w_bf16)


def _retention_kernel(h_ref, cos_ref, sin_ref, dmask_ref, qw_ref, kw_ref, cd_ref, avg_ref, gnw_ref,
                      o_ref, state_ref):
    L = REC_CHUNK
    nb = h_ref.shape[0]
    nc = h_ref.shape[1] // L
    seqs = range(nb)
    heads = range(N_RHEADS)
    lane_head = lax.broadcasted_iota(jnp.int32, (1, REC_W), 1) // HEAD_DIM
    row_head = lax.broadcasted_iota(jnp.int32, (REC_W, REC_W), 0) // HEAD_DIM
    col_head = lax.broadcasted_iota(jnp.int32, (REC_W, REC_W), 1) // HEAD_DIM
    block_diag = row_head == col_head
    state_ref[...] = jnp.zeros_like(state_ref)

    def chunk(c, carry):
        r0 = pl.multiple_of(c * L, L)
        rows = pl.ds(r0, L)
        cos = cos_ref[rows, :]
        sin = sin_ref[rows, :]
        qs, ks, kbs, vbs = [], [], [], []
        for s in seqs:
            q = h_ref[s, rows, 0:REC_W]
            k = h_ref[s, rows, REC_W:2 * REC_W]
            qs.append(q * cos + _swap_halves(q, HEAD_DIM // 2) * sin)
            k = (k * cos + _swap_halves(k, HEAD_DIM // 2) * sin) * (HEAD_DIM ** -0.5)
            ks.append(k)
            kbs.append(k.astype(BF16))
            vbs.append(h_ref[s, rows, 2 * REC_W:3 * REC_W].astype(BF16))
        scores = [[_dot_nt(jnp.where(lane_head == h, qs[s], 0.0).astype(BF16), kbs[s]) for h in heads]
                  for s in seqs]
        probs = [[(scores[s][h] * dmask_ref[h]).astype(BF16) for h in heads] for s in seqs]
        rets = []
        for s in seqs:
            inner = jnp.zeros((L, REC_W), F32)
            for h in heads:
                inner = jnp.where(lane_head == h, _dot(probs[s][h], vbs[s]), inner)
            state = state_ref[s]
            cross = _dot(qs[s].astype(BF16), state.astype(BF16)) * qw_ref[...]
            rets.append(inner + cross)
            kwt = (ks[s] * kw_ref[...]).T.astype(BF16)
            loc = _dot(kwt, vbs[s])
            state_ref[s] = cd_ref[...] * state + jnp.where(block_diag, loc, 0.0)
        for s in seqs:
            g = h_ref[s, rows, 3 * REC_W:4 * REC_W]
            o_ref[s, rows, :] = _silu(g) * _head_norm(rets[s], avg_ref[...], gnw_ref[...])
        return carry

    lax.fori_loop(0, nc, chunk, 0)


def _retention(h3, cos_r, sin_r, gn_w):
    b, s, _ = h3.shape
    dmask, qw, kw, cd = _retention_tables()
    avg = _head_avg_matrix()
    const = lambda shape: pl.BlockSpec(shape, lambda i: (0,) * len(shape))
    return pl.pallas_call(
        _retention_kernel,
        grid=(b // REC_SEQS,),
        in_specs=[pl.BlockSpec((REC_SEQS, s, 4 * REC_W), lambda i: (i, 0, COL_RET // (4 * REC_W))),
                  const((s, REC_W)), const((s, REC_W)),
                  const((N_RHEADS, REC_CHUNK, REC_CHUNK)), const((REC_CHUNK, REC_W)),
                  const((REC_CHUNK, REC_W)), const((1, REC_W)), const((REC_W, REC_W)), const((1, REC_W))],
        out_specs=pl.BlockSpec((REC_SEQS, s, REC_W), lambda i: (i, 0, 0)),
        out_shape=jax.ShapeDtypeStruct((b, s, REC_W), F32),
        scratch_shapes=[pltpu.VMEM((REC_SEQS, REC_W, REC_W), F32)],
        compiler_params=_cparams("parallel"),
        name="retention",
    )(h3, cos_r, sin_r, dmask, qw, kw, cd, avg, gn_w.reshape(1, REC_W))


def _mlstm_kernel(h_ref, misc_ref, convw_ref, convb_ref, bif_ref, avg_ref, gnw_ref,
                  o_ref, c_ref, n_ref, m_ref):
    L = REC_CHUNK
    nb = h_ref.shape[0]
    nc = h_ref.shape[1] // L
    seqs = range(nb)
    heads = range(N_RHEADS)
    lane_head = lax.broadcasted_iota(jnp.int32, (1, REC_W), 1) // HEAD_DIM
    row_head = lax.broadcasted_iota(jnp.int32, (REC_W, REC_W), 0) // HEAD_DIM
    col_head = lax.broadcasted_iota(jnp.int32, (REC_W, REC_W), 1) // HEAD_DIM
    block_diag = row_head == col_head
    n_mask = (lax.broadcasted_iota(jnp.int32, (REC_W, LANES), 0) // HEAD_DIM
              == lax.broadcasted_iota(jnp.int32, (REC_W, LANES), 1))
    lane128 = lax.broadcasted_iota(jnp.int32, (1, LANES), 1)
    gate_lanes = lane128 < N_RHEADS
    ri = lax.broadcasted_iota(jnp.int32, (L, L), 0)
    ci = lax.broadcasted_iota(jnp.int32, (L, L), 1)
    causal = ri >= ci
    tri = causal.astype(BF16)
    row8 = lax.broadcasted_iota(jnp.int32, (8, 2 * REC_W), 0)

    c_ref[...] = jnp.zeros_like(c_ref)
    n_ref[...] = jnp.zeros_like(n_ref)
    m_ref[...] = jnp.full(m_ref.shape, NEG, F32)

    def prepare(s, c, r0, rows):
        x = h_ref[s, rows, 0:2 * REC_W]
        prev = h_ref[s, pl.ds(pl.multiple_of(jnp.maximum(r0 - 8, 0), 8), 8), 0:2 * REC_W]
        prev = jnp.where(c > 0, prev, 0.0)
        acc = x * convw_ref[MLSTM_CONV - 1:MLSTM_CONV, :] + convb_ref[...]
        for sft in range(1, MLSTM_CONV):
            xs = pltpu.roll(x, sft, 0)
            head = jnp.where(row8 < sft, pltpu.roll(prev, sft, 0), xs[0:8, :])
            xs = jnp.concatenate([head, xs[8:, :]], axis=0)
            acc = acc + xs * convw_ref[MLSTM_CONV - 1 - sft:MLSTM_CONV - sft, :]
        qk = _silu(acc)
        q = qk[:, 0:REC_W]
        k = qk[:, REC_W:2 * REC_W] * (HEAD_DIM ** -0.5)

        gates = misc_ref[s, rows, :] + bif_ref[...]
        ic_c = jnp.where(gate_lanes, gates, 0.0)
        fg = pltpu.roll(gates, LANES - N_RHEADS, 1)
        lf_c = jnp.where(gate_lanes, jnp.minimum(fg, 0.0) - jnp.log1p(jnp.exp(-jnp.abs(fg))), 0.0)

        bcum = _dot_exact_rhs_left(tri, lf_c)
        gsum = bcum[L - 1:L, :]
        m_s = m_ref[s]
        a = gsum - bcum + ic_c
        qb = q.astype(BF16)
        c_state = c_ref[s]
        n_state = n_ref[s]
        return dict(q=q, k=k, qb=qb, kb=k.astype(BF16), vb=h_ref[s, rows, 2 * REC_W:3 * REC_W].astype(BF16),
                    bcum=bcum, gsum=gsum, m_s=m_s, a=a, amax=jnp.max(a, axis=0, keepdims=True),
                    rvec=(ic_c - bcum).T,
                    inter=bcum + m_s,
                    c_state=c_state, n_state=n_state,
                    q_c=_dot(qb, c_state.astype(BF16)),
                    q_n=_dot(qb, n_state.astype(BF16)))

    def decay(d, scores):
        cmats, stats = [], []
        for h in heads:
            log_d = jnp.where(causal, d["bcum"][:, h:h + 1] + d["rvec"][h:h + 1, :], NEG)
            inter_h = d["inter"][:, h:h + 1]
            m_t = jnp.maximum(inter_h, jnp.max(log_d, axis=-1, keepdims=True))
            cmat = scores[h] * jnp.exp(log_d - m_t)
            si = jnp.exp(inter_h - m_t)
            den = jnp.sum(cmat, axis=-1, keepdims=True) + si * d["q_n"][:, h:h + 1]
            cmats.append(cmat.astype(BF16))
            stats.append((si, jnp.maximum(jnp.abs(den), jnp.exp(-m_t))))
        return cmats, stats

    def cell(d, cmats, stats):
        hcell = jnp.zeros((L, REC_W), F32)
        for h in heads:
            si, denom = stats[h]
            num = _dot(cmats[h], d["vb"]) + si * d["q_c"]
            hcell = jnp.where(lane_head == h, num / denom, hcell)
        return hcell

    def carry_state(s, d):
        wa = jnp.exp(d["a"] - d["amax"])
        m_new = jnp.maximum(d["gsum"] + d["m_s"], d["amax"])
        sp = jnp.exp(d["gsum"] + d["m_s"] - m_new)
        sl = jnp.exp(d["amax"] - m_new)
        kwt = (d["k"] * _expand_heads(wa, lane_head)).T.astype(BF16)
        loc_c = _dot(kwt, d["vb"])
        loc_n = _dot(kwt, jnp.ones((L, LANES), BF16))
        c_ref[s] = (d["c_state"] * _expand_heads(sp, lane_head)
                    + jnp.where(block_diag, loc_c, 0.0) * _expand_heads(sl, lane_head))
        n_ref[s] = d["n_state"] * sp + jnp.where(n_mask, loc_n, 0.0) * sl
        m_ref[s] = m_new

    def chunk(c, carry):
        r0 = pl.multiple_of(c * L, L)
        rows = pl.ds(r0, L)
        ds = [prepare(s, c, r0, rows) for s in seqs]
        scores = [[_dot_nt(jnp.where(lane_head == h, d["q"], 0.0).astype(BF16), d["kb"]) for h in heads]
                  for d in ds]
        decs = [decay(d, sc) for d, sc in zip(ds, scores)]
        cells = [cell(d, *dec) for d, dec in zip(ds, decs)]
        for s in seqs:
            carry_state(s, ds[s])
        for s in seqs:
            og = h_ref[s, rows, 3 * REC_W:4 * REC_W]
            o_ref[s, rows, :] = _head_norm(_sigmoid(og) * cells[s], avg_ref[...], gnw_ref[...])
        return carry

    lax.fori_loop(0, nc, chunk, 0)


def _dot_exact_rhs_left(m_bf16, x):
    hi, lo = _split2(x)
    return _dot(m_bf16, hi) + _dot(m_bf16, lo)


def _mlstm(h3, conv_w, conv_b, b_i, b_f, gn_w):
    b, s, _ = h3.shape
    avg = _head_avg_matrix()
    bif = jnp.zeros((1, LANES), F32).at[0, 0:N_RHEADS].set(b_i).at[0, N_RHEADS:2 * N_RHEADS].set(b_f)
    const = lambda shape: pl.BlockSpec(shape, lambda i: (0,) * len(shape))
    return pl.pallas_call(
        _mlstm_kernel,
        grid=(b // MLSTM_SEQS,),
        in_specs=[pl.BlockSpec((MLSTM_SEQS, s, 4 * REC_W), lambda i: (i, 0, COL_MLSTM // (4 * REC_W))),
                  pl.BlockSpec((MLSTM_SEQS, s, LANES), lambda i: (i, 0, COL_MISC // LANES)),
                  const((MLSTM_CONV, 2 * REC_W)), const((1, 2 * REC_W)), const((1, LANES)),
                  const((REC_W, REC_W)), const((1, REC_W))],
        out_specs=pl.BlockSpec((MLSTM_SEQS, s, REC_W), lambda i: (i, 0, 0)),
        out_shape=jax.ShapeDtypeStruct((b, s, REC_W), F32),
        scratch_shapes=[pltpu.VMEM((MLSTM_SEQS, REC_W, REC_W), F32),
                        pltpu.VMEM((MLSTM_SEQS, REC_W, LANES), F32), pltpu.VMEM((MLSTM_SEQS, 1, LANES), F32)],
        compiler_params=_cparams("parallel"),
        name="mlstm",
    )(h3, h3, conv_w, conv_b.reshape(1, -1), bif, avg, gn_w.reshape(1, REC_W))


def _mla_prep_kernel(cq_ref, ckv_ref, cos_ref, sin_ref, qnw_ref, kvnw_ref, wuq_ref, wuk_ref, wuvt_ref,
                     q_ref, k_ref, vt_ref):
    scale = (MLA_NOPE + MLA_ROPE) ** -0.5 * LOG2_E
    cos = cos_ref[...]
    sin = sin_ref[...]
    cos_all = jnp.concatenate([cos] * MLA_HEADS, axis=1)
    sin_all = jnp.concatenate([sin] * MLA_HEADS, axis=1)

    cq = cq_ref[...]
    cqn = cq * lax.rsqrt(jnp.mean(cq * cq, axis=-1, keepdims=True) + LN_EPS) * qnw_ref[...]
    q = _dot(cqn.astype(BF16), wuq_ref[...])
    q = q * cos_all + _swap_halves(q, MLA_ROPE // 2) * sin_all
    q_ref[...] = (q * scale).astype(BF16)

    ckv = ckv_ref[:, 0:MLA_KV_RANK]
    kvn = ckv * lax.rsqrt(jnp.mean(ckv * ckv, axis=-1, keepdims=True) + LN_EPS) * kvnw_ref[...]
    kvb = kvn.astype(BF16)
    misc = ckv_ref[:, MLA_KV_RANK:MLA_KV_RANK + LANES]
    lane = lax.broadcasted_iota(jnp.int32, (1, LANES), 1)
    kr = misc * cos + _swap_halves(misc, MLA_ROPE // 2) * sin
    kr = jnp.where((lane >= MISC_KR) & (lane < MISC_KR + MLA_ROPE), kr, 0.0)
    k = _dot(kvb, wuk_ref[...]) + jnp.concatenate([kr] * MLA_HEADS, axis=1)
    k_ref[...] = k.astype(BF16)
    vt = _dot_nt(wuvt_ref[...], kvb).astype(BF16)
    ones = jnp.ones((ATT_VROWS - MLA_V, ATT_TILE), BF16)
    for j in range(vt_ref.shape[0]):
        for h in range(MLA_HEADS):
            vt_ref[j, h * ATT_VROWS:h * ATT_VROWS + MLA_V, :] = vt[h * MLA_V:(h + 1) * MLA_V,
                                                                   j * ATT_TILE:(j + 1) * ATT_TILE]
            vt_ref[j, h * ATT_VROWS + MLA_V:(h + 1) * ATT_VROWS, :] = ones


def _mla_prep(h2, seq, cos_m, sin_m, qn_w, kvn_w, wuq, wuk, wuvt, tm=512):
    t = h2.shape[0]
    nblk = seq // tm
    hw = MLA_HEADS * LANES
    const = lambda shape: pl.BlockSpec(shape, lambda i: (0,) * len(shape))
    return pl.pallas_call(
        _mla_prep_kernel,
        grid=(t // tm,),
        in_specs=[pl.BlockSpec((tm, MLA_Q_RANK), lambda i: (i, COL_CQ // MLA_Q_RANK)),
                  pl.BlockSpec((tm, 2 * LANES), lambda i: (i, COL_CKV // (2 * LANES))),
                  pl.BlockSpec((tm, LANES), lambda i: (i % nblk, 0)),
                  pl.BlockSpec((tm, LANES), lambda i: (i % nblk, 0)),
                  const((1, MLA_Q_RANK)), const((1, MLA_KV_RANK)),
                  const((MLA_Q_RANK, hw)), const((MLA_KV_RANK, hw)), const((MLA_W, MLA_KV_RANK))],
        out_specs=[pl.BlockSpec((tm, hw), lambda i: (i, 0)), pl.BlockSpec((tm, hw), lambda i: (i, 0)),
                   pl.BlockSpec((tm // ATT_TILE, MLA_HEADS * ATT_VROWS, ATT_TILE), lambda i: (i, 0, 0))],
        out_shape=[jax.ShapeDtypeStruct((t, hw), BF16), jax.ShapeDtypeStruct((t, hw), BF16),
                   jax.ShapeDtypeStruct((t // ATT_TILE, MLA_HEADS * ATT_VROWS, ATT_TILE), BF16)],
        compiler_params=_cparams("parallel"),
        name="mla_prep",
    )(h2, h2, cos_m, sin_m, qn_w.reshape(1, -1), kvn_w.reshape(1, -1), wuq, wuk, wuvt)


def _mla_attn_kernel(q_ref, k_ref, vt_ref, o_ref):
    tq = ATT_TILE
    qi = pl.program_id(1)
    key = lax.broadcasted_iota(jnp.int32, (tq, tq), 0)
    qry = lax.broadcasted_iota(jnp.int32, (tq, tq), 1)
    diag_ok = key < (qry // MASK_CHUNK + 1) * MASK_CHUNK

    def step(kt, carry, masked):
        rows = pl.ds(pl.multiple_of(kt * tq, tq), tq)
        sts = []
        for j in range(MLA_HEADS):
            cols = slice(j * LANES, (j + 1) * LANES)
            sts.append(_dot_nt(k_ref[rows, cols], q_ref[:, cols]))
        new = []
        for g0 in range(0, MLA_HEADS, ATT_GROUP):
            pts = []
            for j in range(g0, g0 + ATT_GROUP):
                m = carry[j][0]
                st = sts[j]
                if masked:
                    st = jnp.where(diag_ok, st, NEG)
                m_new = jnp.maximum(m, jnp.max(st, axis=0, keepdims=True))
                pts.append((m_new, jnp.exp2(m - m_new), jnp.exp2(st - m_new).astype(BF16)))
            for j, (m_new, alpha, pt) in zip(range(g0, g0 + ATT_GROUP), pts):
                acc = alpha * carry[j][1] + _dot(vt_ref[kt, j * ATT_VROWS:(j + 1) * ATT_VROWS, :], pt)
                new.append((m_new, acc))
        return tuple(new)

    init = tuple((jnp.full((1, tq), NEG, F32), jnp.zeros((ATT_VROWS, tq), F32)) for _ in range(MLA_HEADS))
    carry = lax.fori_loop(0, qi, lambda kt, c: step(kt, c, False), init)
    carry = step(qi, carry, True)
    outs = [acc[0:MLA_V, :] / acc[MLA_V:MLA_V + 1, :] for _, acc in carry]
    for p in range(MLA_HEADS // 2):
        o_ref[:, p * LANES:(p + 1) * LANES] = jnp.concatenate([outs[2 * p], outs[2 * p + 1]], axis=0).T


def _mla_attn(q, k, vt, batch, seq):
    t = q.shape[0]
    nq = seq // ATT_TILE
    hw = MLA_HEADS * LANES
    return pl.pallas_call(
        _mla_attn_kernel,
        grid=(batch, nq),
        in_specs=[pl.BlockSpec((ATT_TILE, hw), lambda b, i: (b * nq + i, 0)),
                  pl.BlockSpec((seq, hw), lambda b, i: (b, 0)),
                  pl.BlockSpec((nq, MLA_HEADS * ATT_VROWS, ATT_TILE), lambda b, i: (b, 0, 0))],
        out_specs=pl.BlockSpec((ATT_TILE, MLA_W), lambda b, i: (b * nq + i, 0)),
        out_shape=jax.ShapeDtypeStruct((t, MLA_W), F32),
        compiler_params=_cparams("parallel", "arbitrary"),
        name="mla_attn",
    )(q, k, vt)


def _outproj_kernel(ret_ref, att_ref, ml_ref, x_ref, w_ref, lnw_ref, lnb_ref, o_ref):
    mix = (_dot(ret_ref[...].astype(BF16), w_ref[0:REC_W, :])
           + _dot(att_ref[...].astype(BF16), w_ref[REC_W:REC_W + MLA_W, :])
           + _dot(ml_ref[...].astype(BF16), w_ref[REC_W + MLA_W:, :]))
    o_ref[...] = _layer_norm(DN_ALPHA * x_ref[...] + mix, lnw_ref[...], lnb_ref[...])


def _outproj_ln(ret, att, ml, x2, w_bf16, ln_w, ln_b, tm=512):
    t = x2.shape[0]
    const = lambda shape: pl.BlockSpec(shape, lambda i: (0,) * len(shape))
    row = lambda w: pl.BlockSpec((tm, w), lambda i: (i, 0))
    return pl.pallas_call(
        _outproj_kernel,
        grid=(t // tm,),
        in_specs=[row(REC_W), row(MLA_W), row(REC_W), row(D_MODEL),
                  const((D_MODEL, D_MODEL)), const((1, D_MODEL)), const((1, D_MODEL))],
        out_specs=row(D_MODEL),
        out_shape=jax.ShapeDtypeStruct((t, D_MODEL), F32),
        compiler_params=_cparams("parallel"),
        name="outproj_ln",
    )(ret, att, ml, x2, w_bf16, ln_w.reshape(1, -1), ln_b.reshape(1, -1))


MXU_COLS = 256
SWIGLU_GROUP = 4


def _swiglu_chunk(xb, wg_ref, wu_ref, wd_ref):
    tf = wg_ref.shape[1]
    parts = [slice(s, min(s + MXU_COLS, tf)) for s in range(0, tf, MXU_COLS)]
    out = None
    for g0 in range(0, len(parts), SWIGLU_GROUP):
        group = parts[g0:g0 + SWIGLU_GROUP]
        gu = [(_dot(xb, wg_ref[:, p].astype(BF16)), _dot(xb, wu_ref[:, p].astype(BF16))) for p in group]
        hs = [(_silu(g) * u).astype(BF16) for g, u in gu]
        for h, p in zip(hs, group):
            y = _dot(h, wd_ref[p, :].astype(BF16))
            out = y if out is None else out + y
    return out


def _ffn_kernel(x_ref, wg_ref, wu_ref, wd_ref, lnw_ref, lnb_ref, o_ref):
    x = x_ref[...]
    y = _swiglu_chunk(x.astype(BF16), wg_ref, wu_ref, wd_ref)
    o_ref[...] = _layer_norm(DN_ALPHA * x + y, lnw_ref[...], lnb_ref[...])


def _ffn_ln(x2, wg, wu, wd, ln_w, ln_b, tm=1024):
    t = x2.shape[0]
    dff = wg.shape[1]
    once = lambda shape: pl.BlockSpec(shape, lambda i: (0,) * len(shape), pipeline_mode=pl.Buffered(1))
    return pl.pallas_call(
        _ffn_kernel,
        grid=(t // tm,),
        in_specs=[pl.BlockSpec((tm, D_MODEL), lambda i: (i, 0)),
                  once((D_MODEL, dff)), once((D_MODEL, dff)), once((dff, D_MODEL)),
                  once((1, D_MODEL)), once((1, D_MODEL))],
        out_specs=pl.BlockSpec((tm, D_MODEL), lambda i: (i, 0)),
        out_shape=jax.ShapeDtypeStruct((t, D_MODEL), F32),
        compiler_params=_cparams("parallel"),
        name="ffn_ln",
    )(x2, wg, wu, wd, ln_w.reshape(1, -1), ln_b.reshape(1, -1))


RT_E1, RT_E2, RT_W1, RT_W2, RT_R1, RT_R2 = 0, 1, 2, 3, 4, 5


def _router_kernel(x_ref, r_ref, route_ref, count_ref, carry_ref):
    @pl.when(pl.program_id(0) == 0)
    def _():
        carry_ref[...] = jnp.zeros_like(carry_ref)

    x = x_ref[...]
    xh = x.astype(BF16)
    xm = (x - xh.astype(F32)).astype(BF16)
    both = _dot(xh, r_ref[...])
    logits = both[:, 0:LANES] + (both[:, LANES:2 * LANES] + _dot(xm, r_ref[:, 0:LANES]))
    tm = logits.shape[0]
    lane = lax.broadcasted_iota(jnp.int32, logits.shape, 1)
    lg = jnp.where(lane < N_EXPERTS, logits, -jnp.inf)
    m1 = jnp.max(lg, axis=-1, keepdims=True)
    i1 = jnp.min(jnp.where(lg == m1, lane, LANES), axis=-1, keepdims=True)
    lg2 = jnp.where(lane == i1, -jnp.inf, lg)
    m2 = jnp.max(lg2, axis=-1, keepdims=True)
    i2 = jnp.min(jnp.where(lg2 == m2, lane, LANES), axis=-1, keepdims=True)
    e2 = jnp.exp(m2 - m1)
    w1 = 1.0 / (1.0 + e2)
    w2 = e2 / (1.0 + e2)

    sel = ((lane == i1) | (lane == i2)).astype(F32)
    before = (lax.broadcasted_iota(jnp.int32, (tm, tm), 0) > lax.broadcasted_iota(jnp.int32, (tm, tm), 1))
    ranks = _dot(before.astype(BF16), sel.astype(BF16)) + carry_ref[...]
    r1 = jnp.sum(jnp.where(lane == i1, ranks, 0.0), axis=-1, keepdims=True)
    r2 = jnp.sum(jnp.where(lane == i2, ranks, 0.0), axis=-1, keepdims=True)
    carry_ref[...] += jnp.sum(sel, axis=0, keepdims=True)
    count_ref[...] = carry_ref[...]

    rec = jnp.zeros(logits.shape, F32)
    for pos, val in ((RT_E1, i1.astype(F32)), (RT_E2, i2.astype(F32)), (RT_W1, w1), (RT_W2, w2),
                     (RT_R1, r1), (RT_R2, r2)):
        rec = jnp.where(lane == pos, val, rec)
    route_ref[...] = rec


def _router(x2, router, tm=1024):
    t = x2.shape[0]
    rp = jnp.zeros((D_MODEL, LANES), F32).at[:, 0:N_EXPERTS].set(router)
    rh = rp.astype(BF16)
    rm = (rp - rh.astype(F32)).astype(BF16)
    r3 = jnp.concatenate([rh, rm], axis=1)
    return pl.pallas_call(
        _router_kernel,
        grid=(t // tm,),
        in_specs=[pl.BlockSpec((tm, D_MODEL), lambda i: (i, 0)),
                  pl.BlockSpec((D_MODEL, 2 * LANES), lambda i: (0, 0))],
        out_specs=[pl.BlockSpec((tm, LANES), lambda i: (i, 0)),
                   pl.BlockSpec((1, LANES), lambda i: (0, 0))],
        out_shape=[jax.ShapeDtypeStruct((t, LANES), F32), jax.ShapeDtypeStruct((1, LANES), F32)],
        scratch_shapes=[pltpu.VMEM((1, LANES), F32)],
        compiler_params=_cparams("arbitrary"),
        name="router",
    )(x2, r3)


MOE_TM = 1024


def _route_plan(route, counts, t):
    cnt = counts[0, :N_EXPERTS].astype(jnp.int32)
    padded = ((cnt + MOE_TM - 1) // MOE_TM) * MOE_TM
    ends = jnp.cumsum(padded)
    offs = ends - padded
    experts = jnp.arange(N_EXPERTS, dtype=jnp.int32)

    def dest(e_lane, r_lane):
        e = route[:, e_lane].astype(jnp.int32)
        off = jnp.sum(jnp.where(e[:, None] == experts[None, :], offs[None, :], 0), axis=1)
        return off + route[:, r_lane].astype(jnp.int32)

    n_tiles = (2 * t) // MOE_TM + N_EXPERTS
    n_used = ends[-1] // MOE_TM
    tile = jnp.minimum(jnp.arange(n_tiles, dtype=jnp.int32), n_used - 1)
    tile_expert = jnp.sum((tile[:, None] * MOE_TM >= ends[None, :]).astype(jnp.int32), axis=1)
    group_end = jnp.sum(jnp.where(tile_expert[:, None] == experts[None, :], (offs + cnt)[None, :], 0), axis=1)
    tile_rows = jnp.clip(group_end - tile * MOE_TM, 0, MOE_TM)
    return dest(RT_E1, RT_R1), dest(RT_E2, RT_R2), tile_expert, n_used.reshape(1), tile_rows, n_tiles


def _dispatch_kernel(d1_ref, d2_ref, x_ref, zero_ref, xs_ref, sem):
    del zero_ref
    tm = x_ref.shape[0]
    base = pl.program_id(0) * tm

    def copies(r):
        src = x_ref.at[pl.ds(r, 1), :]
        return (pltpu.make_async_copy(src, xs_ref.at[pl.ds(d1_ref[base + r], 1), :], sem),
                pltpu.make_async_copy(src, xs_ref.at[pl.ds(d2_ref[base + r], 1), :], sem))

    for r in range(tm):
        for cp in copies(r):
            cp.start(priority=r % 2)
    for r in range(tm):
        for cp in copies(r):
            cp.wait()


def _dispatch(x2, d1, d2, n_rows, tm=512):
    t = x2.shape[0]
    zeros = jnp.zeros((n_rows, D_MODEL), F32)
    return pl.pallas_call(
        _dispatch_kernel,
        grid_spec=pltpu.PrefetchScalarGridSpec(
            num_scalar_prefetch=2,
            grid=(t // tm,),
            in_specs=[pl.BlockSpec((tm, D_MODEL), lambda i, d1, d2: (i, 0)),
                      pl.BlockSpec(memory_space=pl.ANY)],
            out_specs=pl.BlockSpec(memory_space=pl.ANY),
            scratch_shapes=[pltpu.SemaphoreType.DMA(())]),
        out_shape=jax.ShapeDtypeStruct((n_rows, D_MODEL), F32),
        input_output_aliases={3: 0},
        compiler_params=_cparams("arbitrary"),
        name="moe_dispatch",
    )(d1, d2, x2, zeros)


def _moe_ffn_kernel(te_ref, nu_ref, nr_ref, x_ref, wg_ref, wu_ref, wd_ref, o_ref, xb_ref):
    del te_ref
    i = pl.program_id(0)
    f = pl.program_id(1)
    used = i < nu_ref[0]

    @pl.when((f == 0) | jnp.logical_not(used))
    def _():
        o_ref[...] = jnp.zeros_like(o_ref)

    def swiglu_rows(rows):
        @pl.when(f == 0)
        def _():
            xb_ref[0:rows, :] = x_ref[0:rows, :].astype(BF16)

        o_ref[0:rows, :] += _swiglu_chunk(xb_ref[0:rows, :], wg_ref, wu_ref, wd_ref)

    quarters = pl.cdiv(nr_ref[i], MOE_TM // 4)
    for nq in range(1, 5):
        @pl.when(used & (quarters == nq))
        def _(nq=nq):
            swiglu_rows(nq * (MOE_TM // 4))


def _moe_ffn(xs, tile_expert, n_used, tile_rows, n_tiles, wg, wu, wd, tf=512):
    dff = wg.shape[2]
    nf = dff // tf
    row = lambda i, f, te, nu, nr: (jnp.minimum(i, nu[0] - 1), 0)
    fcl = lambda i, f, nu: jnp.where(i < nu[0], f, nf - 1)
    return pl.pallas_call(
        _moe_ffn_kernel,
        grid_spec=pltpu.PrefetchScalarGridSpec(
            num_scalar_prefetch=3,
            grid=(n_tiles, nf),
            in_specs=[pl.BlockSpec((MOE_TM, D_MODEL), row),
                      pl.BlockSpec((None, D_MODEL, tf), lambda i, f, te, nu, nr: (te[i], 0, fcl(i, f, nu))),
                      pl.BlockSpec((None, D_MODEL, tf), lambda i, f, te, nu, nr: (te[i], 0, fcl(i, f, nu))),
                      pl.BlockSpec((None, tf, D_MODEL), lambda i, f, te, nu, nr: (te[i], fcl(i, f, nu), 0))],
            out_specs=pl.BlockSpec((MOE_TM, D_MODEL), lambda i, f, te, nu, nr: (i, 0)),
            scratch_shapes=[pltpu.VMEM((MOE_TM, D_MODEL), BF16)]),
        out_shape=jax.ShapeDtypeStruct(xs.shape, F32),
        compiler_params=_cparams("arbitrary", "arbitrary"),
        name="moe_ffn",
    )(tile_expert, n_used, tile_rows, xs, wg, wu, wd)


def _combine_kernel(d1_ref, d2_ref, x_ref, route_ref, ys_ref, lnw_ref, lnb_ref, o_ref, b1_ref, b2_ref, sem):
    tm = x_ref.shape[0]
    i = pl.program_id(0)
    n = pl.num_programs(0)

    def copies(step, r):
        slot = step % 2
        base = step * tm
        return (pltpu.make_async_copy(ys_ref.at[pl.ds(d1_ref[base + r], 1), :],
                                      b1_ref.at[slot, pl.ds(r, 1), :], sem.at[slot]),
                pltpu.make_async_copy(ys_ref.at[pl.ds(d2_ref[base + r], 1), :],
                                      b2_ref.at[slot, pl.ds(r, 1), :], sem.at[slot]))

    def start_tile(step):
        for r in range(tm):
            for cp in copies(step, r):
                cp.start(priority=r % 2)

    def wait_tile(step):
        for r in range(tm):
            for cp in copies(step, r):
                cp.wait()

    @pl.when(i == 0)
    def _():
        start_tile(i)

    @pl.when(i + 1 < n)
    def _():
        start_tile(i + 1)

    wait_tile(i)
    slot = i % 2
    rt = route_ref[...]
    y = rt[:, RT_W1:RT_W1 + 1] * b1_ref[slot] + rt[:, RT_W2:RT_W2 + 1] * b2_ref[slot]
    o_ref[...] = _layer_norm(DN_ALPHA * x_ref[...] + y, lnw_ref[...], lnb_ref[...])


def _combine_ln(x2, route, ys, d1, d2, ln_w, ln_b, tm=256):
    t = x2.shape[0]
    const = lambda shape: pl.BlockSpec(shape, lambda i, d1, d2: (0,) * len(shape))
    return pl.pallas_call(
        _combine_kernel,
        grid_spec=pltpu.PrefetchScalarGridSpec(
            num_scalar_prefetch=2,
            grid=(t // tm,),
            in_specs=[pl.BlockSpec((tm, D_MODEL), lambda i, d1, d2: (i, 0)),
                      pl.BlockSpec((tm, LANES), lambda i, d1, d2: (i, 0)),
                      pl.BlockSpec(memory_space=pl.ANY),
                      const((1, D_MODEL)), const((1, D_MODEL))],
            out_specs=pl.BlockSpec((tm, D_MODEL), lambda i, d1, d2: (i, 0)),
            scratch_shapes=[pltpu.VMEM((2, tm, D_MODEL), F32), pltpu.VMEM((2, tm, D_MODEL), F32),
                            pltpu.SemaphoreType.DMA((2,))]),
        out_shape=jax.ShapeDtypeStruct((t, D_MODEL), F32),
        compiler_params=_cparams("arbitrary"),
        name="moe_combine_ln",
    )(d1, d2, x2, route, ys, ln_w.reshape(1, -1), ln_b.reshape(1, -1))


def _moe_ln(x2, router, wg, wu, wd, ln_w, ln_b):
    t = x2.shape[0]
    route, counts = _router(x2, router)
    d1, d2, tile_expert, n_used, tile_rows, n_tiles = _route_plan(route, counts, t)
    xs = _dispatch(x2, d1, d2, n_tiles * MOE_TM)
    ys = _moe_ffn(xs, tile_expert, n_used, tile_rows, n_tiles, wg, wu, wd)
    return _combine_ln(x2, route, ys, d1, d2, ln_w, ln_b)


def _layout_w_in(w):
    sizes = (REC_W, REC_W, REC_W, REC_W, MLA_Q_RANK, MLA_KV_RANK, MLA_ROPE,
             REC_W, REC_W, REC_W, REC_W, N_RHEADS, N_RHEADS)
    offs = np.concatenate([[0], np.cumsum(sizes)])
    part = lambda j: w[:, offs[j]:offs[j + 1]]
    z = lambda n: jnp.zeros((w.shape[0], n), w.dtype)
    misc = jnp.concatenate([part(11), part(12), z(MISC_KR - 2 * N_RHEADS), part(6),
                            z(LANES - MISC_KR - MLA_ROPE)], axis=1)
    cols = [part(0), part(1), part(2), part(3), part(7), part(8), part(9), part(10), part(4), part(5), misc]
    return jnp.concatenate(cols, axis=1).astype(BF16)


def _layout_mla(w_uq, w_ukv):
    r = w_uq.shape[0]
    uq = w_uq.reshape(r, MLA_HEADS, MLA_NOPE + MLA_ROPE)
    uq = jnp.pad(uq, ((0, 0), (0, 0), (0, LANES - MLA_NOPE - MLA_ROPE))).reshape(r, MLA_HEADS * LANES)
    r = w_ukv.shape[0]
    ukv = w_ukv.reshape(r, MLA_HEADS, MLA_NOPE + MLA_V)
    uk = jnp.pad(ukv[:, :, :MLA_NOPE], ((0, 0), (0, 0), (0, LANES - MLA_NOPE))).reshape(r, MLA_HEADS * LANES)
    uvt = ukv[:, :, MLA_NOPE:].reshape(r, MLA_W).T
    return uq.astype(BF16), uk.astype(BF16), uvt.astype(BF16)


def kernel(x, w_in, ret_gn_w, mla_q_norm_w, mla_w_uq, mla_kv_norm_w, mla_w_ukv, mlstm_conv_w, mlstm_conv_b,
           mlstm_b_i, mlstm_b_f, mlstm_gn_w, w_out, ln1_w, ln1_b, ffn_w_gate, ffn_w_up, ffn_w_down,
           moe_router, moe_w_gate, moe_w_up, moe_w_down, ln2_w, ln2_b):
    bsz, seq, d = x.shape
    t = bsz * seq
    cos_r, sin_r, cos_m, sin_m = _rope_tables(seq)
    x2 = x.reshape(t, d)
    for l in range(DEPTH):
        h2 = _inproj(x2, _layout_w_in(w_in[l]))
        h3 = h2.reshape(bsz, seq, D_IN_PAD)
        ret = _retention(h3, cos_r, sin_r, ret_gn_w[l]).reshape(t, REC_W)
        ml = _mlstm(h3, mlstm_conv_w[l], mlstm_conv_b[l], mlstm_b_i[l], mlstm_b_f[l],
                    mlstm_gn_w[l]).reshape(t, REC_W)
        wuq, wuk, wuvt = _layout_mla(mla_w_uq[l], mla_w_ukv[l])
        q, k, vt = _mla_prep(h2, seq, cos_m, sin_m, mla_q_norm_w[l], mla_kv_norm_w[l], wuq, wuk, wuvt)
        att = _mla_attn(q, k, vt, bsz, seq)
        x2 = _outproj_ln(ret, att, ml, x2, w_out[l].astype(BF16), ln1_w[l], ln1_b[l])
        if l % 2 == 0:
            j = l // 2
            x2 = _ffn_ln(x2, ffn_w_gate[j].astype(BF16), ffn_w_up[j].astype(BF16),
                         ffn_w_down[j].astype(BF16), ln2_w[l], ln2_b[l])
        else:
            j = l // 2
            x2 = _moe_ln(x2, moe_router[j], moe_w_gate[j], moe_w_up[j], moe_w_down[j], ln2_w[l], ln2_b[l])
    return x2.reshape(bsz, seq, d)
```

```python
import functools

import numpy as np
import jax
import jax.numpy as jnp
from jax import lax
from jax.experimental import pallas as pl
from jax.experimental.pallas import tpu as pltpu

F32 = jnp.float32
BF16 = jnp.bfloat16

D_MODEL = 1024
DEPTH = 2
ROPE_BASE = 10000.0
LN_EPS = 1e-5
NEG = -1e30

HEAD_DIM = 64
N_RHEADS = 4
REC_W = N_RHEADS * HEAD_DIM
MLA_HEADS = 8
MLA_NOPE = 64
MLA_ROPE = 32
MLA_V = 64
MLA_Q_RANK = 256
MLA_KV_RANK = 128
MLA_W = MLA_HEADS * MLA_V
MLSTM_CONV = 4
MASK_CHUNK = 64
REC_CHUNK = 256
REC_SEQS = 2
MLSTM_SEQS = 1
ATT_TILE = 256
ATT_GROUP = 4
ATT_VROWS = 80
N_EXPERTS = 8
LANES = 128

DN_ALPHA = (2 * DEPTH) ** 0.25
LOG2_E = 1.4426950408889634

COL_RET = 0
COL_MLSTM = 1024
COL_CQ = 2048
COL_CKV = 2304
COL_MISC = 2432
D_IN_PAD = 2560
MISC_KR = 64

VMEM_LIMIT = 56 * 1024 * 1024


def _cparams(*sem):
    return pltpu.CompilerParams(dimension_semantics=sem, vmem_limit_bytes=VMEM_LIMIT)


def _layer_norm(y, w, b):
    mu = jnp.mean(y, axis=-1, keepdims=True)
    d = y - mu
    var = jnp.mean(d * d, axis=-1, keepdims=True)
    return d * lax.rsqrt(var + LN_EPS) * w + b


def _silu(x):
    return x * (1.0 / (1.0 + jnp.exp(-x)))


def _sigmoid(x):
    return 1.0 / (1.0 + jnp.exp(-x))


def _split2(x):
    hi = x.astype(BF16)
    lo = (x - hi.astype(F32)).astype(BF16)
    return hi, lo


def _dot(a, b):
    return jnp.dot(a, b, preferred_element_type=F32)


def _dot_nt(a, b):
    return lax.dot_general(a, b, (((1,), (1,)), ((), ())), preferred_element_type=F32)


def _dot_exact_rhs(x, m_bf16):
    hi, lo = _split2(x)
    return _dot(hi, m_bf16) + _dot(lo, m_bf16)


def _swap_halves(x, half):
    n = x.shape[-1]
    lane = lax.broadcasted_iota(jnp.int32, x.shape, x.ndim - 1)
    first = (lane % (2 * half)) < half
    return jnp.where(first, pltpu.roll(x, n - half, x.ndim - 1), pltpu.roll(x, half, x.ndim - 1))


def _head_norm(x, avg_bf16, w):
    mu = _dot_exact_rhs(x, avg_bf16)
    d = x - mu
    var = _dot_exact_rhs(d * d, avg_bf16)
    return d * lax.rsqrt(var + LN_EPS) * w


def _expand_heads(v, lane_head):
    out = jnp.zeros((v.shape[0], REC_W), F32)
    for h in range(N_RHEADS):
        out = jnp.where(lane_head == h, v[:, h:h + 1], out)
    return out


def _rope_tables(seq):
    pos = np.arange(seq, dtype=np.float64)[:, None]
    half = HEAD_DIM // 2
    inv = ROPE_BASE ** (-np.arange(half, dtype=np.float64) / half)
    ang = pos * inv[None, :]
    cos_r = np.tile(np.concatenate([np.cos(ang), np.cos(ang)], -1), (1, N_RHEADS))
    sin_r = np.tile(np.concatenate([-np.sin(ang), np.sin(ang)], -1), (1, N_RHEADS))
    half = MLA_ROPE // 2
    inv = ROPE_BASE ** (-np.arange(half, dtype=np.float64) / half)
    ang = pos * inv[None, :]
    cos_m = np.ones((seq, LANES))
    sin_m = np.zeros((seq, LANES))
    cos_m[:, MISC_KR:MISC_KR + MLA_ROPE] = np.concatenate([np.cos(ang), np.cos(ang)], -1)
    sin_m[:, MISC_KR:MISC_KR + MLA_ROPE] = np.concatenate([-np.sin(ang), np.sin(ang)], -1)
    f = lambda a: jnp.asarray(a.astype(np.float32))
    return f(cos_r), f(sin_r), f(cos_m), f(sin_m)


def _retention_tables():
    L = REC_CHUNK
    log_gamma = np.log(1.0 - 2.0 ** (-5.0 - np.arange(N_RHEADS, dtype=np.float64)))
    idx = np.arange(L, dtype=np.float64)
    diff = idx[:, None] - idx[None, :]
    dmask = np.where(diff >= 0, np.exp(diff[None] * log_gamma[:, None, None]), 0.0)
    lane_lg = np.repeat(log_gamma, HEAD_DIM)[None, :]
    qw = np.exp((idx + 1.0)[:, None] * lane_lg)
    kw = np.exp((L - 1 - idx)[:, None] * lane_lg)
    cd = np.exp(L * lane_lg)
    f = lambda a: jnp.asarray(a.astype(np.float32))
    return f(dmask), f(qw), f(kw), f(cd)


def _head_avg_matrix():
    h = np.arange(REC_W) // HEAD_DIM
    return jnp.asarray((h[:, None] == h[None, :]).astype(np.float32) / HEAD_DIM, dtype=BF16)


def _inproj_kernel(x_ref, w_ref, o_ref):
    o_ref[...] = _dot(x_ref[...].astype(BF16), w_ref[...])


def _inproj(x2, w_bf16, tm=512):
    t, k = x2.shape
    n = w_bf16.shape[1]
    return pl.pallas_call(
        _inproj_kernel,
        grid=(t // tm,),
        in_specs=[pl.BlockSpec((tm, k), lambda i: (i, 0)),
                  pl.BlockSpec((k, n), lambda i: (0, 0))],
        out_specs=pl.BlockSpec((tm, n), lambda i: (i, 0)),
        out_shape=jax.ShapeDtypeStruct((t, n), F32),
        compiler_params=_cparams("parallel"),
        name="inproj",
    )(x2, w_bf16)


def _retention_kernel(h_ref, cos_ref, sin_ref, dmask_ref, qw_ref, kw_ref, cd_ref, avg_ref, gnw_ref,
                      o_ref, state_ref):
    L = REC_CHUNK
    nb = h_ref.shape[0]
    nc = h_ref.shape[1] // L
    seqs = range(nb)
    heads = range(N_RHEADS)
    lane_head = lax.broadcasted_iota(jnp.int32, (1, REC_W), 1) // HEAD_DIM
    row_head = lax.broadcasted_iota(jnp.int32, (REC_W, REC_W), 0) // HEAD_DIM
    col_head = lax.broadcasted_iota(jnp.int32, (REC_W, REC_W), 1) // HEAD_DIM
    block_diag = row_head == col_head
    state_ref[...] = jnp.zeros_like(state_ref)

    def chunk(c, carry):
        r0 = pl.multiple_of(c * L, L)
        rows = pl.ds(r0, L)
        cos = cos_ref[rows, :]
        sin = sin_ref[rows, :]
        qs, ks, kbs, vbs = [], [], [], []
        for s in seqs:
            q = h_ref[s, rows, 0:REC_W]
            k = h_ref[s, rows, REC_W:2 * REC_W]
            qs.append(q * cos + _swap_halves(q, HEAD_DIM // 2) * sin)
            k = (k * cos + _swap_halves(k, HEAD_DIM // 2) * sin) * (HEAD_DIM ** -0.5)
            ks.append(k)
            kbs.append(k.astype(BF16))
            vbs.append(h_ref[s, rows, 2 * REC_W:3 * REC_W].astype(BF16))
        scores = [[_dot_nt(jnp.where(lane_head == h, qs[s], 0.0).astype(BF16), kbs[s]) for h in heads]
                  for s in seqs]
        probs = [[(scores[s][h] * dmask_ref[h]).astype(BF16) for h in heads] for s in seqs]
        rets = []
        for s in seqs:
            inner = jnp.zeros((L, REC_W), F32)
            for h in heads:
                inner = jnp.where(lane_head == h, _dot(probs[s][h], vbs[s]), inner)
            state = state_ref[s]
            cross = _dot(qs[s].astype(BF16), state.astype(BF16)) * qw_ref[...]
            rets.append(inner + cross)
            kwt = (ks[s] * kw_ref[...]).T.astype(BF16)
            loc = _dot(kwt, vbs[s])
            state_ref[s] = cd_ref[...] * state + jnp.where(block_diag, loc, 0.0)
        for s in seqs:
            g = h_ref[s, rows, 3 * REC_W:4 * REC_W]
            o_ref[s, rows, :] = _silu(g) * _head_norm(rets[s], avg_ref[...], gnw_ref[...])
        return carry

    lax.fori_loop(0, nc, chunk, 0)


def _retention(h3, cos_r, sin_r, gn_w):
    b, s, _ = h3.shape
    dmask, qw, kw, cd = _retention_tables()
    avg = _head_avg_matrix()
    const = lambda shape: pl.BlockSpec(shape, lambda i: (0,) * len(shape))
    return pl.pallas_call(
        _retention_kernel,
        grid=(b // REC_SEQS,),
        in_specs=[pl.BlockSpec((REC_SEQS, s, 4 * REC_W), lambda i: (i, 0, COL_RET // (4 * REC_W))),
                  const((s, REC_W)), const((s, REC_W)),
                  const((N_RHEADS, REC_CHUNK, REC_CHUNK)), const((REC_CHUNK, REC_W)),
                  const((REC_CHUNK, REC_W)), const((1, REC_W)), const((REC_W, REC_W)), const((1, REC_W))],
        out_specs=pl.BlockSpec((REC_SEQS, s, REC_W), lambda i: (i, 0, 0)),
        out_shape=jax.ShapeDtypeStruct((b, s, REC_W), F32),
        scratch_shapes=[pltpu.VMEM((REC_SEQS, REC_W, REC_W), F32)],
        compiler_params=_cparams("parallel"),
        name="retention",
    )(h3, cos_r, sin_r, dmask, qw, kw, cd, avg, gn_w.reshape(1, REC_W))


def _mlstm_kernel(h_ref, misc_ref, convw_ref, convb_ref, bif_ref, avg_ref, gnw_ref,
                  o_ref, c_ref, n_ref, m_ref):
    L = REC_CHUNK
    nb = h_ref.shape[0]
    nc = h_ref.shape[1] // L
    seqs = range(nb)
    heads = range(N_RHEADS)
    lane_head = lax.broadcasted_iota(jnp.int32, (1, REC_W), 1) // HEAD_DIM
    row_head = lax.broadcasted_iota(jnp.int32, (REC_W, REC_W), 0) // HEAD_DIM
    col_head = lax.broadcasted_iota(jnp.int32, (REC_W, REC_W), 1) // HEAD_DIM
    block_diag = row_head == col_head
    n_mask = (lax.broadcasted_iota(jnp.int32, (REC_W, LANES), 0) // HEAD_DIM
              == lax.broadcasted_iota(jnp.int32, (REC_W, LANES), 1))
    lane128 = lax.broadcasted_iota(jnp.int32, (1, LANES), 1)
    gate_lanes = lane128 < N_RHEADS
    ri = lax.broadcasted_iota(jnp.int32, (L, L), 0)
    ci = lax.broadcasted_iota(jnp.int32, (L, L), 1)
    causal = ri >= ci
    tri = causal.astype(BF16)
    row8 = lax.broadcasted_iota(jnp.int32, (8, 2 * REC_W), 0)

    c_ref[...] = jnp.zeros_like(c_ref)
    n_ref[...] = jnp.zeros_like(n_ref)
    m_ref[...] = jnp.full(m_ref.shape, NEG, F32)

    def prepare(s, c, r0, rows):
        x = h_ref[s, rows, 0:2 * REC_W]
        prev = h_ref[s, pl.ds(pl.multiple_of(jnp.maximum(r0 - 8, 0), 8), 8), 0:2 * REC_W]
        prev = jnp.where(c > 0, prev, 0.0)
        acc = x * convw_ref[MLSTM_CONV - 1:MLSTM_CONV, :] + convb_ref[...]
        for sft in range(1, MLSTM_CONV):
            xs = pltpu.roll(x, sft, 0)
            head = jnp.where(row8 < sft, pltpu.roll(prev, sft, 0), xs[0:8, :])
            xs = jnp.concatenate([head, xs[8:, :]], axis=0)
            acc = acc + xs * convw_ref[MLSTM_CONV - 1 - sft:MLSTM_CONV - sft, :]
        qk = _silu(acc)
        q = qk[:, 0:REC_W]
        k = qk[:, REC_W:2 * REC_W] * (HEAD_DIM ** -0.5)

        gates = misc_ref[s, rows, :] + bif_ref[...]
        ic_c = jnp.where(gate_lanes, gates, 0.0)
        fg = pltpu.roll(gates, LANES - N_RHEADS, 1)
        lf_c = jnp.where(gate_lanes, jnp.minimum(fg, 0.0) - jnp.log1p(jnp.exp(-jnp.abs(fg))), 0.0)

        bcum = _dot_exact_rhs_left(tri, lf_c)
        gsum = bcum[L - 1:L, :]
        m_s = m_ref[s]
        a = gsum - bcum + ic_c
        qb = q.astype(BF16)
        c_state = c_ref[s]
        n_state = n_ref[s]
        return dict(q=q, k=k, qb=qb, kb=k.astype(BF16), vb=h_ref[s, rows, 2 * REC_W:3 * REC_W].astype(BF16),
                    bcum=bcum, gsum=gsum, m_s=m_s, a=a, amax=jnp.max(a, axis=0, keepdims=True),
                    rvec=(ic_c - bcum).T,
                    inter=bcum + m_s,
                    c_state=c_state, n_state=n_state,
                    q_c=_dot(qb, c_state.astype(BF16)),
                    q_n=_dot(qb, n_state.astype(BF16)))

    def decay(d, scores):
        cmats, stats = [], []
        for h in heads:
            log_d = jnp.where(causal, d["bcum"][:, h:h + 1] + d["rvec"][h:h + 1, :], NEG)
            inter_h = d["inter"][:, h:h + 1]
            m_t = jnp.maximum(inter_h, jnp.max(log_d, axis=-1, keepdims=True))
            cmat = scores[h] * jnp.exp(log_d - m_t)
            si = jnp.exp(inter_h - m_t)
            den = jnp.sum(cmat, axis=-1, keepdims=True) + si * d["q_n"][:, h:h + 1]
            cmats.append(cmat.astype(BF16))
            stats.append((si, jnp.maximum(jnp.abs(den), jnp.exp(-m_t))))
        return cmats, stats

    def cell(d, cmats, stats):
        hcell = jnp.zeros((L, REC_W), F32)
        for h in heads:
            si, denom = stats[h]
            num = _dot(cmats[h], d["vb"]) + si * d["q_c"]
            hcell = jnp.where(lane_head == h, num / denom, hcell)
        return hcell

    def carry_state(s, d):
        wa = jnp.exp(d["a"] - d["amax"])
        m_new = jnp.maximum(d["gsum"] + d["m_s"], d["amax"])
        sp = jnp.exp(d["gsum"] + d["m_s"] - m_new)
        sl = jnp.exp(d["amax"] - m_new)
        kwt = (d["k"] * _expand_heads(wa, lane_head)).T.astype(BF16)
        loc_c = _dot(kwt, d["vb"])
        loc_n = _dot(kwt, jnp.ones((L, LANES), BF16))
        c_ref[s] = (d["c_state"] * _expand_heads(sp, lane_head)
                    + jnp.where(block_diag, loc_c, 0.0) * _expand_heads(sl, lane_head))
        n_ref[s] = d["n_state"] * sp + jnp.where(n_mask, loc_n, 0.0) * sl
        m_ref[s] = m_new

    def chunk(c, carry):
        r0 = pl.multiple_of(c * L, L)
        rows = pl.ds(r0, L)
        ds = [prepare(s, c, r0, rows) for s in seqs]
        scores = [[_dot_nt(jnp.where(lane_head == h, d["q"], 0.0).astype(BF16), d["kb"]) for h in heads]
                  for d in ds]
        decs = [decay(d, sc) for d, sc in zip(ds, scores)]
        cells = [cell(d, *dec) for d, dec in zip(ds, decs)]
        for s in seqs:
            carry_state(s, ds[s])
        for s in seqs:
            og = h_ref[s, rows, 3 * REC_W:4 * REC_W]
            o_ref[s, rows, :] = _head_norm(_sigmoid(og) * cells[s], avg_ref[...], gnw_ref[...])
        return carry

    lax.fori_loop(0, nc, chunk, 0)


def _dot_exact_rhs_left(m_bf16, x):
    hi, lo = _split2(x)
    return _dot(m_bf16, hi) + _dot(m_bf16, lo)


def _mlstm(h3, conv_w, conv_b, b_i, b_f, gn_w):
    b, s, _ = h3.shape
    avg = _head_avg_matrix()
    bif = jnp.zeros((1, LANES), F32).at[0, 0:N_RHEADS].set(b_i).at[0, N_RHEADS:2 * N_RHEADS].set(b_f)
    const = lambda shape: pl.BlockSpec(shape, lambda i: (0,) * len(shape))
    return pl.pallas_call(
        _mlstm_kernel,
        grid=(b // MLSTM_SEQS,),
        in_specs=[pl.BlockSpec((MLSTM_SEQS, s, 4 * REC_W), lambda i: (i, 0, COL_MLSTM // (4 * REC_W))),
                  pl.BlockSpec((MLSTM_SEQS, s, LANES), lambda i: (i, 0, COL_MISC // LANES)),
                  const((MLSTM_CONV, 2 * REC_W)), const((1, 2 * REC_W)), const((1, LANES)),
                  const((REC_W, REC_W)), const((1, REC_W))],
        out_specs=pl.BlockSpec((MLSTM_SEQS, s, REC_W), lambda i: (i, 0, 0)),
        out_shape=jax.ShapeDtypeStruct((b, s, REC_W), F32),
        scratch_shapes=[pltpu.VMEM((MLSTM_SEQS, REC_W, REC_W), F32),
                        pltpu.VMEM((MLSTM_SEQS, REC_W, LANES), F32), pltpu.VMEM((MLSTM_SEQS, 1, LANES), F32)],
        compiler_params=_cparams("parallel"),
        name="mlstm",
    )(h3, h3, conv_w, conv_b.reshape(1, -1), bif, avg, gn_w.reshape(1, REC_W))


def _mla_prep_kernel(cq_ref, ckv_ref, cos_ref, sin_ref, qnw_ref, kvnw_ref, wuq_ref, wuk_ref, wuvt_ref,
                     q_ref, k_ref, vt_ref):
    scale = (MLA_NOPE + MLA_ROPE) ** -0.5 * LOG2_E
    cos = cos_ref[...]
    sin = sin_ref[...]
    cos_all = jnp.concatenate([cos] * MLA_HEADS, axis=1)
    sin_all = jnp.concatenate([sin] * MLA_HEADS, axis=1)

    cq = cq_ref[...]
    cqn = cq * lax.rsqrt(jnp.mean(cq * cq, axis=-1, keepdims=True) + LN_EPS) * qnw_ref[...]
    cqb = cqn.astype(BF16)
    q = _dot(cqb, wuq_ref[0])
    q = q * cos_all + _dot(cqb, wuq_ref[1]) * sin_all
    qb = (q * scale).astype(BF16)
    for j in range(MLA_HEADS):
        q_ref[j] = qb[:, j * LANES:(j + 1) * LANES]

    ckv = ckv_ref[:, 0:MLA_KV_RANK]
    kvn = ckv * lax.rsqrt(jnp.mean(ckv * ckv, axis=-1, keepdims=True) + LN_EPS) * kvnw_ref[...]
    kvb = kvn.astype(BF16)
    misc = ckv_ref[:, MLA_KV_RANK:MLA_KV_RANK + LANES]
    lane = lax.broadcasted_iota(jnp.int32, (1, LANES), 1)
    kr = misc * cos + _swap_halves(misc, MLA_ROPE // 2) * sin
    kr = jnp.where((lane >= MISC_KR) & (lane < MISC_KR + MLA_ROPE), kr, 0.0)
    k = _dot(kvb, wuk_ref[...]) + jnp.concatenate([kr] * MLA_HEADS, axis=1)
    kb = k.astype(BF16)
    for j in range(MLA_HEADS):
        k_ref[j] = kb[:, j * LANES:(j + 1) * LANES]
    vt = _dot_nt(wuvt_ref[...], kvb).astype(BF16)
    ones = jnp.ones((ATT_VROWS - MLA_V, ATT_TILE), BF16)
    for j in range(vt_ref.shape[0]):
        for h in range(MLA_HEADS):
            vt_ref[j, h * ATT_VROWS:h * ATT_VROWS + MLA_V, :] = vt[h * MLA_V:(h + 1) * MLA_V,
                                                                   j * ATT_TILE:(j + 1) * ATT_TILE]
            vt_ref[j, h * ATT_VROWS + MLA_V:(h + 1) * ATT_VROWS, :] = ones


def _mla_prep(h2, seq, cos_m, sin_m, qn_w, kvn_w, wuq, wuk, wuvt, tm=512):
    t = h2.shape[0]
    nblk = seq // tm
    hw = MLA_HEADS * LANES
    const = lambda shape: pl.BlockSpec(shape, lambda i: (0,) * len(shape))
    return pl.pallas_call(
        _mla_prep_kernel,
        grid=(t // tm,),
        in_specs=[pl.BlockSpec((tm, MLA_Q_RANK), lambda i: (i, COL_CQ // MLA_Q_RANK)),
                  pl.BlockSpec((tm, 2 * LANES), lambda i: (i, COL_CKV // (2 * LANES))),
                  pl.BlockSpec((tm, LANES), lambda i: (i % nblk, 0)),
                  pl.BlockSpec((tm, LANES), lambda i: (i % nblk, 0)),
                  const((1, MLA_Q_RANK)), const((1, MLA_KV_RANK)),
                  const((2, MLA_Q_RANK, hw)), const((MLA_KV_RANK, hw)), const((MLA_W, MLA_KV_RANK))],
        out_specs=[pl.BlockSpec((MLA_HEADS, tm, LANES), lambda i: (0, i, 0)),
                   pl.BlockSpec((MLA_HEADS, tm, LANES), lambda i: (0, i, 0)),
                   pl.BlockSpec((tm // ATT_TILE, MLA_HEADS * ATT_VROWS, ATT_TILE), lambda i: (i, 0, 0))],
        out_shape=[jax.ShapeDtypeStruct((MLA_HEADS, t, LANES), BF16),
                   jax.ShapeDtypeStruct((MLA_HEADS, t, LANES), BF16),
                   jax.ShapeDtypeStruct((t // ATT_TILE, MLA_HEADS * ATT_VROWS, ATT_TILE), BF16)],
        compiler_params=_cparams("parallel"),
        name="mla_prep",
    )(h2, h2, cos_m, sin_m, qn_w.reshape(1, -1), kvn_w.reshape(1, -1), wuq, wuk, wuvt)


def _mla_attn_kernel(q_ref, k_ref, vt_ref, o_ref):
    tq = ATT_TILE
    qi = pl.program_id(1)
    key = lax.broadcasted_iota(jnp.int32, (tq, tq), 0)
    qry = lax.broadcasted_iota(jnp.int32, (tq, tq), 1)
    diag_ok = key < (qry // MASK_CHUNK + 1) * MASK_CHUNK

    def step(kt, carry, masked):
        rows = pl.ds(pl.multiple_of(kt * tq, tq), tq)
        sts = []
        for j in range(MLA_HEADS):
            sts.append(_dot_nt(k_ref[j, rows, :], q_ref[j]))
        new = []
        for g0 in range(0, MLA_HEADS, ATT_GROUP):
            pts = []
            for j in range(g0, g0 + ATT_GROUP):
                m = carry[j][0]
                st = sts[j]
                if masked:
                    st = jnp.where(diag_ok, st, NEG)
                m_new = jnp.maximum(m, jnp.max(st, axis=0, keepdims=True))
                pts.append((m_new, jnp.exp2(m - m_new), jnp.exp2(st - m_new).astype(BF16)))
            for j, (m_new, alpha, pt) in zip(range(g0, g0 + ATT_GROUP), pts):
                acc = alpha * carry[j][1] + _dot(vt_ref[kt, j * ATT_VROWS:(j + 1) * ATT_VROWS, :], pt)
                new.append((m_new, acc))
        return tuple(new)

    init = tuple((jnp.full((1, tq), NEG, F32), jnp.zeros((ATT_VROWS, tq), F32)) for _ in range(MLA_HEADS))
    carry = lax.fori_loop(0, qi, lambda kt, c: step(kt, c, False), init)
    carry = step(qi, carry, True)
    outs = [acc[0:MLA_V, :] / acc[MLA_V:MLA_V + 1, :] for _, acc in carry]
    for p in range(MLA_HEADS // 2):
        o_ref[:, p * LANES:(p + 1) * LANES] = jnp.concatenate([outs[2 * p], outs[2 * p + 1]], axis=0).T


def _mla_attn(q, k, vt, batch, seq):
    t = q.shape[1]
    nq = seq // ATT_TILE
    return pl.pallas_call(
        _mla_attn_kernel,
        grid=(batch, nq),
        in_specs=[pl.BlockSpec((MLA_HEADS, ATT_TILE, LANES), lambda b, i: (0, b * nq + i, 0)),
                  pl.BlockSpec((MLA_HEADS, seq, LANES), lambda b, i: (0, b, 0)),
                  pl.BlockSpec((nq, MLA_HEADS * ATT_VROWS, ATT_TILE), lambda b, i: (b, 0, 0))],
        out_specs=pl.BlockSpec((ATT_TILE, MLA_W), lambda b, i: (b * nq + i, 0)),
        out_shape=jax.ShapeDtypeStruct((t, MLA_W), F32),
        compiler_params=_cparams("parallel", "arbitrary"),
        name="mla_attn",
    )(q, k, vt)


def _outproj_kernel(ret_ref, att_ref, ml_ref, x_ref, w_ref, lnw_ref, lnb_ref, o_ref):
    mix = (_dot(ret_ref[...].astype(BF16), w_ref[0:REC_W, :])
           + _dot(att_ref[...].astype(BF16), w_ref[REC_W:REC_W + MLA_W, :])
           + _dot(ml_ref[...].astype(BF16), w_ref[REC_W + MLA_W:, :]))
    o_ref[...] = _layer_norm(DN_ALPHA * x_ref[...] + mix, lnw_ref[...], lnb_ref[...])


def _outproj_ln(ret, att, ml, x2, w_bf16, ln_w, ln_b, tm=512):
    t = x2.shape[0]
    const = lambda shape: pl.BlockSpec(shape, lambda i: (0,) * len(shape))
    row = lambda w: pl.BlockSpec((tm, w), lambda i: (i, 0))
    return pl.pallas_call(
        _outproj_kernel,
        grid=(t // tm,),
        in_specs=[row(REC_W), row(MLA_W), row(REC_W), row(D_MODEL),
                  const((D_MODEL, D_MODEL)), const((1, D_MODEL)), const((1, D_MODEL))],
        out_specs=row(D_MODEL),
        out_shape=jax.ShapeDtypeStruct((t, D_MODEL), F32),
        compiler_params=_cparams("parallel"),
        name="outproj_ln",
    )(ret, att, ml, x2, w_bf16, ln_w.reshape(1, -1), ln_b.reshape(1, -1))


MXU_COLS = 256
SWIGLU_GROUP = 4


def _swiglu_chunk(xb, wg_ref, wu_ref, wd_ref):
    tf = wg_ref.shape[1]
    parts = [slice(s, min(s + MXU_COLS, tf)) for s in range(0, tf, MXU_COLS)]
    out = None
    for g0 in range(0, len(parts), SWIGLU_GROUP):
        group = parts[g0:g0 + SWIGLU_GROUP]
        gu = [(_dot(xb, wg_ref[:, p].astype(BF16)), _dot(xb, wu_ref[:, p].astype(BF16))) for p in group]
        hs = [(_silu(g) * u).astype(BF16) for g, u in gu]
        for h, p in zip(hs, group):
            y = _dot(h, wd_ref[p, :].astype(BF16))
            out = y if out is None else out + y
    return out


def _ffn_kernel(x_ref, wg_ref, wu_ref, wd_ref, lnw_ref, lnb_ref, o_ref):
    x = x_ref[...]
    y = _swiglu_chunk(x.astype(BF16), wg_ref, wu_ref, wd_ref)
    o_ref[...] = _layer_norm(DN_ALPHA * x + y, lnw_ref[...], lnb_ref[...])


def _ffn_ln(x2, wg, wu, wd, ln_w, ln_b, tm=1024):
    t = x2.shape[0]
    dff = wg.shape[1]
    once = lambda shape: pl.BlockSpec(shape, lambda i: (0,) * len(shape), pipeline_mode=pl.Buffered(1))
    return pl.pallas_call(
        _ffn_kernel,
        grid=(t // tm,),
        in_specs=[pl.BlockSpec((tm, D_MODEL), lambda i: (i, 0)),
                  once((D_MODEL, dff)), once((D_MODEL, dff)), once((dff, D_MODEL)),
                  once((1, D_MODEL)), once((1, D_MODEL))],
        out_specs=pl.BlockSpec((tm, D_MODEL), lambda i: (i, 0)),
        out_shape=jax.ShapeDtypeStruct((t, D_MODEL), F32),
        compiler_params=_cparams("parallel"),
        name="ffn_ln",
    )(x2, wg, wu, wd, ln_w.reshape(1, -1), ln_b.reshape(1, -1))


RT_E1, RT_E2, RT_W1, RT_W2, RT_R1, RT_R2 = 0, 1, 2, 3, 4, 5


def _router_kernel(x_ref, r_ref, route_ref, count_ref, carry_ref):
    @pl.when(pl.program_id(0) == 0)
    def _():
        carry_ref[...] = jnp.zeros_like(carry_ref)

    x = x_ref[...]
    xh = x.astype(BF16)
    xm = (x - xh.astype(F32)).astype(BF16)
    both = _dot(xh, r_ref[...])
    logits = both[:, 0:LANES] + (both[:, LANES:2 * LANES] + _dot(xm, r_ref[:, 0:LANES]))
    tm = logits.shape[0]
    lane = lax.broadcasted_iota(jnp.int32, logits.shape, 1)
    lg = jnp.where(lane < N_EXPERTS, logits, -jnp.inf)
    m1 = jnp.max(lg, axis=-1, keepdims=True)
    i1 = jnp.min(jnp.where(lg == m1, lane, LANES), axis=-1, keepdims=True)
    lg2 = jnp.where(lane == i1, -jnp.inf, lg)
    m2 = jnp.max(lg2, axis=-1, keepdims=True)
    i2 = jnp.min(jnp.where(lg2 == m2, lane, LANES), axis=-1, keepdims=True)
    e2 = jnp.exp(m2 - m1)
    w1 = 1.0 / (1.0 + e2)
    w2 = e2 / (1.0 + e2)

    sel = ((lane == i1) | (lane == i2)).astype(F32)
    before = (lax.broadcasted_iota(jnp.int32, (tm, tm), 0) > lax.broadcasted_iota(jnp.int32, (tm, tm), 1))
    ranks = _dot(before.astype(BF16), sel.astype(BF16)) + carry_ref[...]
    r1 = jnp.sum(jnp.where(lane == i1, ranks, 0.0), axis=-1, keepdims=True)
    r2 = jnp.sum(jnp.where(lane == i2, ranks, 0.0), axis=-1, keepdims=True)
    carry_ref[...] += jnp.sum(sel, axis=0, keepdims=True)
    count_ref[...] = carry_ref[...]

    rec = jnp.zeros(logits.shape, F32)
    for pos, val in ((RT_E1, i1.astype(F32)), (RT_E2, i2.astype(F32)), (RT_W1, w1), (RT_W2, w2),
                     (RT_R1, r1), (RT_R2, r2)):
        rec = jnp.where(lane == pos, val, rec)
    route_ref[...] = rec


def _router(x2, router, tm=1024):
    t = x2.shape[0]
    rp = jnp.zeros((D_MODEL, LANES), F32).at[:, 0:N_EXPERTS].set(router)
    rh = rp.astype(BF16)
    rm = (rp - rh.astype(F32)).astype(BF16)
    r3 = jnp.concatenate([rh, rm], axis=1)
    return pl.pallas_call(
        _router_kernel,
        grid=(t // tm,),
        in_specs=[pl.BlockSpec((tm, D_MODEL), lambda i: (i, 0)),
                  pl.BlockSpec((D_MODEL, 2 * LANES), lambda i: (0, 0))],
        out_specs=[pl.BlockSpec((tm, LANES), lambda i: (i, 0)),
                   pl.BlockSpec((1, LANES), lambda i: (0, 0))],
        out_shape=[jax.ShapeDtypeStruct((t, LANES), F32), jax.ShapeDtypeStruct((1, LANES), F32)],
        scratch_shapes=[pltpu.VMEM((1, LANES), F32)],
        compiler_params=_cparams("arbitrary"),
        name="router",
    )(x2, r3)


MOE_TM = 1024


def _route_plan(route, counts, t):
    cnt = counts[0, :N_EXPERTS].astype(jnp.int32)
    padded = ((cnt + MOE_TM - 1) // MOE_TM) * MOE_TM
    ends = jnp.cumsum(padded)
    offs = ends - padded
    experts = jnp.arange(N_EXPERTS, dtype=jnp.int32)

    def dest(e_lane, r_lane):
        e = route[:, e_lane].astype(jnp.int32)
        off = jnp.sum(jnp.where(e[:, None] == experts[None, :], offs[None, :], 0), axis=1)
        return off + route[:, r_lane].astype(jnp.int32)

    n_tiles = (2 * t) // MOE_TM + N_EXPERTS
    n_used = ends[-1] // MOE_TM
    tile = jnp.minimum(jnp.arange(n_tiles, dtype=jnp.int32), n_used - 1)
    tile_expert = jnp.sum((tile[:, None] * MOE_TM >= ends[None, :]).astype(jnp.int32), axis=1)
    group_end = jnp.sum(jnp.where(tile_expert[:, None] == experts[None, :], (offs + cnt)[None, :], 0), axis=1)
    tile_rows = jnp.clip(group_end - tile * MOE_TM, 0, MOE_TM)
    return dest(RT_E1, RT_R1), dest(RT_E2, RT_R2), tile_expert, n_used.reshape(1), tile_rows, n_tiles


def _dispatch_kernel(d1_ref, d2_ref, x_ref, zero_ref, xs_ref, sem):
    del zero_ref
    tm = x_ref.shape[0]
    base = pl.program_id(0) * tm

    def copies(r):
        src = x_ref.at[pl.ds(r, 1), :]
        return (pltpu.make_async_copy(src, xs_ref.at[pl.ds(d1_ref[base + r], 1), :], sem),
                pltpu.make_async_copy(src, xs_ref.at[pl.ds(d2_ref[base + r], 1), :], sem))

    for r in range(tm):
        for cp in copies(r):
            cp.start(priority=r % 2)
    for r in range(tm):
        for cp in copies(r):
            cp.wait()


def _dispatch(x2, d1, d2, n_rows, tm=512):
    t = x2.shape[0]
    zeros = jnp.zeros((n_rows, D_MODEL), F32)
    return pl.pallas_call(
        _dispatch_kernel,
        grid_spec=pltpu.PrefetchScalarGridSpec(
            num_scalar_prefetch=2,
            grid=(t // tm,),
            in_specs=[pl.BlockSpec((tm, D_MODEL), lambda i, d1, d2: (i, 0)),
                      pl.BlockSpec(memory_space=pl.ANY)],
            out_specs=pl.BlockSpec(memory_space=pl.ANY),
            scratch_shapes=[pltpu.SemaphoreType.DMA(())]),
        out_shape=jax.ShapeDtypeStruct((n_rows, D_MODEL), F32),
        input_output_aliases={3: 0},
        compiler_params=_cparams("arbitrary"),
        name="moe_dispatch",
    )(d1, d2, x2, zeros)


def _moe_ffn_kernel(te_ref, nu_ref, nr_ref, x_ref, wg_ref, wu_ref, wd_ref, o_ref, xb_ref):
    del te_ref
    i = pl.program_id(0)
    f = pl.program_id(1)
    used = i < nu_ref[0]

    @pl.when((f == 0) | jnp.logical_not(used))
    def _():
        o_ref[...] = jnp.zeros_like(o_ref)

    def swiglu_rows(rows):
        @pl.when(f == 0)
        def _():
            xb_ref[0:rows, :] = x_ref[0:rows, :].astype(BF16)

        o_ref[0:rows, :] += _swiglu_chunk(xb_ref[0:rows, :], wg_ref, wu_ref, wd_ref)

    quarters = pl.cdiv(nr_ref[i], MOE_TM // 4)
    for nq in range(1, 5):
        @pl.when(used & (quarters == nq))
        def _(nq=nq):
            swiglu_rows(nq * (MOE_TM // 4))


def _moe_ffn(xs, tile_expert, n_used, tile_rows, n_tiles, wg, wu, wd, tf=512):
    dff = wg.shape[2]
    nf = dff // tf
    row = lambda i, f, te, nu, nr: (jnp.minimum(i, nu[0] - 1), 0)
    fcl = lambda i, f, nu: jnp.where(i < nu[0], f, nf - 1)
    return pl.pallas_call(
        _moe_ffn_kernel,
        grid_spec=pltpu.PrefetchScalarGridSpec(
            num_scalar_prefetch=3,
            grid=(n_tiles, nf),
            in_specs=[pl.BlockSpec((MOE_TM, D_MODEL), row),
                      pl.BlockSpec((None, D_MODEL, tf), lambda i, f, te, nu, nr: (te[i], 0, fcl(i, f, nu))),
                      pl.BlockSpec((None, D_MODEL, tf), lambda i, f, te, nu, nr: (te[i], 0, fcl(i, f, nu))),
                      pl.BlockSpec((None, tf, D_MODEL), lambda i, f, te, nu, nr: (te[i], fcl(i, f, nu), 0))],
            out_specs=pl.BlockSpec((MOE_TM, D_MODEL), lambda i, f, te, nu, nr: (i, 0)),
            scratch_shapes=[pltpu.VMEM((MOE_TM, D_MODEL), BF16)]),
        out_shape=jax.ShapeDtypeStruct(xs.shape, F32),
        compiler_params=_cparams("arbitrary", "arbitrary"),
        name="moe_ffn",
    )(tile_expert, n_used, tile_rows, xs, wg, wu, wd)


def _combine_kernel(d1_ref, d2_ref, x_ref, route_ref, ys_ref, lnw_ref, lnb_ref, o_ref, b1_ref, b2_ref, sem):
    tm = x_ref.shape[0]
    i = pl.program_id(0)
    n = pl.num_programs(0)

    def copies(step, r):
        slot = step % 2
        base = step * tm
        return (pltpu.make_async_copy(ys_ref.at[pl.ds(d1_ref[base + r], 1), :],
                                      b1_ref.at[slot, pl.ds(r, 1), :], sem.at[slot]),
                pltpu.make_async_copy(ys_ref.at[pl.ds(d2_ref[base + r], 1), :],
                                      b2_ref.at[slot, pl.ds(r, 1), :], sem.at[slot]))

    def start_tile(step):
        for r in range(tm):
            for cp in copies(step, r):
                cp.start(priority=r % 2)

    def wait_tile(step):
        for r in range(tm):
            for cp in copies(step, r):
                cp.wait()

    @pl.when(i == 0)
    def _():
        start_tile(i)

    @pl.when(i + 1 < n)
    def _():
        start_tile(i + 1)

    wait_tile(i)
    slot = i % 2
    rt = route_ref[...]
    y = rt[:, RT_W1:RT_W1 + 1] * b1_ref[slot] + rt[:, RT_W2:RT_W2 + 1] * b2_ref[slot]
    o_ref[...] = _layer_norm(DN_ALPHA * x_ref[...] + y, lnw_ref[...], lnb_ref[...])


def _combine_ln(x2, route, ys, d1, d2, ln_w, ln_b, tm=256):
    t = x2.shape[0]
    const = lambda shape: pl.BlockSpec(shape, lambda i, d1, d2: (0,) * len(shape))
    return pl.pallas_call(
        _combine_kernel,
        grid_spec=pltpu.PrefetchScalarGridSpec(
            num_scalar_prefetch=2,
            grid=(t // tm,),
            in_specs=[pl.BlockSpec((tm, D_MODEL), lambda i, d1, d2: (i, 0)),
                      pl.BlockSpec((tm, LANES), lambda i, d1, d2: (i, 0)),
                      pl.BlockSpec(memory_space=pl.ANY),
                      const((1, D_MODEL)), const((1, D_MODEL))],
            out_specs=pl.BlockSpec((tm, D_MODEL), lambda i, d1, d2: (i, 0)),
            scratch_shapes=[pltpu.VMEM((2, tm, D_MODEL), F32), pltpu.VMEM((2, tm, D_MODEL), F32),
                            pltpu.SemaphoreType.DMA((2,))]),
        out_shape=jax.ShapeDtypeStruct((t, D_MODEL), F32),
        compiler_params=_cparams("arbitrary"),
        name="moe_combine_ln",
    )(d1, d2, x2, route, ys, ln_w.reshape(1, -1), ln_b.reshape(1, -1))


def _moe_ln(x2, router, wg, wu, wd, ln_w, ln_b):
    t = x2.shape[0]
    route, counts = _router(x2, router)
    d1, d2, tile_expert, n_used, tile_rows, n_tiles = _route_plan(route, counts, t)
    xs = _dispatch(x2, d1, d2, n_tiles * MOE_TM)
    ys = _moe_ffn(xs, tile_expert, n_used, tile_rows, n_tiles, wg, wu, wd)
    return _combine_ln(x2, route, ys, d1, d2, ln_w, ln_b)


def _layout_w_in(w):
    sizes = (REC_W, REC_W, REC_W, REC_W, MLA_Q_RANK, MLA_KV_RANK, MLA_ROPE,
             REC_W, REC_W, REC_W, REC_W, N_RHEADS, N_RHEADS)
    offs = np.concatenate([[0], np.cumsum(sizes)])
    part = lambda j: w[:, offs[j]:offs[j + 1]]
    z = lambda n: jnp.zeros((w.shape[0], n), w.dtype)
    misc = jnp.concatenate([part(11), part(12), z(MISC_KR - 2 * N_RHEADS), part(6),
                            z(LANES - MISC_KR - MLA_ROPE)], axis=1)
    cols = [part(0), part(1), part(2), part(3), part(7), part(8), part(9), part(10), part(4), part(5), misc]
    return jnp.concatenate(cols, axis=1).astype(BF16)


def _layout_mla(w_uq, w_ukv):
    r = w_uq.shape[0]
    uq = w_uq.reshape(r, MLA_HEADS, MLA_NOPE + MLA_ROPE)
    half = MLA_ROPE // 2
    rope_sw = jnp.concatenate([uq[:, :, MLA_NOPE + half:], uq[:, :, MLA_NOPE:MLA_NOPE + half]], axis=-1)
    uq_sw = jnp.pad(rope_sw, ((0, 0), (0, 0), (MLA_NOPE, LANES - MLA_NOPE - MLA_ROPE)))
    uq = jnp.pad(uq, ((0, 0), (0, 0), (0, LANES - MLA_NOPE - MLA_ROPE)))
    uq = jnp.stack([uq.reshape(r, MLA_HEADS * LANES), uq_sw.reshape(r, MLA_HEADS * LANES)])
    r = w_ukv.shape[0]
    ukv = w_ukv.reshape(r, MLA_HEADS, MLA_NOPE + MLA_V)
    uk = jnp.pad(ukv[:, :, :MLA_NOPE], ((0, 0), (0, 0), (0, LANES - MLA_NOPE))).reshape(r, MLA_HEADS * LANES)
    uvt = ukv[:, :, MLA_NOPE:].reshape(r, MLA_W).T
    return uq.astype(BF16), uk.astype(BF16), uvt.astype(BF16)


def kernel(x, w_in, ret_gn_w, mla_q_norm_w, mla_w_uq, mla_kv_norm_w, mla_w_ukv, mlstm_conv_w, mlstm_conv_b,
           mlstm_b_i, mlstm_b_f, mlstm_gn_w, w_out, ln1_w, ln1_b, ffn_w_gate, ffn_w_up, ffn_w_down,
           moe_router, moe_w_gate, moe_w_up, moe_w_down, ln2_w, ln2_b):
    bsz, seq, d = x.shape
    t = bsz * seq
    cos_r, sin_r, cos_m, sin_m = _rope_tables(seq)
    x2 = x.reshape(t, d)
    for l in range(DEPTH):
        h2 = _inproj(x2, _layout_w_in(w_in[l]))
        h3 = h2.reshape(bsz, seq, D_IN_PAD)
        ret = _retention(h3, cos_r, sin_r, ret_gn_w[l]).reshape(t, REC_W)
        ml = _mlstm(h3, mlstm_conv_w[l], mlstm_conv_b[l], mlstm_b_i[l], mlstm_b_f[l],
                    mlstm_gn_w[l]).reshape(t, REC_W)
        wuq, wuk, wuvt = _layout_mla(mla_w_uq[l], mla_w_ukv[l])
        q, k, vt = _mla_prep(h2, seq, cos_m, sin_m, mla_q_norm_w[l], mla_kv_norm_w[l], wuq, wuk, wuvt)
        att = _mla_attn(q, k, vt, bsz, seq)
        x2 = _outproj_ln(ret, att, ml, x2, w_out[l].astype(BF16), ln1_w[l], ln1_b[l])
        if l % 2 == 0:
            j = l // 2
            x2 = _ffn_ln(x2, ffn_w_gate[j].astype(BF16), ffn_w_up[j].astype(BF16),
                         ffn_w_down[j].astype(BF16), ln2_w[l], ln2_b[l])
        else:
            j = l // 2
            x2 = _moe_ln(x2, moe_router[j], moe_w_gate[j], moe_w_up[j], moe_w_down[j], ln2_w[l], ln2_b[l])
    return x2.reshape(bsz, seq, d)
```
